```python
import math
import numpy as np
import jax
import jax.numpy as jnp
from jax import lax

D_MODEL = 1024
BATCH = 2
SEQ = 8192
DEPTH = 1

GRID_W = 64
CTX_LEN = 256
EPS = 1e-6
GDN_HEADS = 8
GDN_DK = 128
GDN_DV = 128
GDN_WIDTH = GDN_HEADS * GDN_DV
CONV_K = 5
CHUNK = 64
DIFF_HEADS = 8
DIFF_DQK = D_MODEL // DIFF_HEADS // 2
DIFF_DV = 2 * DIFF_DQK
DIFF_QK_WIDTH = 2 * DIFF_HEADS * DIFF_DQK
DIFF_V_WIDTH = DIFF_HEADS * DIFF_DV
Q_BLOCK = 128
ROPE_THETA = 10000.0
IN_SIZES = (3 * GDN_WIDTH, GDN_WIDTH, 2 * GDN_HEADS, 2 * GDN_HEADS,
            DIFF_QK_WIDTH, DIFF_QK_WIDTH, DIFF_V_WIDTH, DIFF_V_WIDTH, 2 * D_MODEL)
IN_COLS = 4 * GDN_WIDTH + 4 * GDN_HEADS + 2 * DIFF_QK_WIDTH + 2 * DIFF_V_WIDTH + 2 * D_MODEL

kernel_name = 'hybrid_gdn_diffattn_prefix_block'


def rms_norm(x, gain=None):
    xf = x.astype(jnp.float32)
    y = xf * lax.rsqrt(jnp.mean(xf * xf, axis=-1, keepdims=True) + EPS)
    if gain is not None:
        y = y * gain.astype(jnp.float32)
    return y.astype(x.dtype)


def l2_norm(x):
    return x * lax.rsqrt(jnp.sum(x * x, axis=-1, keepdims=True) + EPS)


def split_cols(cols):
    return jnp.split(cols, np.cumsum(IN_SIZES)[:-1].tolist(), axis=-1)


def centred_conv(x, w):
    pad = CONV_K // 2
    length = x.shape[1]
    xp = jnp.pad(x, ((0, 0), (pad, pad), (0, 0)))
    return sum(xp[:, j:j + length] * w[j] for j in range(CONV_K))


def axial_rope_tables(length):
    rows = length // GRID_W
    row = jnp.broadcast_to(jnp.arange(rows)[:, None], (rows, GRID_W)).reshape(-1).astype(jnp.float32)
    col = jnp.broadcast_to(jnp.arange(GRID_W)[None, :], (rows, GRID_W)).reshape(-1).astype(jnp.float32)
    n_freq = DIFF_DQK // 4
    inv_freq = ROPE_THETA ** (-jnp.arange(n_freq, dtype=jnp.float32) / n_freq)
    ang_r = row[:, None] * inv_freq
    ang_c = col[:, None] * inv_freq
    return (jnp.cos(ang_r), jnp.sin(ang_r), jnp.cos(ang_c), jnp.sin(ang_c))


def rotate(x, cos, sin):
    x1, x2 = jnp.split(x, 2, axis=-1)
    return jnp.concatenate([x1 * cos - x2 * sin, x2 * cos + x1 * sin], axis=-1)


def axial_rope(x, tables):
    cos_r, sin_r, cos_c, sin_c = [t[None, :, None, None, :].astype(x.dtype) for t in tables]
    xr, xc = jnp.split(x, 2, axis=-1)
    return jnp.concatenate([rotate(xr, cos_r, sin_r), rotate(xc, cos_c, sin_c)], axis=-1)


def gdn_prepare(qkv, a_raw, b_raw, conv_w, a_log, dt_bias):
    b, length, _ = qkv.shape
    qkv = jax.nn.silu(centred_conv(qkv, conv_w)).astype(jnp.float32)
    q, k, v = jnp.split(qkv, 3, axis=-1)
    q = l2_norm(q.reshape(b, length, GDN_HEADS, GDN_DK)) * (GDN_DK ** -0.5)
    k = l2_norm(k.reshape(b, length, GDN_HEADS, GDN_DK))
    v = v.reshape(b, length, GDN_HEADS, GDN_DV)
    g = -jnp.exp(a_log.astype(jnp.float32)) * jax.nn.softplus(
        a_raw.astype(jnp.float32).reshape(b, length, 2, GDN_HEADS) + dt_bias.astype(jnp.float32))
    beta = jax.nn.sigmoid(b_raw.astype(jnp.float32).reshape(b, length, 2, GDN_HEADS))
    return q, k, v, g, beta


def gdn_chunk_scan(q, k, v, g, beta, s0, with_out):
    b, length, h, _ = q.shape
    n = length // CHUNK

    def chunks(t):
        t = t.reshape((b, n, CHUNK, h) + t.shape[3:])
        return jnp.moveaxis(t, (1, 3), (0, 2))

    kc, vc, bc = chunks(k), chunks(v), chunks(beta)
    gc = jnp.cumsum(chunks(g), axis=-1)
    idx = jnp.arange(CHUNK)
    incl = idx[:, None] >= idx[None, :]
    strict = idx[:, None] > idx[None, :]
    decay = jnp.where(incl, jnp.exp(jnp.where(incl, gc[..., :, None] - gc[..., None, :], 0.0)), 0.0)
    a_mat = jnp.where(strict, jnp.einsum('nbhid,nbhjd->nbhij', kc * bc[..., None], kc) * decay, 0.0)
    eye = jnp.eye(CHUNK, dtype=a_mat.dtype)
    t_mat = lax.linalg.triangular_solve(eye + a_mat, jnp.broadcast_to(eye, a_mat.shape),
                                        left_side=True, lower=True)
    u = t_mat @ (vc * bc[..., None])
    w = t_mat @ (kc * (bc * jnp.exp(gc))[..., None])
    g_last = gc[..., -1]
    k_dec = kc * jnp.exp(g_last[..., None] - gc)[..., None]
    xs = (u, w, k_dec, g_last)
    if with_out:
        qc = chunks(q)
        qk = jnp.where(incl, jnp.einsum('nbhid,nbhjd->nbhij', qc, kc) * decay, 0.0)
        xs = xs + (qc * jnp.exp(gc)[..., None], qk)

    def step(s, xs_i):
        u_i, w_i, kd_i, gl_i = xs_i[:4]
        v_new = u_i - jnp.einsum('bhck,bhkv->bhcv', w_i, s)
        s_next = s * jnp.exp(gl_i)[..., None, None] + jnp.einsum('bhck,bhcv->bhkv', kd_i, v_new)
        if with_out:
            qd_i, qk_i = xs_i[4:]
            o = jnp.einsum('bhck,bhkv->bhcv', qd_i, s) + jnp.einsum('bhij,bhjv->bhiv', qk_i, v_new)
            return s_next, o
        return s_next, None

    s_final, o = lax.scan(step, s0, xs)
    if with_out:
        o = jnp.moveaxis(o, (0, 2), (1, 3)).reshape(b, length, h, v.shape[-1])
    return s_final, o


def flip_if(t, rev):
    return t[:, ::-1] if rev else t


def gdn_bidirectional(ctx_in, lat_in, with_ctx_out):
    q_c, k_c, v_c, g_c, beta_c = ctx_in
    q_l, k_l, v_l, g_l, beta_l = lat_in
    b = q_l.shape[0]
    o_lat = 0.0
    o_ctx = 0.0
    for d in range(2):
        rev = d == 1
        s0 = jnp.zeros((b, GDN_HEADS, GDN_DK, GDN_DV), jnp.float32)
        s_ctx, oc = gdn_chunk_scan(flip_if(q_c, rev), flip_if(k_c, rev), flip_if(v_c, rev),
                                   flip_if(g_c[:, :, d], rev), flip_if(beta_c[:, :, d], rev), s0, with_ctx_out)
        _, ol = gdn_chunk_scan(flip_if(q_l, rev), flip_if(k_l, rev), flip_if(v_l, rev),
                               flip_if(g_l[:, :, d], rev), flip_if(beta_l[:, :, d], rev), s_ctx, True)
        o_lat = o_lat + flip_if(ol, rev)
        if with_ctx_out:
            o_ctx = o_ctx + flip_if(oc, rev)
    return o_lat, (o_ctx if with_ctx_out else None)


def diff_heads(raw, gain, rope):
    b, length, _ = raw.shape
    t = rms_norm(raw.reshape(b, length, DIFF_HEADS, 2, DIFF_DQK), gain)
    if rope is not None:
        t = axial_rope(t, rope)
    return t


def diff_attend(q, k, v, lam):
    b, lq = q.shape[:2]
    nb = lq // Q_BLOCK
    qb = jnp.moveaxis(q.reshape((b, nb, Q_BLOCK) + q.shape[2:]), 1, 0)
    scale = DIFF_DQK ** -0.5

    def block(qi):
        s = jnp.einsum('bqhcd,bkhcd->bhcqk', qi, k).astype(jnp.float32) * scale
        p = jax.nn.softmax(s, axis=-1)
        attn = p[:, :, 0] - lam * p[:, :, 1]
        return jnp.einsum('bhqk,bkhv->bqhv', attn.astype(v.dtype), v)

    o = lax.map(block, qb)
    return jnp.moveaxis(o, 0, 1).reshape((b, lq) + v.shape[2:])


def mixer_output(o_a, z_a, o_b, z_b, gate_cols, gdn_norm_w, diff_norm_w, lambda_init, w_oa, w_ob, w_out):
    b, length = z_a.shape[:2]
    y_a = rms_norm(o_a.astype(z_a.dtype), gdn_norm_w) * jax.nn.silu(z_a.reshape(b, length, GDN_HEADS, GDN_DV))
    y_b = rms_norm(o_b, diff_norm_w) * (1.0 - lambda_init) * jax.nn.silu(z_b.reshape(b, length, DIFF_HEADS, DIFF_DV))
    g_a, g_b = jnp.split(jax.nn.sigmoid(gate_cols), 2, axis=-1)
    merged = g_a * (y_a.reshape(b, length, GDN_WIDTH) @ w_oa) + g_b * (y_b.reshape(b, length, DIFF_V_WIDTH) @ w_ob)
    return merged @ w_out


def hybrid_layer(x, ctx, c, c_ctx, layer, w_ada, b_ada, w_in, conv_w, a_log, dt_bias, gdn_norm_w,
                 q_norm_w, k_norm_w, lambda_q1, lambda_k1, lambda_q2, lambda_k2, diff_norm_w,
                 w_oa, w_ob, w_out, rope, update_ctx):
    b, length, _ = x.shape
    n_ctx = ctx.shape[1]
    shift, scale, gate = jnp.split(jax.nn.silu(c) @ w_ada + b_ada, 3, axis=-1)
    shift_c, scale_c, gate_c = jnp.split(jax.nn.silu(c_ctx) @ w_ada + b_ada, 3, axis=-1)
    h = rms_norm(x) * (1.0 + scale[:, None]) + shift[:, None]
    hc = rms_norm(ctx) * (1.0 + scale_c) + shift_c
    qkv_a, z_a, a_raw, b_raw, q_b, k_b, v_b, z_b, gates = split_cols(h @ w_in)
    qkv_ac, z_ac, a_rawc, b_rawc, q_bc, k_bc, v_bc, z_bc, gates_c = split_cols(hc @ w_in)

    o_a, o_ac = gdn_bidirectional(gdn_prepare(qkv_ac, a_rawc, b_rawc, conv_w, a_log, dt_bias),
                                  gdn_prepare(qkv_a, a_raw, b_raw, conv_w, a_log, dt_bias), update_ctx)

    lambda_init = 0.8 - 0.6 * math.exp(-0.3 * layer)
    lam = (jnp.exp(jnp.sum(lambda_q1.astype(jnp.float32) * lambda_k1.astype(jnp.float32)))
           - jnp.exp(jnp.sum(lambda_q2.astype(jnp.float32) * lambda_k2.astype(jnp.float32))) + lambda_init)
    q_l = diff_heads(q_b, q_norm_w, rope)
    k_l = diff_heads(k_b, k_norm_w, rope)
    k_c = diff_heads(k_bc, k_norm_w, None)
    v_l = v_b.reshape(b, length, DIFF_HEADS, DIFF_DV)
    v_c = v_bc.reshape(b, n_ctx, DIFF_HEADS, DIFF_DV)
    o_b = diff_attend(q_l, jnp.concatenate([k_c, k_l], axis=1), jnp.concatenate([v_c, v_l], axis=1), lam)

    x = x + gate[:, None] * mixer_output(o_a, z_a, o_b, z_b, gates, gdn_norm_w, diff_norm_w,
                                         lambda_init, w_oa, w_ob, w_out)
    if update_ctx:
        q_c = diff_heads(q_bc, q_norm_w, None)
        o_bc = diff_attend(q_c, k_c, v_c, lam)
        ctx = ctx + gate_c * mixer_output(o_ac, z_ac, o_bc, z_bc, gates_c, gdn_norm_w, diff_norm_w,
                                          lambda_init, w_oa, w_ob, w_out)
    return x, ctx


def setup_inputs(seed: int = 0) -> dict:
    key = jax.random.key(seed)
    ks = jax.random.split(key, 21)

    def nrm(k, shape, s):
        return jax.random.normal(k, shape, jnp.float32) * s

    dt = jnp.exp(jax.random.uniform(ks[8], (DEPTH, 2, GDN_HEADS), jnp.float32,
                                    minval=math.log(1e-3), maxval=math.log(1e-1)))
    return {
        'x': nrm(ks[0], (BATCH, SEQ, D_MODEL), 1.0),
        'c': nrm(ks[1], (BATCH, D_MODEL), 1.0),
        'ctx': nrm(ks[2], (BATCH, CTX_LEN, D_MODEL), 1.0),
        'c_ctx': nrm(ks[3], (D_MODEL,), 1.0),
        'w_ada': nrm(ks[4], (DEPTH, D_MODEL, 3 * D_MODEL), D_MODEL ** -0.5),
        'b_ada': nrm(ks[5], (DEPTH, 3 * D_MODEL), 0.01),
        'w_in': nrm(ks[6], (DEPTH, D_MODEL, IN_COLS), D_MODEL ** -0.5),
        'conv_w': nrm(ks[7], (DEPTH, CONV_K, 3 * GDN_WIDTH), CONV_K ** -0.5),
        'a_log': jnp.log(jax.random.uniform(ks[9], (DEPTH, 2, GDN_HEADS), jnp.float32, minval=1.0, maxval=16.0)),
        'dt_bias': dt + jnp.log(-jnp.expm1(-dt)),
        'gdn_norm_w': 1.0 + nrm(ks[10], (DEPTH, GDN_DV), 0.02),
        'q_norm_w': 1.0 + nrm(ks[11], (DEPTH, DIFF_DQK), 0.02),
        'k_norm_w': 1.0 + nrm(ks[12], (DEPTH, DIFF_DQK), 0.02),
        'lambda_q1': nrm(ks[13], (DEPTH, DIFF_DQK), 0.1),
        'lambda_k1': nrm(ks[14], (DEPTH, DIFF_DQK), 0.1),
        'lambda_q2': nrm(ks[15], (DEPTH, DIFF_DQK), 0.1),
        'lambda_k2': nrm(ks[16], (DEPTH, DIFF_DQK), 0.1),
        'diff_norm_w': 1.0 + nrm(ks[17], (DEPTH, DIFF_DV), 0.02),
        'w_oa': nrm(ks[18], (DEPTH, GDN_WIDTH, D_MODEL), GDN_WIDTH ** -0.5),
        'w_ob': nrm(ks[19], (DEPTH, DIFF_V_WIDTH, D_MODEL), DIFF_V_WIDTH ** -0.5),
        'w_out': nrm(ks[20], (DEPTH, D_MODEL, D_MODEL), D_MODEL ** -0.5),
    }


def reference(x, c, ctx, c_ctx, w_ada, b_ada, w_in, conv_w, a_log, dt_bias, gdn_norm_w,
              q_norm_w, k_norm_w, lambda_q1, lambda_k1, lambda_q2, lambda_k2, diff_norm_w,
              w_oa, w_ob, w_out):
    rope = axial_rope_tables(x.shape[1])
    for layer in range(DEPTH):
        x, ctx = hybrid_layer(x, ctx, c, c_ctx, layer, w_ada[layer], b_ada[layer], w_in[layer],
                              conv_w[layer], a_log[layer], dt_bias[layer], gdn_norm_w[layer],
                              q_norm_w[layer], k_norm_w[layer], lambda_q1[layer], lambda_k1[layer],
                              lambda_q2[layer], lambda_k2[layer], diff_norm_w[layer],
                              w_oa[layer], w_ob[layer], w_out[layer], rope, layer < DEPTH - 1)
    return x
```

```python
import functools
import math

import jax
import jax.numpy as jnp
import numpy as np
from jax import lax
from jax.experimental import pallas as pl
from jax.experimental.pallas import tpu as pltpu

F32 = jnp.float32
BF16 = jnp.bfloat16

D_MODEL = 1024
GRID_W = 64
EPS = 1e-6
HEADS = 8
HEAD_W = 128
GDN_WIDTH = HEADS * HEAD_W
CONV_K = 5
CHUNK = 64
DIFF_DQK = 64
ROPE_THETA = 10000.0
LAMBDA_INIT = 0.8 - 0.6 * math.exp(-0.3 * 0)
IN_SIZES = (3 * GDN_WIDTH, GDN_WIDTH, 2 * HEADS, 2 * HEADS, 1024, 1024, 1024, 1024, 2 * D_MODEL)

ROW_TILE = 256
HALO = 16
LANES = 128
VMEM_LIMIT = 56 * 1024 * 1024


def _bdot(a, b):
    return jnp.dot(a.astype(BF16), b.astype(BF16), preferred_element_type=F32)


def _bdot_nt(a, b):
    return lax.dot_general(a.astype(BF16), b.astype(BF16), (((1,), (1,)), ((), ())),
                           preferred_element_type=F32)


def _bdot_tn(a, b):
    return lax.dot_general(a.astype(BF16), b.astype(BF16), (((0,), (0,)), ((), ())),
                           preferred_element_type=F32)


def _silu(x):
    return x * (1.0 / (1.0 + jnp.exp(-x)))


def _sigmoid(x):
    return 1.0 / (1.0 + jnp.exp(-x))


def _ada_kernel(c_ref, w_ref, b_ref, o_ref):
    o_ref[...] = jnp.dot(_silu(c_ref[...]), w_ref[...], preferred_element_type=F32,
                         precision=lax.Precision.HIGHEST) + b_ref[...]


def _ada_modulation(cc, w_ada, b_ada):
    n_col = w_ada.shape[1] // D_MODEL
    return pl.pallas_call(
        _ada_kernel,
        grid=(n_col,),
        in_specs=[pl.BlockSpec((8, D_MODEL), lambda j: (0, 0)),
                  pl.BlockSpec((D_MODEL, D_MODEL), lambda j: (0, j)),
                  pl.BlockSpec((1, D_MODEL), lambda j: (0, j))],
        out_specs=pl.BlockSpec((8, D_MODEL), lambda j: (0, j)),
        out_shape=jax.ShapeDtypeStruct((8, w_ada.shape[1]), F32),
        name="ada_modulation",
    )(cc, w_ada, b_ada)


IN_GROUPS = (("qkv_a", 3 * GDN_WIDTH, BF16), ("z_a", GDN_WIDTH, BF16), ("q_b", 1024, BF16),
             ("k_b", 1024, BF16), ("v_b", 1024, BF16), ("z_b", 1024, BF16),
             ("gates", 2 * D_MODEL, BF16), ("ab", LANES, F32))
IN_COL_CHUNK = 1024


def _inproj_kernel(n_lat_tiles, n_batch, x_ref, ctx_ref, mod_ref, w_ref, *out_refs):
    b = pl.program_id(0)
    i = pl.program_id(1)
    is_ctx = i == n_lat_tiles
    xt = jnp.where(is_ctx, ctx_ref[0], x_ref[0])
    mod = mod_ref[pl.ds(jnp.where(is_ctx, n_batch, b), 1), :]
    shift = mod[:, 0:D_MODEL]
    scale = mod[:, D_MODEL:2 * D_MODEL]
    ms = jnp.mean(xt * xt, axis=-1, keepdims=True)
    h = (xt * lax.rsqrt(ms + EPS) * (1.0 + scale) + shift).astype(BF16)
    off = 0
    for o_ref, (_, width, _) in zip(out_refs, IN_GROUPS):
        for c0 in range(0, width, IN_COL_CHUNK):
            cw = min(IN_COL_CHUNK, width - c0)
            o_ref[0, :, c0:c0 + cw] = jnp.dot(
                h, w_ref[:, off + c0:off + c0 + cw], preferred_element_type=F32).astype(o_ref.dtype)
        off += width


def _input_projection(x, ctx, mod, w_packed):
    n_batch, n_lat, _ = x.shape
    n_lat_tiles = n_lat // ROW_TILE
    n_tok = n_lat + ctx.shape[1]
    total_cols = w_packed.shape[1]
    out_shape = [jax.ShapeDtypeStruct((n_batch, n_tok, w), dt) for _, w, dt in IN_GROUPS]
    out_specs = [pl.BlockSpec((1, ROW_TILE, w), lambda b, i: (b, i, 0)) for _, w, _ in IN_GROUPS]
    return pl.pallas_call(
        functools.partial(_inproj_kernel, n_lat_tiles, n_batch),
        grid=(n_batch, n_lat_tiles + 1),
        in_specs=[
            pl.BlockSpec((1, ROW_TILE, D_MODEL), lambda b, i: (b, jnp.minimum(i, n_lat_tiles - 1), 0)),
            pl.BlockSpec((1, ROW_TILE, D_MODEL), lambda b, i: (b, 0, 0)),
            pl.BlockSpec((8, 3 * D_MODEL), lambda b, i: (0, 0)),
            pl.BlockSpec((D_MODEL, total_cols), lambda b, i: (0, 0), pipeline_mode=pl.Buffered(1)),
        ],
        out_specs=out_specs,
        out_shape=out_shape,
        compiler_params=pltpu.CompilerParams(
            dimension_semantics=("arbitrary", "arbitrary"), vmem_limit_bytes=VMEM_LIMIT),
        name="input_projection",
    )(x, ctx, mod, w_packed)


def _gdn_prep_kernel(n_lat_tiles, cur_ref, prev_ref, next_ref, convw_ref, ab_ref, gpar_ref,
                     q_ref, k_ref, v_ref, gb_ref, xp_ref):
    i = pl.program_id(1)
    zero_prev = (i == 0) | (i == n_lat_tiles)
    zero_next = i >= n_lat_tiles - 1
    xp_ref[0:HALO, :] = jnp.where(zero_prev, 0.0, prev_ref[0].astype(F32))
    xp_ref[HALO:HALO + ROW_TILE, :] = cur_ref[0].astype(F32)
    xp_ref[HALO + ROW_TILE:, :] = jnp.where(zero_next, 0.0, next_ref[0].astype(F32))
    pad = CONV_K // 2
    outs = (q_ref, k_ref, v_ref)
    for g in range(3 * HEADS):
        cols = slice(g * HEAD_W, (g + 1) * HEAD_W)
        y = None
        for j in range(CONV_K):
            term = xp_ref[HALO - pad + j:HALO - pad + j + ROW_TILE, cols] * convw_ref[j:j + 1, cols]
            y = term if y is None else y + term
        y = _silu(y)
        which, head = divmod(g, HEADS)
        if which < 2:
            y = y * lax.rsqrt(jnp.sum(y * y, axis=-1, keepdims=True) + EPS)
        if which == 0:
            y = y * (HEAD_W ** -0.5)
        outs[which][0, :, head * HEAD_W:(head + 1) * HEAD_W] = y.astype(BF16)
    ab = ab_ref[0]
    lane = lax.broadcasted_iota(jnp.int32, ab.shape, 1)
    a_log = gpar_ref[0:1, :]
    dt_bias = gpar_ref[1:2, :]
    xs = ab + dt_bias
    softplus = jnp.maximum(xs, 0.0) + jnp.log1p(jnp.exp(-jnp.abs(xs)))
    g_all = -jnp.exp(a_log) * softplus
    gb_ref[0] = jnp.where(lane < 2 * HEADS, g_all, jnp.where(lane < 4 * HEADS, _sigmoid(ab), 0.0))


def _gdn_prepare(qkv_a, ab, conv_w8, gpar, n_lat):
    n_batch, n_tok, width = qkv_a.shape
    n_lat_tiles = n_lat // ROW_TILE
    n_tiles = n_tok // ROW_TILE
    halo_per_tile = ROW_TILE // HALO
    n_halo = n_tok // HALO
    tok_spec = lambda w: pl.BlockSpec((1, ROW_TILE, w), lambda b, i: (b, i, 0))
    return pl.pallas_call(
        functools.partial(_gdn_prep_kernel, n_lat_tiles),
        grid=(n_batch, n_tiles),
        in_specs=[
            tok_spec(width),
            pl.BlockSpec((1, HALO, width), lambda b, i: (b, jnp.maximum(i * halo_per_tile - 1, 0), 0)),
            pl.BlockSpec((1, HALO, width),
                         lambda b, i: (b, jnp.minimum((i + 1) * halo_per_tile, n_halo - 1), 0)),
            pl.BlockSpec((8, width), lambda b, i: (0, 0)),
            tok_spec(LANES),
            pl.BlockSpec((8, LANES), lambda b, i: (0, 0)),
        ],
        out_specs=[tok_spec(GDN_WIDTH), tok_spec(GDN_WIDTH), tok_spec(GDN_WIDTH), tok_spec(LANES)],
        out_shape=[jax.ShapeDtypeStruct((n_batch, n_tok, GDN_WIDTH), BF16)] * 3
        + [jax.ShapeDtypeStruct((n_batch, n_tok, LANES), F32)],
        scratch_shapes=[pltpu.VMEM((ROW_TILE + 2 * HALO, width), F32)],
        compiler_params=pltpu.CompilerParams(
            dimension_semantics=("arbitrary", "arbitrary"), vmem_limit_bytes=VMEM_LIMIT),
        name="gdn_prepare",
    )(qkv_a, qkv_a, qkv_a, conv_w8, ab, gpar)


def _split3(x):
    hi = x.astype(BF16)
    r1 = x - hi.astype(F32)
    mid = r1.astype(BF16)
    lo = (r1 - mid.astype(F32)).astype(BF16)
    return hi, mid, lo


def _unit_triangular_inverse(a, row, col):
    t = jnp.where(row == col, 1.0, 0.0) - jnp.where((row >> 1) == (col >> 1), a, 0.0)
    shift = 1
    while (1 << shift) < a.shape[0]:
        off = ((row >> (shift + 1)) == (col >> (shift + 1))) & ((row >> shift) != (col >> shift))
        x = _bdot(jnp.where(off, a, 0.0), t)
        t = t - _bdot(t, x)
        shift += 1
    return t


def _gdn_scan_kernel(qf_ref, kf_ref, vf_ref, gbf_ref, qb_ref, kb_ref, vb_ref, gbb_ref,
                     of_ref, ob_ref, s_ref):
    @pl.when(pl.program_id(1) == 0)
    def _():
        s_ref[...] = jnp.zeros_like(s_ref)

    row = lax.broadcasted_iota(jnp.int32, (CHUNK, CHUNK), 0)
    col = lax.broadcasted_iota(jnp.int32, (CHUNK, CHUNK), 1)
    for d, (q_ref, k_ref, v_ref, gb_ref, o_ref) in enumerate(
            ((qf_ref, kf_ref, vf_ref, gbf_ref, of_ref), (qb_ref, kb_ref, vb_ref, gbb_ref, ob_ref))):
        incl = (row >= col) if d == 0 else (row <= col)
        strict = (row > col) if d == 0 else (row < col)
        last = CHUNK - 1 if d == 0 else 0
        gb = gb_ref[0]
        tri = jnp.where(incl, 1.0, 0.0).astype(BF16)
        gc = None
        for part in _split3(gb):
            term = jnp.dot(tri, part, preferred_element_type=F32)
            gc = term if gc is None else gc + term
        gc_t = gc.T
        for h in range(HEADS):
            lane = d * HEADS + h
            cols = slice(h * HEAD_W, (h + 1) * HEAD_W)
            g_col = gc[:, lane:lane + 1]
            g_row = gc_t[lane:lane + 1, :]
            g_last = gc[last:last + 1, lane:lane + 1]
            beta = gb[:, 2 * HEADS + lane:2 * HEADS + lane + 1]
            q = q_ref[0, :, cols].astype(F32)
            k = k_ref[0, :, cols].astype(F32)
            v = v_ref[0, :, cols].astype(F32)
            decay = jnp.where(incl, jnp.exp(jnp.where(incl, g_col - g_row, 0.0)), 0.0)
            kq = _bdot_nt(jnp.concatenate([k * beta, q], axis=0), k)
            a_mat = jnp.where(strict, kq[:CHUNK] * decay, 0.0)
            qk = jnp.where(incl, kq[CHUNK:] * decay, 0.0)
            t_mat = _unit_triangular_inverse(a_mat, row, col)
            e_g = jnp.exp(g_col)
            uw = _bdot(t_mat, jnp.concatenate([v * beta, k * (beta * e_g)], axis=1))
            u = uw[:, :HEAD_W]
            w = uw[:, HEAD_W:]
            s = s_ref[lane]
            ws_qs = _bdot(jnp.concatenate([w, q * e_g], axis=0), s)
            v_new = u - ws_qs[:CHUNK]
            k_dec = k * jnp.exp(g_last - g_col)
            s_ref[lane] = s * jnp.exp(g_last) + _bdot_tn(k_dec, v_new)
            o_ref[0, :, cols] = (ws_qs[CHUNK:] + _bdot(qk, v_new)).astype(o_ref.dtype)


def _gdn_scan(q, k, v, gb, n_lat):
    n_batch, n_tok, _ = q.shape
    n_lat_chunks = n_lat // CHUNK
    n_ctx_chunks = (n_tok - n_lat) // CHUNK
    n_steps = n_tok // CHUNK

    def fwd_chunk(s):
        return jnp.where(s < n_ctx_chunks, n_lat_chunks + s, s - n_ctx_chunks)

    def bwd_chunk(s):
        return jnp.where(s < n_ctx_chunks, n_lat_chunks + n_ctx_chunks - 1 - s,
                         n_lat_chunks - 1 - (s - n_ctx_chunks))

    f_spec = lambda w: pl.BlockSpec((1, CHUNK, w), lambda b, s: (b, fwd_chunk(s), 0))
    b_spec = lambda w: pl.BlockSpec((1, CHUNK, w), lambda b, s: (b, bwd_chunk(s), 0))
    return pl.pallas_call(
        _gdn_scan_kernel,
        grid=(n_batch, n_steps),
        in_specs=[f_spec(GDN_WIDTH)] * 3 + [f_spec(LANES)] + [b_spec(GDN_WIDTH)] * 3 + [b_spec(LANES)],
        out_specs=[f_spec(GDN_WIDTH), b_spec(GDN_WIDTH)],
        out_shape=[jax.ShapeDtypeStruct((n_batch, n_tok, GDN_WIDTH), BF16)] * 2,
        scratch_shapes=[pltpu.VMEM((2 * HEADS, HEAD_W, HEAD_W), F32)],
        compiler_params=pltpu.CompilerParams(
            dimension_semantics=("arbitrary", "arbitrary"), vmem_limit_bytes=VMEM_LIMIT),
        name="gdn_scan",
    )(q, k, v, gb, q, k, v, gb)


def _qk_prep_kernel(q_ref, k_ref, cos_ref, sin_ref, gain_ref, qo_ref, ko_ref):
    cos = cos_ref[...]
    sin = sin_ref[...]
    lane = lax.broadcasted_iota(jnp.int32, cos.shape, 1)
    first_comp = lane < DIFF_DQK
    first_half = (lane % (DIFF_DQK // 2)) < (DIFF_DQK // 4)
    for x_ref, o_ref, gain, out_scale in ((q_ref, qo_ref, gain_ref[0:1, :], DIFF_DQK ** -0.5),
                                          (k_ref, ko_ref, gain_ref[1:2, :], 1.0)):
        for h in range(HEADS):
            cols = slice(h * HEAD_W, (h + 1) * HEAD_W)
            x = x_ref[0, :, cols].astype(F32)
            xx = x * x
            s1 = jnp.sum(jnp.where(first_comp, xx, 0.0), axis=-1, keepdims=True)
            s2 = jnp.sum(jnp.where(first_comp, 0.0, xx), axis=-1, keepdims=True)
            ms = jnp.where(first_comp, s1, s2) * (1.0 / DIFF_DQK)
            t = x * lax.rsqrt(ms + EPS) * gain
            rot = jnp.where(first_half, -pltpu.roll(t, LANES - DIFF_DQK // 4, 1),
                            pltpu.roll(t, DIFF_DQK // 4, 1))
            y = t * cos + rot * sin
            if out_scale != 1.0:
                y = y * out_scale
            o_ref[0, :, cols] = y.astype(o_ref.dtype)


def _qk_prepare(q_b, k_b, cos_tab, sin_tab, gains):
    n_batch, n_tok, width = q_b.shape
    tok_spec = pl.BlockSpec((1, ROW_TILE, width), lambda b, i: (b, i, 0))
    tab_spec = pl.BlockSpec((ROW_TILE, LANES), lambda b, i: (i, 0))
    return pl.pallas_call(
        _qk_prep_kernel,
        grid=(n_batch, n_tok // ROW_TILE),
        in_specs=[tok_spec, tok_spec, tab_spec, tab_spec, pl.BlockSpec((8, LANES), lambda b, i: (0, 0))],
        out_specs=[tok_spec, tok_spec],
        out_shape=[jax.ShapeDtypeStruct((n_batch, n_tok, width), BF16)] * 2,
        compiler_params=pltpu.CompilerParams(dimension_semantics=("arbitrary", "arbitrary")),
        name="qk_norm_rope",
    )(q_b, k_b, cos_tab, sin_tab, gains)


ATT_TQ = 256
ATT_TK = 768


def _diff_attn_kernel(n_kv, q_ref, k_ref, v_ref, z_ref, lam_ref, normw_ref, o_ref):
    tq = q_ref.shape[1]
    q = q_ref[0]
    lane = lax.broadcasted_iota(jnp.int32, q.shape, 1)
    zero = jnp.zeros_like(q)
    qs = jnp.concatenate([jnp.where(lane < DIFF_DQK, q, zero), jnp.where(lane < DIFF_DQK, zero, q)], axis=0)

    def body(j, carry):
        m, l, acc = carry
        start = pl.multiple_of(j * ATT_TK, ATT_TK)
        kb = k_ref[0, pl.ds(start, ATT_TK), :]
        vb = v_ref[0, pl.ds(start, ATT_TK), :]
        s = lax.dot_general(qs, kb, (((1,), (1,)), ((), ())), preferred_element_type=F32)
        m_new = jnp.maximum(m, jnp.max(s, axis=-1, keepdims=True))
        alpha = jnp.exp(m - m_new)
        p = jnp.exp(s - m_new)
        l = alpha * l + jnp.sum(p, axis=-1, keepdims=True)
        acc = alpha * acc + jnp.dot(p.astype(BF16), vb, preferred_element_type=F32)
        return m_new, l, acc

    m0 = jnp.full((2 * tq, 1), -jnp.inf, F32)
    l0 = jnp.zeros((2 * tq, 1), F32)
    acc0 = jnp.zeros((2 * tq, HEAD_W), F32)
    _, l, acc = lax.fori_loop(0, n_kv, body, (m0, l0, acc0))
    o = acc / l
    lam_p = lam_ref[...]
    lam = (jnp.exp(jnp.sum(lam_p[0:1, :] * lam_p[1:2, :], axis=-1, keepdims=True))
           - jnp.exp(jnp.sum(lam_p[2:3, :] * lam_p[3:4, :], axis=-1, keepdims=True)) + LAMBDA_INIT)
    o = o[:tq] - lam * o[tq:]
    y = o * lax.rsqrt(jnp.mean(o * o, axis=-1, keepdims=True) + EPS) * normw_ref[0:1, :]
    o_ref[0] = (y * (1.0 - LAMBDA_INIT) * _silu(z_ref[0].astype(F32))).astype(o_ref.dtype)


def _diff_attention(qn, kn, v_b, z_b, lam_par, norm_w, n_lat):
    n_batch, n_tok, _ = qn.shape
    q_spec = pl.BlockSpec((1, ATT_TQ, HEAD_W), lambda b, h, i: (b, i, h))
    kv_spec = pl.BlockSpec((1, n_tok, HEAD_W), lambda b, h, i: (b, 0, h))
    par_spec = pl.BlockSpec((8, LANES), lambda b, h, i: (0, 0))
    return pl.pallas_call(
        functools.partial(_diff_attn_kernel, n_tok // ATT_TK),
        grid=(n_batch, HEADS, n_lat // ATT_TQ),
        in_specs=[q_spec, kv_spec, kv_spec, q_spec, par_spec, par_spec],
        out_specs=q_spec,
        out_shape=jax.ShapeDtypeStruct((n_batch, n_lat, HEADS * HEAD_W), BF16),
        compiler_params=pltpu.CompilerParams(
            dimension_semantics=("arbitrary", "arbitrary", "arbitrary"), vmem_limit_bytes=VMEM_LIMIT),
        name="diff_attention",
    )(qn, kn, v_b, z_b, lam_par, norm_w)


def _mixer_out_kernel(x_ref, of_ref, ob_ref, za_ref, yb_ref, gates_ref, mod_ref, normw_ref,
                      woa_ref, wob_ref, wout_ref, o_ref, ya_ref):
    b = pl.program_id(0)
    for h in range(HEADS):
        cols = slice(h * HEAD_W, (h + 1) * HEAD_W)
        o_a = of_ref[0, :, cols].astype(F32) + ob_ref[0, :, cols].astype(F32)
        y = o_a * lax.rsqrt(jnp.mean(o_a * o_a, axis=-1, keepdims=True) + EPS) * normw_ref[0:1, :]
        ya_ref[:, cols] = (y * _silu(za_ref[0, :, cols].astype(F32))).astype(BF16)
    gates = _sigmoid(gates_ref[0].astype(F32))
    merged = (gates[:, :D_MODEL] * jnp.dot(ya_ref[...], woa_ref[...], preferred_element_type=F32)
              + gates[:, D_MODEL:] * jnp.dot(yb_ref[0], wob_ref[...], preferred_element_type=F32))
    out = jnp.dot(merged.astype(BF16), wout_ref[...], preferred_element_type=F32)
    gate = mod_ref[pl.ds(b, 1), 2 * D_MODEL:3 * D_MODEL]
    o_ref[0] = x_ref[0] + gate * out


def _mixer_output(x, o_f, o_b, z_a, y_b, gates, mod, gdn_norm_w, w_oa, w_ob, w_out):
    n_batch, n_lat, _ = x.shape
    tok_spec = lambda w: pl.BlockSpec((1, ROW_TILE, w), lambda b, i: (b, i, 0))
    w_spec = pl.BlockSpec((D_MODEL, D_MODEL), lambda b, i: (0, 0))
    return pl.pallas_call(
        _mixer_out_kernel,
        grid=(n_batch, n_lat // ROW_TILE),
        in_specs=[tok_spec(D_MODEL), tok_spec(GDN_WIDTH), tok_spec(GDN_WIDTH), tok_spec(GDN_WIDTH),
                  tok_spec(D_MODEL), tok_spec(2 * D_MODEL),
                  pl.BlockSpec((8, 3 * D_MODEL), lambda b, i: (0, 0)),
                  pl.BlockSpec((8, LANES), lambda b, i: (0, 0)),
                  w_spec, w_spec, w_spec],
        out_specs=tok_spec(D_MODEL),
        out_shape=jax.ShapeDtypeStruct(x.shape, F32),
        scratch_shapes=[pltpu.VMEM((ROW_TILE, GDN_WIDTH), BF16)],
        compiler_params=pltpu.CompilerParams(
            dimension_semantics=("arbitrary", "arbitrary"), vmem_limit_bytes=VMEM_LIMIT),
        name="mixer_output",
    )(x, o_f, o_b, z_a, y_b, gates, mod, gdn_norm_w, w_oa, w_ob, w_out)


def _pad_rows(a, rows=8):
    return jnp.pad(a, ((0, rows - a.shape[0]), (0, 0)))


def _pad_lanes(a, lanes=LANES):
    return jnp.pad(a, ((0, 0), (0, lanes - a.shape[1])))


def _rope_tables(n_lat, n_ctx):
    rows = n_lat // GRID_W
    row = jnp.broadcast_to(jnp.arange(rows)[:, None], (rows, GRID_W)).reshape(-1).astype(F32)
    col = jnp.broadcast_to(jnp.arange(GRID_W)[None, :], (rows, GRID_W)).reshape(-1).astype(F32)
    n_freq = DIFF_DQK // 4
    inv_freq = ROPE_THETA ** (-jnp.arange(n_freq, dtype=F32) / n_freq)
    ang_r = row[:, None] * inv_freq
    ang_c = col[:, None] * inv_freq
    ang = jnp.concatenate([ang_r, ang_r, ang_c, ang_c] * 2, axis=-1)
    cos = jnp.concatenate([jnp.cos(ang), jnp.ones((n_ctx, LANES), F32)], axis=0)
    sin = jnp.concatenate([jnp.sin(ang), jnp.zeros((n_ctx, LANES), F32)], axis=0)
    return cos, sin


def kernel(x, c, ctx, c_ctx, w_ada, b_ada, w_in, conv_w, a_log, dt_bias, gdn_norm_w, q_norm_w, k_norm_w,
           lambda_q1, lambda_k1, lambda_q2, lambda_k2, diff_norm_w, w_oa, w_ob, w_out):
    assert w_ada.shape[0] == 1, "single-layer block"
    n_batch, n_lat, _ = x.shape
    n_ctx = ctx.shape[1]
    assert n_ctx == ROW_TILE and n_lat % ROW_TILE == 0 and n_batch < 8

    bounds = np.cumsum((0,) + IN_SIZES)
    piece = lambda j: w_in[0][:, bounds[j]:bounds[j + 1]]
    ab_cols = _pad_lanes(jnp.concatenate([piece(2), piece(3)], axis=1))
    w_packed = jnp.concatenate([piece(0), piece(1), piece(4), piece(5), piece(6), piece(7), piece(8)],
                               axis=1).astype(BF16)
    w_packed = jnp.concatenate([w_packed, ab_cols.astype(BF16)], axis=1)
    cc = _pad_rows(jnp.concatenate([c, c_ctx[None, :]], axis=0))
    conv_w8 = _pad_rows(conv_w[0])
    gpar = _pad_rows(_pad_lanes(jnp.stack([a_log[0].reshape(-1), dt_bias[0].reshape(-1)])))
    gains = _pad_rows(jnp.stack([jnp.tile(q_norm_w[0], 2), jnp.tile(k_norm_w[0], 2)]))
    lam_par = _pad_rows(_pad_lanes(jnp.stack([lambda_q1[0], lambda_k1[0], lambda_q2[0], lambda_k2[0]])))
    cos_tab, sin_tab = _rope_tables(n_lat, n_ctx)

    mod = _ada_modulation(cc, w_ada[0], b_ada)
    qkv_a, z_a, q_b, k_b, v_b, z_b, gates, ab = _input_projection(x, ctx, mod, w_packed)
    q_a, k_a, v_a, gb = _gdn_prepare(qkv_a, ab, conv_w8, gpar, n_lat)
    o_f, o_b = _gdn_scan(q_a, k_a, v_a, gb, n_lat)
    qn, kn = _qk_prepare(q_b, k_b, cos_tab, sin_tab, gains)
    y_b = _diff_attention(qn, kn, v_b, z_b, lam_par, _pad_rows(diff_norm_w), n_lat)
    return _mixer_output(x, o_f, o_b, z_a, y_b, gates, mod, _pad_rows(gdn_norm_w),
                         w_oa[0].astype(BF16), w_ob[0].astype(BF16), w_out[0].astype(BF16))
```

```python
import functools
import math

import jax
import jax.numpy as jnp
import numpy as np
from jax import lax
from jax.experimental import pallas as pl
from jax.experimental.pallas import tpu as pltpu

F32 = jnp.float32
BF16 = jnp.bfloat16

D_MODEL = 1024
GRID_W = 64
EPS = 1e-6
HEADS = 8
HEAD_W = 128
GDN_WIDTH = HEADS * HEAD_W
CONV_K = 5
CHUNK = 64
DIFF_DQK = 64
ROPE_THETA = 10000.0
LAMBDA_INIT = 0.8 - 0.6 * math.exp(-0.3 * 0)
IN_SIZES = (3 * GDN_WIDTH, GDN_WIDTH, 2 * HEADS, 2 * HEADS, 1024, 1024, 1024, 1024, 2 * D_MODEL)

ROW_TILE = 256
HALO = 16
LANES = 128
VMEM_LIMIT = 56 * 1024 * 1024


def _bdot(a, b):
    return jnp.dot(a.astype(BF16), b.astype(BF16), preferred_element_type=F32)


def _bdot_nt(a, b):
    return lax.dot_general(a.astype(BF16), b.astype(BF16), (((1,), (1,)), ((), ())),
                           preferred_element_type=F32)


def _bdot_tn(a, b):
    return lax.dot_general(a.astype(BF16), b.astype(BF16), (((0,), (0,)), ((), ())),
                           preferred_element_type=F32)


def _silu(x):
    return x * (1.0 / (1.0 + jnp.exp(-x)))


def _sigmoid(x):
    return 1.0 / (1.0 + jnp.exp(-x))


def _ada_kernel(c_ref, w_ref, b_ref, o_ref):
    o_ref[...] = jnp.dot(_silu(c_ref[...]), w_ref[...], preferred_element_type=F32,
                         precision=lax.Precision.HIGHEST) + b_ref[...]


def _ada_modulation(cc, w_ada, b_ada):
    n_col = w_ada.shape[1] // D_MODEL
    return pl.pallas_call(
        _ada_kernel,
        grid=(n_col,),
        in_specs=[pl.BlockSpec((8, D_MODEL), lambda j: (0, 0)),
                  pl.BlockSpec((D_MODEL, D_MODEL), lambda j: (0, j)),
                  pl.BlockSpec((1, D_MODEL), lambda j: (0, j))],
        out_specs=pl.BlockSpec((8, D_MODEL), lambda j: (0, j)),
        out_shape=jax.ShapeDtypeStruct((8, w_ada.shape[1]), F32),
        name="ada_modulation",
    )(cc, w_ada, b_ada)


IN_GROUPS = (("qkv_a", 3 * GDN_WIDTH, BF16), ("z_a", GDN_WIDTH, BF16), ("q_b", 1024, BF16),
             ("k_b", 1024, BF16), ("v_b", 1024, BF16), ("z_b", 1024, BF16),
             ("gates", 2 * D_MODEL, BF16), ("ab", LANES, F32))
IN_COL_CHUNK = 1024


def _inproj_kernel(n_lat_tiles, n_batch, x_ref, ctx_ref, mod_ref, w_ref, *out_refs):
    b = pl.program_id(0)
    i = pl.program_id(1)
    is_ctx = i == n_lat_tiles
    xt = jnp.where(is_ctx, ctx_ref[0], x_ref[0])
    mod = mod_ref[pl.ds(jnp.where(is_ctx, n_batch, b), 1), :]
    shift = mod[:, 0:D_MODEL]
    scale = mod[:, D_MODEL:2 * D_MODEL]
    ms = jnp.mean(xt * xt, axis=-1, keepdims=True)
    h = (xt * lax.rsqrt(ms + EPS) * (1.0 + scale) + shift).astype(BF16)
    off = 0
    for o_ref, (_, width, _) in zip(out_refs, IN_GROUPS):
        for c0 in range(0, width, IN_COL_CHUNK):
            cw = min(IN_COL_CHUNK, width - c0)
            o_ref[0, :, c0:c0 + cw] = jnp.dot(
                h, w_ref[:, off + c0:off + c0 + cw], preferred_element_type=F32).astype(o_ref.dtype)
        off += width


def _input_projection(x, ctx, mod, w_packed):
    n_batch, n_lat, _ = x.shape
    n_lat_tiles = n_lat // ROW_TILE
    n_tok = n_lat + ctx.shape[1]
    total_cols = w_packed.shape[1]
    out_shape = [jax.ShapeDtypeStruct((n_batch, n_tok, w), dt) for _, w, dt in IN_GROUPS]
    out_specs = [pl.BlockSpec((1, ROW_TILE, w), lambda b, i: (b, i, 0)) for _, w, _ in IN_GROUPS]
    return pl.pallas_call(
        functools.partial(_inproj_kernel, n_lat_tiles, n_batch),
        grid=(n_batch, n_lat_tiles + 1),
        in_specs=[
            pl.BlockSpec((1, ROW_TILE, D_MODEL), lambda b, i: (b, jnp.minimum(i, n_lat_tiles - 1), 0)),
            pl.BlockSpec((1, ROW_TILE, D_MODEL), lambda b, i: (b, 0, 0)),
            pl.BlockSpec((8, 3 * D_MODEL), lambda b, i: (0, 0)),
            pl.BlockSpec((D_MODEL, total_cols), lambda b, i: (0, 0), pipeline_mode=pl.Buffered(1)),
        ],
        out_specs=out_specs,
        out_shape=out_shape,
        compiler_params=pltpu.CompilerParams(
            dimension_semantics=("arbitrary", "arbitrary"), vmem_limit_bytes=VMEM_LIMIT),
        name="input_projection",
    )(x, ctx, mod, w_packed)


def _gdn_prep_kernel(n_lat_tiles, cur_ref, prev_ref, next_ref, convw_ref, ab_ref, gpar_ref,
                     q_ref, k_ref, v_ref, gb_ref, xp_ref):
    i = pl.program_id(1)
    zero_prev = (i == 0) | (i == n_lat_tiles)
    zero_next = i >= n_lat_tiles - 1
    xp_ref[0:HALO, :] = jnp.where(zero_prev, 0.0, prev_ref[0].astype(F32))
    xp_ref[HALO:HALO + ROW_TILE, :] = cur_ref[0].astype(F32)
    xp_ref[HALO + ROW_TILE:, :] = jnp.where(zero_next, 0.0, next_ref[0].astype(F32))
    pad = CONV_K // 2
    outs = (q_ref, k_ref, v_ref)
    for g in range(3 * HEADS):
        cols = slice(g * HEAD_W, (g + 1) * HEAD_W)
        y = None
        for j in range(CONV_K):
            term = xp_ref[HALO - pad + j:HALO - pad + j + ROW_TILE, cols] * convw_ref[j:j + 1, cols]
            y = term if y is None else y + term
        y = _silu(y)
        which, head = divmod(g, HEADS)
        if which < 2:
            y = y * lax.rsqrt(jnp.sum(y * y, axis=-1, keepdims=True) + EPS)
        if which == 0:
            y = y * (HEAD_W ** -0.5)
        outs[which][0, :, head * HEAD_W:(head + 1) * HEAD_W] = y.astype(BF16)
    ab = ab_ref[0]
    lane = lax.broadcasted_iota(jnp.int32, ab.shape, 1)
    a_log = gpar_ref[0:1, :]
    dt_bias = gpar_ref[1:2, :]
    xs = ab + dt_bias
    softplus = jnp.maximum(xs, 0.0) + jnp.log1p(jnp.exp(-jnp.abs(xs)))
    g_all = -jnp.exp(a_log) * softplus
    gb_ref[0] = jnp.where(lane < 2 * HEADS, g_all, jnp.where(lane < 4 * HEADS, _sigmoid(ab), 0.0))


def _gdn_prepare(qkv_a, ab, conv_w8, gpar, n_lat):
    n_batch, n_tok, width = qkv_a.shape
    n_lat_tiles = n_lat // ROW_TILE
    n_tiles = n_tok // ROW_TILE
    halo_per_tile = ROW_TILE // HALO
    n_halo = n_tok // HALO
    tok_spec = lambda w: pl.BlockSpec((1, ROW_TILE, w), lambda b, i: (b, i, 0))
    return pl.pallas_call(
        functools.partial(_gdn_prep_kernel, n_lat_tiles),
        grid=(n_batch, n_tiles),
        in_specs=[
            tok_spec(width),
            pl.BlockSpec((1, HALO, width), lambda b, i: (b, jnp.maximum(i * halo_per_tile - 1, 0), 0)),
            pl.BlockSpec((1, HALO, width),
                         lambda b, i: (b, jnp.minimum((i + 1) * halo_per_tile, n_halo - 1), 0)),
            pl.BlockSpec((8, width), lambda b, i: (0, 0)),
            tok_spec(LANES),
            pl.BlockSpec((8, LANES), lambda b, i: (0, 0)),
        ],
        out_specs=[tok_spec(GDN_WIDTH), tok_spec(GDN_WIDTH), tok_spec(GDN_WIDTH), tok_spec(LANES)],
        out_shape=[jax.ShapeDtypeStruct((n_batch, n_tok, GDN_WIDTH), BF16)] * 3
        + [jax.ShapeDtypeStruct((n_batch, n_tok, LANES), F32)],
        scratch_shapes=[pltpu.VMEM((ROW_TILE + 2 * HALO, width), F32)],
        compiler_params=pltpu.CompilerParams(
            dimension_semantics=("arbitrary", "arbitrary"), vmem_limit_bytes=VMEM_LIMIT),
        name="gdn_prepare",
    )(qkv_a, qkv_a, qkv_a, conv_w8, ab, gpar)


def _split3(x):
    hi = x.astype(BF16)
    r1 = x - hi.astype(F32)
    mid = r1.astype(BF16)
    lo = (r1 - mid.astype(F32)).astype(BF16)
    return hi, mid, lo


GROUP = 4
GSIZE = GROUP * CHUNK
N_LEVELS = 5
M_EYE, M_PAIR, M_LEVEL0 = 0, 1, 2
M_INCL = (M_LEVEL0 + N_LEVELS, M_LEVEL0 + N_LEVELS + 2)
M_STRICT = (M_LEVEL0 + N_LEVELS + 1, M_LEVEL0 + N_LEVELS + 3)


def _scan_masks():
    r = np.arange(GSIZE)[:, None]
    c = np.arange(GSIZE)[None, :]
    blk = (r // CHUNK) == (c // CHUNK)
    planes = [r == c, (r >> 1) == (c >> 1)]
    for shift in range(1, N_LEVELS + 1):
        planes.append(((r >> (shift + 1)) == (c >> (shift + 1))) & ((r >> shift) != (c >> shift)))
    planes += [blk & (r >= c), blk & (r > c), blk & (r <= c), blk & (r < c)]
    return np.stack(planes).astype(np.float32)


def _gdn_scan_kernel(qf_ref, kf_ref, vf_ref, gbf_ref, qb_ref, kb_ref, vb_ref, gbb_ref, mask_ref,
                     of_ref, ob_ref, s_ref):
    @pl.when(pl.program_id(1) == 0)
    def _():
        s_ref[...] = jnp.zeros_like(s_ref)

    rows_of = lambda c: slice(c * CHUNK, (c + 1) * CHUNK)
    stack_rows = lambda parts: jnp.concatenate(parts, axis=0)

    groups = []
    for d, (q_ref, k_ref, v_ref, gb_ref, o_ref) in enumerate(
            ((qf_ref, kf_ref, vf_ref, gbf_ref, of_ref), (qb_ref, kb_ref, vb_ref, gbb_ref, ob_ref))):
        last = CHUNK - 1 if d == 0 else 0
        gb = gb_ref[0]
        tri = mask_ref[M_INCL[d], 0:CHUNK, 0:CHUNK].astype(BF16)
        gc = None
        for part in _split3(gb):
            term = jnp.dot(tri, part, preferred_element_type=F32)
            gc = term if gc is None else gc + term
        gc_t = gc.T
        for half in range(HEADS // GROUP):
            heads = [half * GROUP + c for c in range(GROUP)]
            lanes = [d * HEADS + h for h in heads]
            head_cols = [slice(h * HEAD_W, (h + 1) * HEAD_W) for h in heads]
            groups.append(dict(
                d=d, lanes=lanes, head_cols=head_cols, o_ref=o_ref,
                q=stack_rows([q_ref[0, :, cs].astype(F32) for cs in head_cols]),
                k=stack_rows([k_ref[0, :, cs].astype(F32) for cs in head_cols]),
                v=stack_rows([v_ref[0, :, cs].astype(F32) for cs in head_cols]),
                g_col=stack_rows([gc[:, l:l + 1] for l in lanes]),
                g_row=jnp.concatenate([gc_t[l:l + 1, :] for l in lanes], axis=1),
                g_last=stack_rows([jnp.broadcast_to(gc[last:last + 1, l:l + 1], (CHUNK, 1)) for l in lanes]),
                beta=stack_rows([gb[:, 2 * HEADS + l:2 * HEADS + l + 1] for l in lanes])))

    for g in groups:
        incl = mask_ref[M_INCL[g["d"]]]
        decay = jnp.exp((g["g_col"] - g["g_row"]) * incl) * incl
        kq = _bdot_nt(stack_rows([g["k"] * g["beta"], g["q"]]), g["k"])
        g["a"] = kq[:GSIZE] * decay * mask_ref[M_STRICT[g["d"]]]
        g["qk"] = kq[GSIZE:] * decay
        g["t"] = mask_ref[M_EYE] - g["a"] * mask_ref[M_PAIR]
    for level in range(N_LEVELS):
        for g in groups:
            g["x"] = _bdot(g["a"] * mask_ref[M_LEVEL0 + level], g["t"])
        for g in groups:
            g["t"] = g["t"] - _bdot(g["t"], g["x"])
    for g in groups:
        e_g = jnp.exp(g["g_col"])
        uw = _bdot(g["t"], jnp.concatenate([g["v"] * g["beta"], g["k"] * (g["beta"] * e_g)], axis=1))
        g["u"] = uw[:, :HEAD_W]
        g["w"] = uw[:, HEAD_W:]
        g["qd"] = g["q"] * e_g
        g["k_dec"] = g["k"] * jnp.exp(g["g_last"] - g["g_col"])

    for g in groups:
        v_new, o_inter = [], []
        for c, lane in enumerate(g["lanes"]):
            s = s_ref[lane]
            ws_qs = _bdot(stack_rows([g["w"][rows_of(c)], g["qd"][rows_of(c)]]), s)
            v_new.append(g["u"][rows_of(c)] - ws_qs[:CHUNK])
            o_inter.append(ws_qs[CHUNK:])
            s_ref[lane] = (s * jnp.exp(g["g_last"][c * CHUNK:c * CHUNK + 1, :])
                           + _bdot_tn(g["k_dec"][rows_of(c)], v_new[-1]))
        o = stack_rows(o_inter) + _bdot(g["qk"], stack_rows(v_new))
        for c, cs in enumerate(g["head_cols"]):
            g["o_ref"][0, :, cs] = o[rows_of(c)].astype(g["o_ref"].dtype)


def _gdn_scan(q, k, v, gb, n_lat):
    n_batch, n_tok, _ = q.shape
    n_lat_chunks = n_lat // CHUNK
    n_ctx_chunks = (n_tok - n_lat) // CHUNK
    n_steps = n_tok // CHUNK

    def fwd_chunk(s):
        return jnp.where(s < n_ctx_chunks, n_lat_chunks + s, s - n_ctx_chunks)

    def bwd_chunk(s):
        return jnp.where(s < n_ctx_chunks, n_lat_chunks + n_ctx_chunks - 1 - s,
                         n_lat_chunks - 1 - (s - n_ctx_chunks))

    f_spec = lambda w: pl.BlockSpec((1, CHUNK, w), lambda b, s: (b, fwd_chunk(s), 0))
    b_spec = lambda w: pl.BlockSpec((1, CHUNK, w), lambda b, s: (b, bwd_chunk(s), 0))
    masks = jnp.asarray(_scan_masks())
    return pl.pallas_call(
        _gdn_scan_kernel,
        grid=(n_batch, n_steps),
        in_specs=[f_spec(GDN_WIDTH)] * 3 + [f_spec(LANES)] + [b_spec(GDN_WIDTH)] * 3 + [b_spec(LANES)]
        + [pl.BlockSpec(masks.shape, lambda b, s: (0, 0, 0))],
        out_specs=[f_spec(GDN_WIDTH), b_spec(GDN_WIDTH)],
        out_shape=[jax.ShapeDtypeStruct((n_batch, n_tok, GDN_WIDTH), BF16)] * 2,
        scratch_shapes=[pltpu.VMEM((2 * HEADS, HEAD_W, HEAD_W), F32)],
        compiler_params=pltpu.CompilerParams(
            dimension_semantics=("arbitrary", "arbitrary"), vmem_limit_bytes=VMEM_LIMIT),
        name="gdn_scan",
    )(q, k, v, gb, q, k, v, gb, masks)


def _qk_prep_kernel(q_ref, k_ref, cos_ref, sin_ref, gain_ref, qo_ref, ko_ref):
    cos = cos_ref[...]
    sin = sin_ref[...]
    lane = lax.broadcasted_iota(jnp.int32, cos.shape, 1)
    first_comp = lane < DIFF_DQK
    first_half = (lane % (DIFF_DQK // 2)) < (DIFF_DQK // 4)
    for x_ref, o_ref, gain, out_scale in ((q_ref, qo_ref, gain_ref[0:1, :], DIFF_DQK ** -0.5),
                                          (k_ref, ko_ref, gain_ref[1:2, :], 1.0)):
        for h in range(HEADS):
            cols = slice(h * HEAD_W, (h + 1) * HEAD_W)
            x = x_ref[0, :, cols].astype(F32)
            xx = x * x
            s1 = jnp.sum(jnp.where(first_comp, xx, 0.0), axis=-1, keepdims=True)
            s2 = jnp.sum(jnp.where(first_comp, 0.0, xx), axis=-1, keepdims=True)
            ms = jnp.where(first_comp, s1, s2) * (1.0 / DIFF_DQK)
            t = x * lax.rsqrt(ms + EPS) * gain
            rot = jnp.where(first_half, -pltpu.roll(t, LANES - DIFF_DQK // 4, 1),
                            pltpu.roll(t, DIFF_DQK // 4, 1))
            y = t * cos + rot * sin
            if out_scale != 1.0:
                y = y * out_scale
            o_ref[0, :, cols] = y.astype(o_ref.dtype)


def _qk_prepare(q_b, k_b, cos_tab, sin_tab, gains):
    n_batch, n_tok, width = q_b.shape
    tok_spec = pl.BlockSpec((1, ROW_TILE, width), lambda b, i: (b, i, 0))
    tab_spec = pl.BlockSpec((ROW_TILE, LANES), lambda b, i: (i, 0))
    return pl.pallas_call(
        _qk_prep_kernel,
        grid=(n_batch, n_tok // ROW_TILE),
        in_specs=[tok_spec, tok_spec, tab_spec, tab_spec, pl.BlockSpec((8, LANES), lambda b, i: (0, 0))],
        out_specs=[tok_spec, tok_spec],
        out_shape=[jax.ShapeDtypeStruct((n_batch, n_tok, width), BF16)] * 2,
        compiler_params=pltpu.CompilerParams(dimension_semantics=("arbitrary", "arbitrary")),
        name="qk_norm_rope",
    )(q_b, k_b, cos_tab, sin_tab, gains)


ATT_TQ = 512
ATT_TK = 768


def _diff_attn_kernel(n_kv, q_ref, k_ref, v_ref, z_ref, lam_ref, normw_ref, o_ref):
    tq = q_ref.shape[1]
    q = q_ref[0]
    lane = lax.broadcasted_iota(jnp.int32, q.shape, 1)
    zero = jnp.zeros_like(q)
    qs = jnp.concatenate([jnp.where(lane < DIFF_DQK, q, zero), jnp.where(lane < DIFF_DQK, zero, q)], axis=0)

    def scores(j):
        start = pl.multiple_of(j * ATT_TK, ATT_TK)
        return lax.dot_general(qs, k_ref[0, pl.ds(start, ATT_TK), :], (((1,), (1,)), ((), ())),
                               preferred_element_type=F32)

    def softmax_block(s, m, l):
        m_new = jnp.maximum(m, jnp.max(s, axis=-1, keepdims=True))
        alpha = jnp.exp(m - m_new)
        p = jnp.exp(s - m_new)
        l = alpha * l + jnp.sum(p, axis=-1, keepdims=True)
        return p.astype(BF16), alpha, m_new, l

    def accumulate(j, p, alpha, acc):
        start = pl.multiple_of(j * ATT_TK, ATT_TK)
        return alpha * acc + jnp.dot(p, v_ref[0, pl.ds(start, ATT_TK), :], preferred_element_type=F32)

    m = jnp.full((2 * tq, 1), -jnp.inf, F32)
    l = jnp.zeros((2 * tq, 1), F32)
    acc = jnp.zeros((2 * tq, HEAD_W), F32)
    for j in range(n_kv):
        p, alpha, m, l = softmax_block(scores(j), m, l)
        acc = accumulate(j, p, alpha, acc)
    o = acc / l
    lam_p = lam_ref[...]
    lam = (jnp.exp(jnp.sum(lam_p[0:1, :] * lam_p[1:2, :], axis=-1, keepdims=True))
           - jnp.exp(jnp.sum(lam_p[2:3, :] * lam_p[3:4, :], axis=-1, keepdims=True)) + LAMBDA_INIT)
    o = o[:tq] - lam * o[tq:]
    y = o * lax.rsqrt(jnp.mean(o * o, axis=-1, keepdims=True) + EPS) * normw_ref[0:1, :]
    o_ref[0] = (y * (1.0 - LAMBDA_INIT) * _silu(z_ref[0].astype(F32))).astype(o_ref.dtype)


def _diff_attention(qn, kn, v_b, z_b, lam_par, norm_w, n_lat):
    n_batch, n_tok, _ = qn.shape
    q_spec = pl.BlockSpec((1, ATT_TQ, HEAD_W), lambda b, h, i: (b, i, h))
    kv_spec = pl.BlockSpec((1, n_tok, HEAD_W), lambda b, h, i: (b, 0, h))
    par_spec = pl.BlockSpec((8, LANES), lambda b, h, i: (0, 0))
    return pl.pallas_call(
        functools.partial(_diff_attn_kernel, n_tok // ATT_TK),
        grid=(n_batch, HEADS, n_lat // ATT_TQ),
        in_specs=[q_spec, kv_spec, kv_spec, q_spec, par_spec, par_spec],
        out_specs=q_spec,
        out_shape=jax.ShapeDtypeStruct((n_batch, n_lat, HEADS * HEAD_W), BF16),
        compiler_params=pltpu.CompilerParams(
            dimension_semantics=("arbitrary", "arbitrary", "arbitrary"), vmem_limit_bytes=VMEM_LIMIT),
        name="diff_attention",
    )(qn, kn, v_b, z_b, lam_par, norm_w)


def _mixer_out_kernel(x_ref, of_ref, ob_ref, za_ref, yb_ref, gates_ref, mod_ref, normw_ref,
                      woa_ref, wob_ref, wout_ref, o_ref, ya_ref):
    b = pl.program_id(0)
    for h in range(HEADS):
        cols = slice(h * HEAD_W, (h + 1) * HEAD_W)
        o_a = of_ref[0, :, cols].astype(F32) + ob_ref[0, :, cols].astype(F32)
        y = o_a * lax.rsqrt(jnp.mean(o_a * o_a, axis=-1, keepdims=True) + EPS) * normw_ref[0:1, :]
        ya_ref[:, cols] = (y * _silu(za_ref[0, :, cols].astype(F32))).astype(BF16)
    gates = _sigmoid(gates_ref[0].astype(F32))
    merged = (gates[:, :D_MODEL] * jnp.dot(ya_ref[...], woa_ref[...], preferred_element_type=F32)
              + gates[:, D_MODEL:] * jnp.dot(yb_ref[0], wob_ref[...], preferred_element_type=F32))
    out = jnp.dot(merged.astype(BF16), wout_ref[...], preferred_element_type=F32)
    gate = mod_ref[pl.ds(b, 1), 2 * D_MODEL:3 * D_MODEL]
    o_ref[0] = x_ref[0] + gate * out


def _mixer_output(x, o_f, o_b, z_a, y_b, gates, mod, gdn_norm_w, w_oa, w_ob, w_out):
    n_batch, n_lat, _ = x.shape
    tok_spec = lambda w: pl.BlockSpec((1, ROW_TILE, w), lambda b, i: (b, i, 0))
    w_spec = pl.BlockSpec((D_MODEL, D_MODEL), lambda b, i: (0, 0))
    return pl.pallas_call(
        _mixer_out_kernel,
        grid=(n_batch, n_lat // ROW_TILE),
        in_specs=[tok_spec(D_MODEL), tok_spec(GDN_WIDTH), tok_spec(GDN_WIDTH), tok_spec(GDN_WIDTH),
                  tok_spec(D_MODEL), tok_spec(2 * D_MODEL),
                  pl.BlockSpec((8, 3 * D_MODEL), lambda b, i: (0, 0)),
                  pl.BlockSpec((8, LANES), lambda b, i: (0, 0)),
                  w_spec, w_spec, w_spec],
        out_specs=tok_spec(D_MODEL),
        out_shape=jax.ShapeDtypeStruct(x.shape, F32),
        scratch_shapes=[pltpu.VMEM((ROW_TILE, GDN_WIDTH), BF16)],
        compiler_params=pltpu.CompilerParams(
            dimension_semantics=("arbitrary", "arbitrary"), vmem_limit_bytes=VMEM_LIMIT),
        name="mixer_output",
    )(x, o_f, o_b, z_a, y_b, gates, mod, gdn_norm_w, w_oa, w_ob, w_out)


def _pad_rows(a, rows=8):
    return jnp.pad(a, ((0, rows - a.shape[0]), (0, 0)))


def _pad_lanes(a, lanes=LANES):
    return jnp.pad(a, ((0, 0), (0, lanes - a.shape[1])))


def _rope_tables(n_lat, n_ctx):
    rows = n_lat // GRID_W
    row = jnp.broadcast_to(jnp.arange(rows)[:, None], (rows, GRID_W)).reshape(-1).astype(F32)
    col = jnp.broadcast_to(jnp.arange(GRID_W)[None, :], (rows, GRID_W)).reshape(-1).astype(F32)
    n_freq = DIFF_DQK // 4
    inv_freq = ROPE_THETA ** (-jnp.arange(n_freq, dtype=F32) / n_freq)
    ang_r = row[:, None] * inv_freq
    ang_c = col[:, None] * inv_freq
    ang = jnp.concatenate([ang_r, ang_r, ang_c, ang_c] * 2, axis=-1)
    cos = jnp.concatenate([jnp.cos(ang), jnp.ones((n_ctx, LANES), F32)], axis=0)
    sin = jnp.concatenate([jnp.sin(ang), jnp.zeros((n_ctx, LANES), F32)], axis=0)
    return cos, sin


def kernel(x, c, ctx, c_ctx, w_ada, b_ada, w_in, conv_w, a_log, dt_bias, gdn_norm_w, q_norm_w, k_norm_w,
           lambda_q1, lambda_k1, lambda_q2, lambda_k2, diff_norm_w, w_oa, w_ob, w_out):
    assert w_ada.shape[0] == 1, "single-layer block"
    n_batch, n_lat, _ = x.shape
    n_ctx = ctx.shape[1]
    assert n_ctx == ROW_TILE and n_lat % ROW_TILE == 0 and n_batch < 8

    bounds = np.cumsum((0,) + IN_SIZES)
    piece = lambda j: w_in[0][:, bounds[j]:bounds[j + 1]]
    ab_cols = _pad_lanes(jnp.concatenate([piece(2), piece(3)], axis=1))
    w_packed = jnp.concatenate([piece(0), piece(1), piece(4), piece(5), piece(6), piece(7), piece(8)],
                               axis=1).astype(BF16)
    w_packed = jnp.concatenate([w_packed, ab_cols.astype(BF16)], axis=1)
    cc = _pad_rows(jnp.concatenate([c, c_ctx[None, :]], axis=0))
    conv_w8 = _pad_rows(conv_w[0])
    gpar = _pad_rows(_pad_lanes(jnp.stack([a_log[0].reshape(-1), dt_bias[0].reshape(-1)])))
    gains = _pad_rows(jnp.stack([jnp.tile(q_norm_w[0], 2), jnp.tile(k_norm_w[0], 2)]))
    lam_par = _pad_rows(_pad_lanes(jnp.stack([lambda_q1[0], lambda_k1[0], lambda_q2[0], lambda_k2[0]])))
    cos_tab, sin_tab = _rope_tables(n_lat, n_ctx)

    mod = _ada_modulation(cc, w_ada[0], b_ada)
    qkv_a, z_a, q_b, k_b, v_b, z_b, gates, ab = _input_projection(x, ctx, mod, w_packed)
    q_a, k_a, v_a, gb = _gdn_prepare(qkv_a, ab, conv_w8, gpar, n_lat)
    o_f, o_b = _gdn_scan(q_a, k_a, v_a, gb, n_lat)
    qn, kn = _qk_prepare(q_b, k_b, cos_tab, sin_tab, gains)
    y_b = _diff_attention(qn, kn, v_b, z_b, lam_par, _pad_rows(diff_norm_w), n_lat)
    return _mixer_output(x, o_f, o_b, z_a, y_b, gates, mod, _pad_rows(gdn_norm_w),
                         w_oa[0].astype(BF16), w_ob[0].astype(BF16), w_out[0].astype(BF16))
```

```python
import functools
import math

import jax
import jax.numpy as jnp
import numpy as np
from jax import lax
from jax.experimental import pallas as pl
from jax.experimental.pallas import tpu as pltpu

F32 = jnp.float32
BF16 = jnp.bfloat16

D_MODEL = 1024
GRID_W = 64
EPS = 1e-6
HEADS = 8
HEAD_W = 128
GDN_WIDTH = HEADS * HEAD_W
CONV_K = 5
CHUNK = 64
DIFF_DQK = 64
ROPE_THETA = 10000.0
LAMBDA_INIT = 0.8 - 0.6 * math.exp(-0.3 * 0)
IN_SIZES = (3 * GDN_WIDTH, GDN_WIDTH, 2 * HEADS, 2 * HEADS, 1024, 1024, 1024, 1024, 2 * D_MODEL)

ROW_TILE = 256
HALO = 16
LANES = 128
VMEM_LIMIT = 56 * 1024 * 1024


def _bdot(a, b):
    return jnp.dot(a.astype(BF16), b.astype(BF16), preferred_element_type=F32)


def _bdot_nt(a, b):
    return lax.dot_general(a.astype(BF16), b.astype(BF16), (((1,), (1,)), ((), ())),
                           preferred_element_type=F32)


def _bdot_tn(a, b):
    return lax.dot_general(a.astype(BF16), b.astype(BF16), (((0,), (0,)), ((), ())),
                           preferred_element_type=F32)


def _silu(x):
    return x * (1.0 / (1.0 + jnp.exp(-x)))


def _sigmoid(x):
    return 1.0 / (1.0 + jnp.exp(-x))


def _split3(x):
    hi = x.astype(BF16)
    r1 = x - hi.astype(F32)
    mid = r1.astype(BF16)
    lo = (r1 - mid.astype(F32)).astype(BF16)
    return hi, mid, lo


def _ada_kernel(c_ref, w_ref, b_ref, o_ref):
    a = _split3(_silu(c_ref[...]))
    w = _split3(w_ref[...])
    acc = b_ref[...]
    for i, j in ((2, 0), (1, 1), (0, 2), (1, 0), (0, 1), (0, 0)):
        acc = acc + jnp.dot(a[i], w[j], preferred_element_type=F32)
    o_ref[...] = acc


def _ada_modulation(cc, w_ada, b_ada):
    n_col = w_ada.shape[1] // D_MODEL
    return pl.pallas_call(
        _ada_kernel,
        grid=(n_col,),
        in_specs=[pl.BlockSpec((8, D_MODEL), lambda j: (0, 0)),
                  pl.BlockSpec((D_MODEL, D_MODEL), lambda j: (0, j)),
                  pl.BlockSpec((1, D_MODEL), lambda j: (0, j))],
        out_specs=pl.BlockSpec((8, D_MODEL), lambda j: (0, j)),
        out_shape=jax.ShapeDtypeStruct((8, w_ada.shape[1]), F32),
        name="ada_modulation",
    )(cc, w_ada, b_ada)


IN_GROUPS = (("qkv_a", 3 * GDN_WIDTH, BF16), ("z_a", GDN_WIDTH, BF16), ("q_b", 1024, BF16),
             ("k_b", 1024, BF16), ("v_b", 1024, BF16), ("z_b", 1024, BF16),
             ("gates", 2 * D_MODEL, BF16), ("ab", LANES, F32))
IN_COL_CHUNK = 1024


def _inproj_kernel(n_lat_tiles, n_batch, x_ref, ctx_ref, mod_ref, w_ref, *out_refs):
    b = pl.program_id(0)
    i = pl.program_id(1)
    is_ctx = i == n_lat_tiles
    xt = jnp.where(is_ctx, ctx_ref[0], x_ref[0])
    mod = mod_ref[pl.ds(jnp.where(is_ctx, n_batch, b), 1), :]
    shift = mod[:, 0:D_MODEL]
    scale = mod[:, D_MODEL:2 * D_MODEL]
    ms = jnp.mean(xt * xt, axis=-1, keepdims=True)
    h = (xt * lax.rsqrt(ms + EPS) * (1.0 + scale) + shift).astype(BF16)
    off = 0
    for o_ref, (_, width, _) in zip(out_refs, IN_GROUPS):
        for c0 in range(0, width, IN_COL_CHUNK):
            cw = min(IN_COL_CHUNK, width - c0)
            o_ref[0, :, c0:c0 + cw] = jnp.dot(
                h, w_ref[:, off + c0:off + c0 + cw], preferred_element_type=F32).astype(o_ref.dtype)
        off += width


def _input_projection(x, ctx, mod, w_packed):
    n_batch, n_lat, _ = x.shape
    n_lat_tiles = n_lat // ROW_TILE
    n_tok = n_lat + ctx.shape[1]
    total_cols = w_packed.shape[1]
    out_shape = [jax.ShapeDtypeStruct((n_batch, n_tok, w), dt) for _, w, dt in IN_GROUPS]
    out_specs = [pl.BlockSpec((1, ROW_TILE, w), lambda b, i: (b, i, 0)) for _, w, _ in IN_GROUPS]
    return pl.pallas_call(
        functools.partial(_inproj_kernel, n_lat_tiles, n_batch),
        grid=(n_batch, n_lat_tiles + 1),
        in_specs=[
            pl.BlockSpec((1, ROW_TILE, D_MODEL), lambda b, i: (b, jnp.minimum(i, n_lat_tiles - 1), 0)),
            pl.BlockSpec((1, ROW_TILE, D_MODEL), lambda b, i: (b, 0, 0)),
            pl.BlockSpec((8, 3 * D_MODEL), lambda b, i: (0, 0)),
            pl.BlockSpec((D_MODEL, total_cols), lambda b, i: (0, 0), pipeline_mode=pl.Buffered(1)),
        ],
        out_specs=out_specs,
        out_shape=out_shape,
        compiler_params=pltpu.CompilerParams(
            dimension_semantics=("arbitrary", "arbitrary"), vmem_limit_bytes=VMEM_LIMIT),
        name="input_projection",
    )(x, ctx, mod, w_packed)


def _gdn_prep_kernel(n_lat_tiles, cur_ref, prev_ref, next_ref, convw_ref, ab_ref, gpar_ref,
                     q_ref, k_ref, v_ref, gb_ref, xp_ref):
    i = pl.program_id(1)
    zero_prev = (i == 0) | (i == n_lat_tiles)
    zero_next = i >= n_lat_tiles - 1
    xp_ref[0:HALO, :] = jnp.where(zero_prev, 0.0, prev_ref[0].astype(F32))
    xp_ref[HALO:HALO + ROW_TILE, :] = cur_ref[0].astype(F32)
    xp_ref[HALO + ROW_TILE:, :] = jnp.where(zero_next, 0.0, next_ref[0].astype(F32))
    pad = CONV_K // 2
    outs = (q_ref, k_ref, v_ref)
    for g in range(3 * HEADS):
        cols = slice(g * HEAD_W, (g + 1) * HEAD_W)
        y = None
        for j in range(CONV_K):
            term = xp_ref[HALO - pad + j:HALO - pad + j + ROW_TILE, cols] * convw_ref[j:j + 1, cols]
            y = term if y is None else y + term
        y = _silu(y)
        which, head = divmod(g, HEADS)
        if which < 2:
            y = y * lax.rsqrt(jnp.sum(y * y, axis=-1, keepdims=True) + EPS)
        if which == 0:
            y = y * (HEAD_W ** -0.5)
        outs[which][0, :, head * HEAD_W:(head + 1) * HEAD_W] = y.astype(BF16)
    ab = ab_ref[0]
    lane = lax.broadcasted_iota(jnp.int32, ab.shape, 1)
    a_log = gpar_ref[0:1, :]
    dt_bias = gpar_ref[1:2, :]
    xs = ab + dt_bias
    softplus = jnp.maximum(xs, 0.0) + jnp.log1p(jnp.exp(-jnp.abs(xs)))
    g_all = -jnp.exp(a_log) * softplus
    gb_ref[0] = jnp.where(lane < 2 * HEADS, g_all, jnp.where(lane < 4 * HEADS, _sigmoid(ab), 0.0))


def _gdn_prepare(qkv_a, ab, conv_w8, gpar, n_lat):
    n_batch, n_tok, width = qkv_a.shape
    n_lat_tiles = n_lat // ROW_TILE
    n_tiles = n_tok // ROW_TILE
    halo_per_tile = ROW_TILE // HALO
    n_halo = n_tok // HALO
    tok_spec = lambda w: pl.BlockSpec((1, ROW_TILE, w), lambda b, i: (b, i, 0))
    return pl.pallas_call(
        functools.partial(_gdn_prep_kernel, n_lat_tiles),
        grid=(n_batch, n_tiles),
        in_specs=[
            tok_spec(width),
            pl.BlockSpec((1, HALO, width), lambda b, i: (b, jnp.maximum(i * halo_per_tile - 1, 0), 0)),
            pl.BlockSpec((1, HALO, width),
                         lambda b, i: (b, jnp.minimum((i + 1) * halo_per_tile, n_halo - 1), 0)),
            pl.BlockSpec((8, width), lambda b, i: (0, 0)),
            tok_spec(LANES),
            pl.BlockSpec((8, LANES), lambda b, i: (0, 0)),
        ],
        out_specs=[tok_spec(GDN_WIDTH), tok_spec(GDN_WIDTH), tok_spec(GDN_WIDTH), tok_spec(LANES)],
        out_shape=[jax.ShapeDtypeStruct((n_batch, n_tok, GDN_WIDTH), BF16)] * 3
        + [jax.ShapeDtypeStruct((n_batch, n_tok, LANES), F32)],
        scratch_shapes=[pltpu.VMEM((ROW_TILE + 2 * HALO, width), F32)],
        compiler_params=pltpu.CompilerParams(
            dimension_semantics=("arbitrary", "arbitrary"), vmem_limit_bytes=VMEM_LIMIT),
        name="gdn_prepare",
    )(qkv_a, qkv_a, qkv_a, conv_w8, ab, gpar)


GROUP = 4
GSIZE = GROUP * CHUNK
N_LEVELS = 5
M_EYE, M_PAIR, M_LEVEL0 = 0, 1, 2
M_INCL = (M_LEVEL0 + N_LEVELS, M_LEVEL0 + N_LEVELS + 2)
M_STRICT = (M_LEVEL0 + N_LEVELS + 1, M_LEVEL0 + N_LEVELS + 3)


def _scan_masks():
    r = np.arange(GSIZE)[:, None]
    c = np.arange(GSIZE)[None, :]
    blk = (r // CHUNK) == (c // CHUNK)
    planes = [r == c, (r >> 1) == (c >> 1)]
    for shift in range(1, N_LEVELS + 1):
        planes.append(((r >> (shift + 1)) == (c >> (shift + 1))) & ((r >> shift) != (c >> shift)))
    planes += [blk & (r >= c), blk & (r > c), blk & (r <= c), blk & (r < c)]
    return np.stack(planes).astype(np.float32)


def _gdn_scan_kernel(qf_ref, kf_ref, vf_ref, gbf_ref, qb_ref, kb_ref, vb_ref, gbb_ref, mask_ref,
                     of_ref, ob_ref, s_ref):
    @pl.when(pl.program_id(1) == 0)
    def _():
        s_ref[...] = jnp.zeros_like(s_ref)

    rows_of = lambda c: slice(c * CHUNK, (c + 1) * CHUNK)
    stack_rows = lambda parts: jnp.concatenate(parts, axis=0)

    groups = []
    for d, (q_ref, k_ref, v_ref, gb_ref, o_ref) in enumerate(
            ((qf_ref, kf_ref, vf_ref, gbf_ref, of_ref), (qb_ref, kb_ref, vb_ref, gbb_ref, ob_ref))):
        last = CHUNK - 1 if d == 0 else 0
        gb = gb_ref[0]
        tri = mask_ref[M_INCL[d], 0:CHUNK, 0:CHUNK].astype(BF16)
        gc = None
        for part in _split3(gb):
            term = jnp.dot(tri, part, preferred_element_type=F32)
            gc = term if gc is None else gc + term
        gc_t = gc.T
        for half in range(HEADS // GROUP):
            heads = [half * GROUP + c for c in range(GROUP)]
            lanes = [d * HEADS + h for h in heads]
            head_cols = [slice(h * HEAD_W, (h + 1) * HEAD_W) for h in heads]
            groups.append(dict(
                d=d, lanes=lanes, head_cols=head_cols, o_ref=o_ref,
                q=stack_rows([q_ref[0, :, cs].astype(F32) for cs in head_cols]),
                k=stack_rows([k_ref[0, :, cs].astype(F32) for cs in head_cols]),
                v=stack_rows([v_ref[0, :, cs].astype(F32) for cs in head_cols]),
                g_col=stack_rows([gc[:, l:l + 1] for l in lanes]),
                g_row=jnp.concatenate([gc_t[l:l + 1, :] for l in lanes], axis=1),
                g_last=stack_rows([jnp.broadcast_to(gc[last:last + 1, l:l + 1], (CHUNK, 1)) for l in lanes]),
                beta=stack_rows([gb[:, 2 * HEADS + l:2 * HEADS + l + 1] for l in lanes])))

    for g in groups:
        incl = mask_ref[M_INCL[g["d"]]]
        decay = jnp.exp((g["g_col"] - g["g_row"]) * incl) * incl
        kq = _bdot_nt(stack_rows([g["k"] * g["beta"], g["q"]]), g["k"])
        g["a"] = kq[:GSIZE] * decay * mask_ref[M_STRICT[g["d"]]]
        g["qk"] = kq[GSIZE:] * decay
        g["t"] = mask_ref[M_EYE] - g["a"] * mask_ref[M_PAIR]
    for level in range(N_LEVELS):
        for g in groups:
            g["x"] = _bdot(g["a"] * mask_ref[M_LEVEL0 + level], g["t"])
        for g in groups:
            g["t"] = g["t"] - _bdot(g["t"], g["x"])
    for g in groups:
        e_g = jnp.exp(g["g_col"])
        uw = _bdot(g["t"], jnp.concatenate([g["v"] * g["beta"], g["k"] * (g["beta"] * e_g)], axis=1))
        g["u"] = uw[:, :HEAD_W]
        g["w"] = uw[:, HEAD_W:]
        g["qd"] = g["q"] * e_g
        g["k_dec"] = g["k"] * jnp.exp(g["g_last"] - g["g_col"])

    for g in groups:
        v_new, o_inter = [], []
        for c, lane in enumerate(g["lanes"]):
            s = s_ref[lane]
            ws_qs = _bdot(stack_rows([g["w"][rows_of(c)], g["qd"][rows_of(c)]]), s)
            v_new.append(g["u"][rows_of(c)] - ws_qs[:CHUNK])
            o_inter.append(ws_qs[CHUNK:])
            s_ref[lane] = (s * jnp.exp(g["g_last"][c * CHUNK:c * CHUNK + 1, :])
                           + _bdot_tn(g["k_dec"][rows_of(c)], v_new[-1]))
        o = stack_rows(o_inter) + _bdot(g["qk"], stack_rows(v_new))
        for c, cs in enumerate(g["head_cols"]):
            g["o_ref"][0, :, cs] = o[rows_of(c)].astype(g["o_ref"].dtype)


def _gdn_scan(q, k, v, gb, n_lat):
    n_batch, n_tok, _ = q.shape
    n_lat_chunks = n_lat // CHUNK
    n_ctx_chunks = (n_tok - n_lat) // CHUNK
    n_steps = n_tok // CHUNK

    def fwd_chunk(s):
        return jnp.where(s < n_ctx_chunks, n_lat_chunks + s, s - n_ctx_chunks)

    def bwd_chunk(s):
        return jnp.where(s < n_ctx_chunks, n_lat_chunks + n_ctx_chunks - 1 - s,
                         n_lat_chunks - 1 - (s - n_ctx_chunks))

    f_spec = lambda w: pl.BlockSpec((1, CHUNK, w), lambda b, s: (b, fwd_chunk(s), 0))
    b_spec = lambda w: pl.BlockSpec((1, CHUNK, w), lambda b, s: (b, bwd_chunk(s), 0))
    masks = jnp.asarray(_scan_masks())
    return pl.pallas_call(
        _gdn_scan_kernel,
        grid=(n_batch, n_steps),
        in_specs=[f_spec(GDN_WIDTH)] * 3 + [f_spec(LANES)] + [b_spec(GDN_WIDTH)] * 3 + [b_spec(LANES)]
        + [pl.BlockSpec(masks.shape, lambda b, s: (0, 0, 0))],
        out_specs=[f_spec(GDN_WIDTH), b_spec(GDN_WIDTH)],
        out_shape=[jax.ShapeDtypeStruct((n_batch, n_tok, GDN_WIDTH), BF16)] * 2,
        scratch_shapes=[pltpu.VMEM((2 * HEADS, HEAD_W, HEAD_W), F32)],
        compiler_params=pltpu.CompilerParams(
            dimension_semantics=("arbitrary", "arbitrary"), vmem_limit_bytes=VMEM_LIMIT),
        name="gdn_scan",
    )(q, k, v, gb, q, k, v, gb, masks)


def _qk_prep_kernel(q_ref, k_ref, cos_ref, sin_ref, gain_ref, qo_ref, ko_ref):
    cos = cos_ref[...]
    sin = sin_ref[...]
    lane = lax.broadcasted_iota(jnp.int32, cos.shape, 1)
    first_comp = lane < DIFF_DQK
    first_half = (lane % (DIFF_DQK // 2)) < (DIFF_DQK // 4)
    for x_ref, o_ref, gain, out_scale in ((q_ref, qo_ref, gain_ref[0:1, :], DIFF_DQK ** -0.5),
                                          (k_ref, ko_ref, gain_ref[1:2, :], 1.0)):
        for h in range(HEADS):
            cols = slice(h * HEAD_W, (h + 1) * HEAD_W)
            x = x_ref[0, :, cols].astype(F32)
            xx = x * x
            s1 = jnp.sum(jnp.where(first_comp, xx, 0.0), axis=-1, keepdims=True)
            s2 = jnp.sum(jnp.where(first_comp, 0.0, xx), axis=-1, keepdims=True)
            ms = jnp.where(first_comp, s1, s2) * (1.0 / DIFF_DQK)
            t = x * lax.rsqrt(ms + EPS) * gain
            rot = jnp.where(first_half, -pltpu.roll(t, LANES - DIFF_DQK // 4, 1),
                            pltpu.roll(t, DIFF_DQK // 4, 1))
            y = t * cos + rot * sin
            if out_scale != 1.0:
                y = y * out_scale
            o_ref[0, :, cols] = y.astype(o_ref.dtype)


def _qk_prepare(q_b, k_b, cos_tab, sin_tab, gains):
    n_batch, n_tok, width = q_b.shape
    tok_spec = pl.BlockSpec((1, ROW_TILE, width), lambda b, i: (b, i, 0))
    tab_spec = pl.BlockSpec((ROW_TILE, LANES), lambda b, i: (i, 0))
    return pl.pallas_call(
        _qk_prep_kernel,
        grid=(n_batch, n_tok // ROW_TILE),
        in_specs=[tok_spec, tok_spec, tab_spec, tab_spec, pl.BlockSpec((8, LANES), lambda b, i: (0, 0))],
        out_specs=[tok_spec, tok_spec],
        out_shape=[jax.ShapeDtypeStruct((n_batch, n_tok, width), BF16)] * 2,
        compiler_params=pltpu.CompilerParams(dimension_semantics=("arbitrary", "arbitrary")),
        name="qk_norm_rope",
    )(q_b, k_b, cos_tab, sin_tab, gains)


ATT_TQ = 512
ATT_TK = 768


LOG2E = 1.4426950408889634
BOUND_LIMIT = 50.0
_NT = (((1,), (1,)), ((), ()))


def _diff_attn_kernel(n_kv, q_ref, k_ref, v_ref, z_ref, lam_ref, normw_ref, o_ref, kx_ref, vx_ref, kmax_ref):
    tq = q_ref.shape[1]
    lane = lax.broadcasted_iota(jnp.int32, (tq, HEAD_W), 1)
    first = lane < DIFF_DQK

    @pl.when(pl.program_id(2) == 0)
    def _():
        lane_k = lax.broadcasted_iota(jnp.int32, (ATT_TK, HEAD_W), 1)
        unit = jnp.where(lane_k == 0, 1.0, 0.0).astype(BF16)

        def widen(j, carry):
            n1, n2 = carry
            rows = pl.ds(pl.multiple_of(j * ATT_TK, ATT_TK), ATT_TK)
            kb = k_ref[0, rows, :]
            kx_ref[rows, 0:HEAD_W] = kb
            kx_ref[rows, HEAD_W:2 * HEAD_W] = unit
            vx_ref[rows, 0:HEAD_W] = v_ref[0, rows, :]
            vx_ref[rows, HEAD_W:2 * HEAD_W] = unit
            sq = kb.astype(F32) * kb.astype(F32)
            a1 = jnp.sum(jnp.where(lane_k < DIFF_DQK, sq, 0.0), axis=1, keepdims=True)
            a2 = jnp.sum(jnp.where(lane_k < DIFF_DQK, 0.0, sq), axis=1, keepdims=True)
            return (jnp.maximum(n1, jnp.max(a1, axis=0, keepdims=True)),
                    jnp.maximum(n2, jnp.max(a2, axis=0, keepdims=True)))

        zero11 = jnp.zeros((1, 1), F32)
        n1, n2 = lax.fori_loop(0, n_kv, widen, (zero11, zero11))
        kmax_ref[0:1, :] = jnp.broadcast_to(jnp.sqrt(n1), (1, LANES))
        kmax_ref[1:2, :] = jnp.broadcast_to(jnp.sqrt(n2), (1, LANES))

    q = (q_ref[0].astype(F32) * LOG2E).astype(BF16)
    zero = jnp.zeros_like(q)
    qm1 = jnp.where(first, q, zero)
    qm2 = jnp.where(first, zero, q)
    sq = q.astype(F32) * q.astype(F32)
    c1 = jnp.sqrt(jnp.sum(jnp.where(first, sq, 0.0), axis=1, keepdims=True)) * kmax_ref[0:1, 0:1]
    c2 = jnp.sqrt(jnp.sum(jnp.where(first, 0.0, sq), axis=1, keepdims=True)) * kmax_ref[1:2, 0:1]
    bound = jnp.max(jnp.maximum(c1, c2))

    def shifted_by_bound():
        aug1 = jnp.where(lane == 0, -c1, 0.0).astype(BF16)
        aug2 = jnp.where(lane == 0, -c2, 0.0).astype(BF16)
        qx = jnp.concatenate([jnp.concatenate([qm1, aug1], axis=1),
                              jnp.concatenate([qm2, aug2], axis=1)], axis=0)
        acc = jnp.zeros((2 * tq, 2 * HEAD_W), F32)
        for j in range(n_kv):
            rows = slice(j * ATT_TK, (j + 1) * ATT_TK)
            s = lax.dot_general(qx, kx_ref[rows, :], _NT, preferred_element_type=F32)
            acc = acc + jnp.dot(jnp.exp2(s).astype(BF16), vx_ref[rows, :], preferred_element_type=F32)
        return acc[:, :HEAD_W] / acc[:, HEAD_W:HEAD_W + 1]

    def running_max():
        qs = jnp.concatenate([qm1, qm2], axis=0)
        m = jnp.full((2 * tq, 1), -jnp.inf, F32)
        l = jnp.zeros((2 * tq, 1), F32)
        acc = jnp.zeros((2 * tq, HEAD_W), F32)
        for j in range(n_kv):
            rows = slice(j * ATT_TK, (j + 1) * ATT_TK)
            s = lax.dot_general(qs, k_ref[0, rows, :], _NT, preferred_element_type=F32)
            m_new = jnp.maximum(m, jnp.max(s, axis=-1, keepdims=True))
            alpha = jnp.exp2(m - m_new)
            p = jnp.exp2(s - m_new)
            l = alpha * l + jnp.sum(p, axis=-1, keepdims=True)
            acc = alpha * acc + jnp.dot(p.astype(BF16), v_ref[0, rows, :], preferred_element_type=F32)
            m = m_new
        return acc / l

    o = lax.cond(bound < BOUND_LIMIT, shifted_by_bound, running_max)
    lam_p = lam_ref[...]
    lam = (jnp.exp(jnp.sum(lam_p[0:1, :] * lam_p[1:2, :], axis=-1, keepdims=True))
           - jnp.exp(jnp.sum(lam_p[2:3, :] * lam_p[3:4, :], axis=-1, keepdims=True)) + LAMBDA_INIT)
    o = o[:tq] - lam * o[tq:]
    y = o * lax.rsqrt(jnp.mean(o * o, axis=-1, keepdims=True) + EPS) * normw_ref[0:1, :]
    o_ref[0] = (y * (1.0 - LAMBDA_INIT) * _silu(z_ref[0].astype(F32))).astype(o_ref.dtype)


def _diff_attention(qn, kn, v_b, z_b, lam_par, norm_w, n_lat):
    n_batch, n_tok, _ = qn.shape
    q_spec = pl.BlockSpec((1, ATT_TQ, HEAD_W), lambda b, h, i: (b, i, h))
    kv_spec = pl.BlockSpec((1, n_tok, HEAD_W), lambda b, h, i: (b, 0, h))
    par_spec = pl.BlockSpec((8, LANES), lambda b, h, i: (0, 0))
    return pl.pallas_call(
        functools.partial(_diff_attn_kernel, n_tok // ATT_TK),
        grid=(n_batch, HEADS, n_lat // ATT_TQ),
        in_specs=[q_spec, kv_spec, kv_spec, q_spec, par_spec, par_spec],
        out_specs=q_spec,
        out_shape=jax.ShapeDtypeStruct((n_batch, n_lat, HEADS * HEAD_W), BF16),
        scratch_shapes=[pltpu.VMEM((n_tok, 2 * HEAD_W), BF16), pltpu.VMEM((n_tok, 2 * HEAD_W), BF16),
                        pltpu.VMEM((8, LANES), F32)],
        compiler_params=pltpu.CompilerParams(
            dimension_semantics=("arbitrary", "arbitrary", "arbitrary"), vmem_limit_bytes=VMEM_LIMIT),
        name="diff_attention",
    )(qn, kn, v_b, z_b, lam_par, norm_w)


def _mixer_out_kernel(x_ref, of_ref, ob_ref, za_ref, yb_ref, gates_ref, mod_ref, normw_ref,
                      woa_ref, wob_ref, wout_ref, o_ref, ya_ref):
    b = pl.program_id(0)
    for h in range(HEADS):
        cols = slice(h * HEAD_W, (h + 1) * HEAD_W)
        o_a = of_ref[0, :, cols].astype(F32) + ob_ref[0, :, cols].astype(F32)
        y = o_a * lax.rsqrt(jnp.mean(o_a * o_a, axis=-1, keepdims=True) + EPS) * normw_ref[0:1, :]
        ya_ref[:, cols] = (y * _silu(za_ref[0, :, cols].astype(F32))).astype(BF16)
    gates = _sigmoid(gates_ref[0].astype(F32))
    merged = (gates[:, :D_MODEL] * jnp.dot(ya_ref[...], woa_ref[...], preferred_element_type=F32)
              + gates[:, D_MODEL:] * jnp.dot(yb_ref[0], wob_ref[...], preferred_element_type=F32))
    out = jnp.dot(merged.astype(BF16), wout_ref[...], preferred_element_type=F32)
    gate = mod_ref[pl.ds(b, 1), 2 * D_MODEL:3 * D_MODEL]
    o_ref[0] = x_ref[0] + gate * out


def _mixer_output(x, o_f, o_b, z_a, y_b, gates, mod, gdn_norm_w, w_oa, w_ob, w_out):
    n_batch, n_lat, _ = x.shape
    tok_spec = lambda w: pl.BlockSpec((1, ROW_TILE, w), lambda b, i: (b, i, 0))
    w_spec = pl.BlockSpec((D_MODEL, D_MODEL), lambda b, i: (0, 0))
    return pl.pallas_call(
        _mixer_out_kernel,
        grid=(n_batch, n_lat // ROW_TILE),
        in_specs=[tok_spec(D_MODEL), tok_spec(GDN_WIDTH), tok_spec(GDN_WIDTH), tok_spec(GDN_WIDTH),
                  tok_spec(D_MODEL), tok_spec(2 * D_MODEL),
                  pl.BlockSpec((8, 3 * D_MODEL), lambda b, i: (0, 0)),
                  pl.BlockSpec((8, LANES), lambda b, i: (0, 0)),
                  w_spec, w_spec, w_spec],
        out_specs=tok_spec(D_MODEL),
        out_shape=jax.ShapeDtypeStruct(x.shape, F32),
        scratch_shapes=[pltpu.VMEM((ROW_TILE, GDN_WIDTH), BF16)],
        compiler_params=pltpu.CompilerParams(
            dimension_semantics=("arbitrary", "arbitrary"), vmem_limit_bytes=VMEM_LIMIT),
        name="mixer_output",
    )(x, o_f, o_b, z_a, y_b, gates, mod, gdn_norm_w, w_oa, w_ob, w_out)


def _pad_rows(a, rows=8):
    return jnp.pad(a, ((0, rows - a.shape[0]), (0, 0)))


def _pad_lanes(a, lanes=LANES):
    return jnp.pad(a, ((0, 0), (0, lanes - a.shape[1])))


def _rope_tables(n_lat, n_ctx):
    rows = n_lat // GRID_W
    row = jnp.broadcast_to(jnp.arange(rows)[:, None], (rows, GRID_W)).reshape(-1).astype(F32)
    col = jnp.broadcast_to(jnp.arange(GRID_W)[None, :], (rows, GRID_W)).reshape(-1).astype(F32)
    n_freq = DIFF_DQK // 4
    inv_freq = ROPE_THETA ** (-jnp.arange(n_freq, dtype=F32) / n_freq)
    ang_r = row[:, None] * inv_freq
    ang_c = col[:, None] * inv_freq
    ang = jnp.concatenate([ang_r, ang_r, ang_c, ang_c] * 2, axis=-1)
    cos = jnp.concatenate([jnp.cos(ang), jnp.ones((n_ctx, LANES), F32)], axis=0)
    sin = jnp.concatenate([jnp.sin(ang), jnp.zeros((n_ctx, LANES), F32)], axis=0)
    return cos, sin


def kernel(x, c, ctx, c_ctx, w_ada, b_ada, w_in, conv_w, a_log, dt_bias, gdn_norm_w, q_norm_w, k_norm_w,
           lambda_q1, lambda_k1, lambda_q2, lambda_k2, diff_norm_w, w_oa, w_ob, w_out):
    assert w_ada.shape[0] == 1, "single-layer block"
    n_batch, n_lat, _ = x.shape
    n_ctx = ctx.shape[1]
    assert n_ctx == ROW_TILE and n_lat % ROW_TILE == 0 and n_batch < 8

    bounds = np.cumsum((0,) + IN_SIZES)
    piece = lambda j: w_in[0][:, bounds[j]:bounds[j + 1]]
    ab_cols = _pad_lanes(jnp.concatenate([piece(2), piece(3)], axis=1))
    w_packed = jnp.concatenate([piece(0), piece(1), piece(4), piece(5), piece(6), piece(7), piece(8)],
                               axis=1).astype(BF16)
    w_packed = jnp.concatenate([w_packed, ab_cols.astype(BF16)], axis=1)
    cc = _pad_rows(jnp.concatenate([c, c_ctx[None, :]], axis=0))
    conv_w8 = _pad_rows(conv_w[0])
    gpar = _pad_rows(_pad_lanes(jnp.stack([a_log[0].reshape(-1), dt_bias[0].reshape(-1)])))
    gains = _pad_rows(jnp.stack([jnp.tile(q_norm_w[0], 2), jnp.tile(k_norm_w[0], 2)]))
    lam_par = _pad_rows(_pad_lanes(jnp.stack([lambda_q1[0], lambda_k1[0], lambda_q2[0], lambda_k2[0]])))
    cos_tab, sin_tab = _rope_tables(n_lat, n_ctx)

    mod = _ada_modulation(cc, w_ada[0], b_ada)
    qkv_a, z_a, q_b, k_b, v_b, z_b, gates, ab = _input_projection(x, ctx, mod, w_packed)
    q_a, k_a, v_a, gb = _gdn_prepare(qkv_a, ab, conv_w8, gpar, n_lat)
    o_f, o_b = _gdn_scan(q_a, k_a, v_a, gb, n_lat)
    qn, kn = _qk_prepare(q_b, k_b, cos_tab, sin_tab, gains)
    y_b = _diff_attention(qn, kn, v_b, z_b, lam_par, _pad_rows(diff_norm_w), n_lat)
    return _mixer_output(x, o_f, o_b, z_a, y_b, gates, mod, _pad_rows(gdn_norm_w),
                         w_oa[0].astype(BF16), w_ob[0].astype(BF16), w_out[0].astype(BF16))
```

```python
import functools
import math

import jax
import jax.numpy as jnp
import numpy as np
from jax import lax
from jax.experimental import pallas as pl
from jax.experimental.pallas import tpu as pltpu

F32 = jnp.float32
BF16 = jnp.bfloat16

D_MODEL = 1024
GRID_W = 64
EPS = 1e-6
HEADS = 8
HEAD_W = 128
GDN_WIDTH = HEADS * HEAD_W
CONV_K = 5
CHUNK = 64
DIFF_DQK = 64
ROPE_THETA = 10000.0
LAMBDA_INIT = 0.8 - 0.6 * math.exp(-0.3 * 0)
IN_SIZES = (3 * GDN_WIDTH, GDN_WIDTH, 2 * HEADS, 2 * HEADS, 1024, 1024, 1024, 1024, 2 * D_MODEL)

ROW_TILE = 256
HALO = 16
LANES = 128
VMEM_LIMIT = 56 * 1024 * 1024


def _bdot(a, b):
    return jnp.dot(a.astype(BF16), b.astype(BF16), preferred_element_type=F32)


def _bdot_nt(a, b):
    return lax.dot_general(a.astype(BF16), b.astype(BF16), (((1,), (1,)), ((), ())),
                           preferred_element_type=F32)


def _bdot_tn(a, b):
    return lax.dot_general(a.astype(BF16), b.astype(BF16), (((0,), (0,)), ((), ())),
                           preferred_element_type=F32)


def _silu(x):
    return x * (1.0 / (1.0 + jnp.exp(-x)))


def _sigmoid(x):
    return 1.0 / (1.0 + jnp.exp(-x))


def _split3(x):
    hi = x.astype(BF16)
    r1 = x - hi.astype(F32)
    mid = r1.astype(BF16)
    lo = (r1 - mid.astype(F32)).astype(BF16)
    return hi, mid, lo


def _ada_kernel(c_ref, w_ref, b_ref, o_ref):
    a = _split3(_silu(c_ref[...]))
    w = _split3(w_ref[...])
    acc = b_ref[...]
    for i, j in ((2, 0), (1, 1), (0, 2), (1, 0), (0, 1), (0, 0)):
        acc = acc + jnp.dot(a[i], w[j], preferred_element_type=F32)
    o_ref[...] = acc


def _ada_modulation(cc, w_ada, b_ada):
    n_col = w_ada.shape[1] // D_MODEL
    return pl.pallas_call(
        _ada_kernel,
        grid=(n_col,),
        in_specs=[pl.BlockSpec((8, D_MODEL), lambda j: (0, 0)),
                  pl.BlockSpec((D_MODEL, D_MODEL), lambda j: (0, j)),
                  pl.BlockSpec((1, D_MODEL), lambda j: (0, j))],
        out_specs=pl.BlockSpec((8, D_MODEL), lambda j: (0, j)),
        out_shape=jax.ShapeDtypeStruct((8, w_ada.shape[1]), F32),
        name="ada_modulation",
    )(cc, w_ada, b_ada)


IN_GROUPS = (("qkv_a", 3 * GDN_WIDTH, BF16), ("z_a", GDN_WIDTH, BF16), ("q_b", 1024, BF16),
             ("k_b", 1024, BF16), ("v_b", 1024, BF16), ("z_b", 1024, BF16),
             ("gates", 2 * D_MODEL, BF16), ("ab", LANES, F32))
IN_COL_CHUNK = 1024


def _inproj_kernel(n_lat_tiles, n_batch, x_ref, ctx_ref, mod_ref, w_ref, *out_refs):
    b = pl.program_id(0)
    i = pl.program_id(1)
    is_ctx = i == n_lat_tiles
    xt = jnp.where(is_ctx, ctx_ref[0], x_ref[0])
    mod = mod_ref[pl.ds(jnp.where(is_ctx, n_batch, b), 1), :]
    shift = mod[:, 0:D_MODEL]
    scale = mod[:, D_MODEL:2 * D_MODEL]
    ms = jnp.mean(xt * xt, axis=-1, keepdims=True)
    h = (xt * lax.rsqrt(ms + EPS) * (1.0 + scale) + shift).astype(BF16)
    off = 0
    for o_ref, (_, width, _) in zip(out_refs, IN_GROUPS):
        for c0 in range(0, width, IN_COL_CHUNK):
            cw = min(IN_COL_CHUNK, width - c0)
            o_ref[0, :, c0:c0 + cw] = jnp.dot(
                h, w_ref[:, off + c0:off + c0 + cw], preferred_element_type=F32).astype(o_ref.dtype)
        off += width


def _input_projection(x, ctx, mod, w_packed):
    n_batch, n_lat, _ = x.shape
    n_lat_tiles = n_lat // ROW_TILE
    n_tok = n_lat + ctx.shape[1]
    total_cols = w_packed.shape[1]
    out_shape = [jax.ShapeDtypeStruct((n_batch, n_tok, w), dt) for _, w, dt in IN_GROUPS]
    out_specs = [pl.BlockSpec((1, ROW_TILE, w), lambda b, i: (b, i, 0)) for _, w, _ in IN_GROUPS]
    return pl.pallas_call(
        functools.partial(_inproj_kernel, n_lat_tiles, n_batch),
        grid=(n_batch, n_lat_tiles + 1),
        in_specs=[
            pl.BlockSpec((1, ROW_TILE, D_MODEL), lambda b, i: (b, jnp.minimum(i, n_lat_tiles - 1), 0)),
            pl.BlockSpec((1, ROW_TILE, D_MODEL), lambda b, i: (b, 0, 0)),
            pl.BlockSpec((8, 3 * D_MODEL), lambda b, i: (0, 0)),
            pl.BlockSpec((D_MODEL, total_cols), lambda b, i: (0, 0), pipeline_mode=pl.Buffered(1)),
        ],
        out_specs=out_specs,
        out_shape=out_shape,
        compiler_params=pltpu.CompilerParams(
            dimension_semantics=("arbitrary", "arbitrary"), vmem_limit_bytes=VMEM_LIMIT),
        name="input_projection",
    )(x, ctx, mod, w_packed)


def _gdn_prep_kernel(n_lat_tiles, cur_ref, prev_ref, next_ref, convw_ref, ab_ref, gpar_ref,
                     q_ref, k_ref, v_ref, gb_ref, xp_ref):
    i = pl.program_id(1)
    zero_prev = (i == 0) | (i == n_lat_tiles)
    zero_next = i >= n_lat_tiles - 1
    xp_ref[0:HALO, :] = jnp.where(zero_prev, 0.0, prev_ref[0].astype(F32))
    xp_ref[HALO:HALO + ROW_TILE, :] = cur_ref[0].astype(F32)
    xp_ref[HALO + ROW_TILE:, :] = jnp.where(zero_next, 0.0, next_ref[0].astype(F32))
    pad = CONV_K // 2
    outs = (q_ref, k_ref, v_ref)
    for g in range(3 * HEADS):
        cols = slice(g * HEAD_W, (g + 1) * HEAD_W)
        y = None
        for j in range(CONV_K):
            term = xp_ref[HALO - pad + j:HALO - pad + j + ROW_TILE, cols] * convw_ref[j:j + 1, cols]
            y = term if y is None else y + term
        y = _silu(y)
        which, head = divmod(g, HEADS)
        if which < 2:
            y = y * lax.rsqrt(jnp.sum(y * y, axis=-1, keepdims=True) + EPS)
        if which == 0:
            y = y * (HEAD_W ** -0.5)
        outs[which][0, :, head * HEAD_W:(head + 1) * HEAD_W] = y.astype(BF16)
    ab = ab_ref[0]
    lane = lax.broadcasted_iota(jnp.int32, ab.shape, 1)
    a_log = gpar_ref[0:1, :]
    dt_bias = gpar_ref[1:2, :]
    xs = ab + dt_bias
    softplus = jnp.maximum(xs, 0.0) + jnp.log1p(jnp.exp(-jnp.abs(xs)))
    g_all = -jnp.exp(a_log) * softplus
    gb_ref[0] = jnp.where(lane < 2 * HEADS, g_all, jnp.where(lane < 4 * HEADS, _sigmoid(ab), 0.0))


def _gdn_prepare(qkv_a, ab, conv_w8, gpar, n_lat):
    n_batch, n_tok, width = qkv_a.shape
    n_lat_tiles = n_lat // ROW_TILE
    n_tiles = n_tok // ROW_TILE
    halo_per_tile = ROW_TILE // HALO
    n_halo = n_tok // HALO
    tok_spec = lambda w: pl.BlockSpec((1, ROW_TILE, w), lambda b, i: (b, i, 0))
    return pl.pallas_call(
        functools.partial(_gdn_prep_kernel, n_lat_tiles),
        grid=(n_batch, n_tiles),
        in_specs=[
            tok_spec(width),
            pl.BlockSpec((1, HALO, width), lambda b, i: (b, jnp.maximum(i * halo_per_tile - 1, 0), 0)),
            pl.BlockSpec((1, HALO, width),
                         lambda b, i: (b, jnp.minimum((i + 1) * halo_per_tile, n_halo - 1), 0)),
            pl.BlockSpec((8, width), lambda b, i: (0, 0)),
            tok_spec(LANES),
            pl.BlockSpec((8, LANES), lambda b, i: (0, 0)),
        ],
        out_specs=[tok_spec(GDN_WIDTH), tok_spec(GDN_WIDTH), tok_spec(GDN_WIDTH), tok_spec(LANES)],
        out_shape=[jax.ShapeDtypeStruct((n_batch, n_tok, GDN_WIDTH), BF16)] * 3
        + [jax.ShapeDtypeStruct((n_batch, n_tok, LANES), F32)],
        scratch_shapes=[pltpu.VMEM((ROW_TILE + 2 * HALO, width), F32)],
        compiler_params=pltpu.CompilerParams(
            dimension_semantics=("arbitrary", "arbitrary"), vmem_limit_bytes=VMEM_LIMIT),
        name="gdn_prepare",
    )(qkv_a, qkv_a, qkv_a, conv_w8, ab, gpar)


GROUP = 4
GSIZE = GROUP * CHUNK
N_LEVELS = 5
M_EYE, M_PAIR, M_LEVEL0 = 0, 1, 2
M_INCL = (M_LEVEL0 + N_LEVELS, M_LEVEL0 + N_LEVELS + 2)
M_STRICT = (M_LEVEL0 + N_LEVELS + 1, M_LEVEL0 + N_LEVELS + 3)


def _scan_masks():
    r = np.arange(GSIZE)[:, None]
    c = np.arange(GSIZE)[None, :]
    blk = (r // CHUNK) == (c // CHUNK)
    planes = [r == c, (r >> 1) == (c >> 1)]
    for shift in range(1, N_LEVELS + 1):
        planes.append(((r >> (shift + 1)) == (c >> (shift + 1))) & ((r >> shift) != (c >> shift)))
    planes += [blk & (r >= c), blk & (r > c), blk & (r <= c), blk & (r < c)]
    return np.stack(planes).astype(np.float32)


def _gdn_scan_kernel(qf_ref, kf_ref, vf_ref, gbf_ref, qb_ref, kb_ref, vb_ref, gbb_ref, mask_ref,
                     of_ref, ob_ref, s_ref):
    @pl.when(pl.program_id(1) == 0)
    def _():
        s_ref[...] = jnp.zeros_like(s_ref)

    rows_of = lambda c: slice(c * CHUNK, (c + 1) * CHUNK)
    stack_rows = lambda parts: jnp.concatenate(parts, axis=0)

    groups = []
    for d, (q_ref, k_ref, v_ref, gb_ref, o_ref) in enumerate(
            ((qf_ref, kf_ref, vf_ref, gbf_ref, of_ref), (qb_ref, kb_ref, vb_ref, gbb_ref, ob_ref))):
        last = CHUNK - 1 if d == 0 else 0
        gb = gb_ref[0]
        tri = mask_ref[M_INCL[d], 0:CHUNK, 0:CHUNK].astype(BF16)
        gc = None
        for part in _split3(gb):
            term = jnp.dot(tri, part, preferred_element_type=F32)
            gc = term if gc is None else gc + term
        gc_t = gc.T
        for half in range(HEADS // GROUP):
            heads = [half * GROUP + c for c in range(GROUP)]
            lanes = [d * HEADS + h for h in heads]
            head_cols = [slice(h * HEAD_W, (h + 1) * HEAD_W) for h in heads]
            groups.append(dict(
                d=d, lanes=lanes, head_cols=head_cols, o_ref=o_ref,
                q=stack_rows([q_ref[0, :, cs].astype(F32) for cs in head_cols]),
                k=stack_rows([k_ref[0, :, cs].astype(F32) for cs in head_cols]),
                v=stack_rows([v_ref[0, :, cs].astype(F32) for cs in head_cols]),
                g_col=stack_rows([gc[:, l:l + 1] for l in lanes]),
                g_row=jnp.concatenate([gc_t[l:l + 1, :] for l in lanes], axis=1),
                g_last=stack_rows([jnp.broadcast_to(gc[last:last + 1, l:l + 1], (CHUNK, 1)) for l in lanes]),
                beta=stack_rows([gb[:, 2 * HEADS + l:2 * HEADS + l + 1] for l in lanes])))

    for g in groups:
        incl = mask_ref[M_INCL[g["d"]]]
        decay = jnp.exp((g["g_col"] - g["g_row"]) * incl) * incl
        kq = _bdot_nt(stack_rows([g["k"] * g["beta"], g["q"]]), g["k"])
        g["a"] = kq[:GSIZE] * decay * mask_ref[M_STRICT[g["d"]]]
        g["qk"] = kq[GSIZE:] * decay
        g["t"] = mask_ref[M_EYE] - g["a"] * mask_ref[M_PAIR]
    for level in range(N_LEVELS):
        for g in groups:
            g["x"] = _bdot(g["a"] * mask_ref[M_LEVEL0 + level], g["t"])
        for g in groups:
            g["t"] = g["t"] - _bdot(g["t"], g["x"])
    for g in groups:
        e_g = jnp.exp(g["g_col"])
        uw = _bdot(g["t"], jnp.concatenate([g["v"] * g["beta"], g["k"] * (g["beta"] * e_g)], axis=1))
        g["u"] = uw[:, :HEAD_W]
        g["w"] = uw[:, HEAD_W:]
        g["qd"] = g["q"] * e_g
        g["k_dec"] = g["k"] * jnp.exp(g["g_last"] - g["g_col"])

    for g in groups:
        v_new, o_inter = [], []
        for c, lane in enumerate(g["lanes"]):
            s = s_ref[lane]
            ws_qs = _bdot(stack_rows([g["w"][rows_of(c)], g["qd"][rows_of(c)]]), s)
            v_new.append(g["u"][rows_of(c)] - ws_qs[:CHUNK])
            o_inter.append(ws_qs[CHUNK:])
            s_ref[lane] = (s * jnp.exp(g["g_last"][c * CHUNK:c * CHUNK + 1, :])
                           + _bdot_tn(g["k_dec"][rows_of(c)], v_new[-1]))
        o = stack_rows(o_inter) + _bdot(g["qk"], stack_rows(v_new))
        for c, cs in enumerate(g["head_cols"]):
            g["o_ref"][0, :, cs] = o[rows_of(c)].astype(g["o_ref"].dtype)


def _gdn_scan(q, k, v, gb, n_lat):
    n_batch, n_tok, _ = q.shape
    n_lat_chunks = n_lat // CHUNK
    n_ctx_chunks = (n_tok - n_lat) // CHUNK
    n_steps = n_tok // CHUNK

    def fwd_chunk(s):
        return jnp.where(s < n_ctx_chunks, n_lat_chunks + s, s - n_ctx_chunks)

    def bwd_chunk(s):
        return jnp.where(s < n_ctx_chunks, n_lat_chunks + n_ctx_chunks - 1 - s,
                         n_lat_chunks - 1 - (s - n_ctx_chunks))

    f_spec = lambda w: pl.BlockSpec((1, CHUNK, w), lambda b, s: (b, fwd_chunk(s), 0))
    b_spec = lambda w: pl.BlockSpec((1, CHUNK, w), lambda b, s: (b, bwd_chunk(s), 0))
    masks = jnp.asarray(_scan_masks())
    return pl.pallas_call(
        _gdn_scan_kernel,
        grid=(n_batch, n_steps),
        in_specs=[f_spec(GDN_WIDTH)] * 3 + [f_spec(LANES)] + [b_spec(GDN_WIDTH)] * 3 + [b_spec(LANES)]
        + [pl.BlockSpec(masks.shape, lambda b, s: (0, 0, 0))],
        out_specs=[f_spec(GDN_WIDTH), b_spec(GDN_WIDTH)],
        out_shape=[jax.ShapeDtypeStruct((n_batch, n_tok, GDN_WIDTH), BF16)] * 2,
        scratch_shapes=[pltpu.VMEM((2 * HEADS, HEAD_W, HEAD_W), F32)],
        compiler_params=pltpu.CompilerParams(
            dimension_semantics=("arbitrary", "arbitrary"), vmem_limit_bytes=VMEM_LIMIT),
        name="gdn_scan",
    )(q, k, v, gb, q, k, v, gb, masks)


LOG2E = 1.4426950408889634


def _qk_prep_kernel(n_lat_tiles, q_ref, k_ref, cos_ref, sin_ref, gain_ref, qo_ref, ko_ref):
    cos = cos_ref[...]
    sin = sin_ref[...]
    lane = lax.broadcasted_iota(jnp.int32, cos.shape, 1)
    first_comp = lane < DIFF_DQK
    first_half = (lane % (DIFF_DQK // 2)) < (DIFF_DQK // 4)

    def norm_rope(x, gain):
        xx = x * x
        s1 = jnp.sum(jnp.where(first_comp, xx, 0.0), axis=-1, keepdims=True)
        s2 = jnp.sum(jnp.where(first_comp, 0.0, xx), axis=-1, keepdims=True)
        ms = jnp.where(first_comp, s1, s2) * (1.0 / DIFF_DQK)
        t = x * lax.rsqrt(ms + EPS) * gain
        rot = jnp.where(first_half, -pltpu.roll(t, LANES - DIFF_DQK // 4, 1),
                        pltpu.roll(t, DIFF_DQK // 4, 1))
        return t * cos + rot * sin

    for h in range(HEADS):
        cols = slice(h * HEAD_W, (h + 1) * HEAD_W)
        ko_ref[0, :, cols] = norm_rope(k_ref[0, :, cols].astype(F32), gain_ref[1:2, :]).astype(ko_ref.dtype)

    @pl.when(pl.program_id(1) < n_lat_tiles)
    def _():
        for h in range(HEADS):
            cols = slice(h * HEAD_W, (h + 1) * HEAD_W)
            y = norm_rope(q_ref[0, :, cols].astype(F32), gain_ref[0:1, :]) * (DIFF_DQK ** -0.5 * LOG2E)
            qo_ref[0, cols, :] = y.T.astype(qo_ref.dtype)


def _qk_prepare(q_b, k_b, cos_tab, sin_tab, gains, n_lat):
    n_batch, n_tok, width = q_b.shape
    n_lat_tiles = n_lat // ROW_TILE
    tok_spec = pl.BlockSpec((1, ROW_TILE, width), lambda b, i: (b, i, 0))
    tab_spec = pl.BlockSpec((ROW_TILE, LANES), lambda b, i: (i, 0))
    qt_spec = pl.BlockSpec((1, width, ROW_TILE), lambda b, i: (b, 0, jnp.minimum(i, n_lat_tiles - 1)))
    return pl.pallas_call(
        functools.partial(_qk_prep_kernel, n_lat_tiles),
        grid=(n_batch, n_tok // ROW_TILE),
        in_specs=[tok_spec, tok_spec, tab_spec, tab_spec, pl.BlockSpec((8, LANES), lambda b, i: (0, 0))],
        out_specs=[qt_spec, tok_spec],
        out_shape=[jax.ShapeDtypeStruct((n_batch, width, n_lat), BF16),
                   jax.ShapeDtypeStruct((n_batch, n_tok, width), BF16)],
        compiler_params=pltpu.CompilerParams(dimension_semantics=("arbitrary", "arbitrary")),
        name="qk_norm_rope",
    )(q_b, k_b, cos_tab, sin_tab, gains)


ATT_TQ = 512
ATT_TK = 768


BOUND_LIMIT = 50.0


def _diff_attn_kernel(n_kv, q_ref, k_ref, v_ref, z_ref, lam_ref, normw_ref, o_ref, kx_ref, vt_ref, kmax_ref):
    tq = q_ref.shape[2]

    @pl.when(pl.program_id(2) == 0)
    def _():
        lane_k = lax.broadcasted_iota(jnp.int32, (ATT_TK, HEAD_W), 1)
        unit = jnp.where(lane_k == 0, 1.0, 0.0).astype(BF16)

        def widen(j, carry):
            n1, n2 = carry
            rows = pl.ds(pl.multiple_of(j * ATT_TK, ATT_TK), ATT_TK)
            kb = k_ref[0, rows, :]
            kx_ref[rows, 0:HEAD_W] = kb
            kx_ref[rows, HEAD_W:2 * HEAD_W] = unit
            vt_ref[:, rows] = v_ref[0, rows, :].astype(F32).T.astype(BF16)
            sq = kb.astype(F32) * kb.astype(F32)
            a1 = jnp.sum(jnp.where(lane_k < DIFF_DQK, sq, 0.0), axis=1, keepdims=True)
            a2 = jnp.sum(jnp.where(lane_k < DIFF_DQK, 0.0, sq), axis=1, keepdims=True)
            return (jnp.maximum(n1, jnp.max(a1, axis=0, keepdims=True)),
                    jnp.maximum(n2, jnp.max(a2, axis=0, keepdims=True)))

        zero11 = jnp.zeros((1, 1), F32)
        n1, n2 = lax.fori_loop(0, n_kv, widen, (zero11, zero11))
        kmax_ref[0:1, :] = jnp.broadcast_to(jnp.sqrt(n1), (1, LANES))
        kmax_ref[1:2, :] = jnp.broadcast_to(jnp.sqrt(n2), (1, LANES))

    qt = q_ref[0]
    feat = lax.broadcasted_iota(jnp.int32, (HEAD_W, tq), 0)
    first = feat < DIFF_DQK
    zero = jnp.zeros_like(qt)
    qs_t = jnp.concatenate([jnp.where(first, qt, zero), jnp.where(first, zero, qt)], axis=1)
    sq = qt.astype(F32) * qt.astype(F32)
    c1 = jnp.sqrt(jnp.sum(jnp.where(first, sq, 0.0), axis=0, keepdims=True)) * kmax_ref[0:1, 0:1]
    c2 = jnp.sqrt(jnp.sum(jnp.where(first, 0.0, sq), axis=0, keepdims=True)) * kmax_ref[1:2, 0:1]
    c = jnp.concatenate([c1, c2], axis=1)
    bound = jnp.max(c)

    def shifted_by_bound():
        feat2 = lax.broadcasted_iota(jnp.int32, (HEAD_W, 2 * tq), 0)
        aug = jnp.where(feat2 == 0, -c, 0.0).astype(BF16)
        qx_t = jnp.concatenate([qs_t, aug], axis=0)
        l = jnp.zeros((1, 2 * tq), F32)
        acc = jnp.zeros((HEAD_W, 2 * tq), F32)
        for j in range(n_kv):
            rows = slice(j * ATT_TK, (j + 1) * ATT_TK)
            p = jnp.exp2(jnp.dot(kx_ref[rows, :], qx_t, preferred_element_type=F32))
            l = l + jnp.sum(p, axis=0, keepdims=True)
            acc = acc + jnp.dot(vt_ref[:, rows], p.astype(BF16), preferred_element_type=F32)
        return acc / l

    def running_max():
        m = jnp.full((1, 2 * tq), -jnp.inf, F32)
        l = jnp.zeros((1, 2 * tq), F32)
        acc = jnp.zeros((HEAD_W, 2 * tq), F32)
        for j in range(n_kv):
            rows = slice(j * ATT_TK, (j + 1) * ATT_TK)
            s = jnp.dot(k_ref[0, rows, :], qs_t, preferred_element_type=F32)
            m_new = jnp.maximum(m, jnp.max(s, axis=0, keepdims=True))
            alpha = jnp.exp2(m - m_new)
            p = jnp.exp2(s - m_new)
            l = alpha * l + jnp.sum(p, axis=0, keepdims=True)
            acc = alpha * acc + jnp.dot(vt_ref[:, rows], p.astype(BF16), preferred_element_type=F32)
            m = m_new
        return acc / l

    o_t = lax.cond(bound < BOUND_LIMIT, shifted_by_bound, running_max)
    lam_p = lam_ref[...]
    lam = (jnp.exp(jnp.sum(lam_p[0:1, :] * lam_p[1:2, :], axis=-1, keepdims=True))
           - jnp.exp(jnp.sum(lam_p[2:3, :] * lam_p[3:4, :], axis=-1, keepdims=True)) + LAMBDA_INIT)
    o_t = o_t[:, :tq] - lam * o_t[:, tq:]
    y_t = o_t * lax.rsqrt(jnp.mean(o_t * o_t, axis=0, keepdims=True) + EPS)
    y = y_t.T * normw_ref[0:1, :]
    o_ref[0] = (y * (1.0 - LAMBDA_INIT) * _silu(z_ref[0].astype(F32))).astype(o_ref.dtype)


def _diff_attention(q_t, kn, v_b, z_b, lam_par, norm_w, n_lat):
    n_batch, n_tok, _ = kn.shape
    qt_spec = pl.BlockSpec((1, HEAD_W, ATT_TQ), lambda b, h, i: (b, h, i))
    tok_spec = pl.BlockSpec((1, ATT_TQ, HEAD_W), lambda b, h, i: (b, i, h))
    kv_spec = pl.BlockSpec((1, n_tok, HEAD_W), lambda b, h, i: (b, 0, h))
    par_spec = pl.BlockSpec((8, LANES), lambda b, h, i: (0, 0))
    return pl.pallas_call(
        functools.partial(_diff_attn_kernel, n_tok // ATT_TK),
        grid=(n_batch, HEADS, n_lat // ATT_TQ),
        in_specs=[qt_spec, kv_spec, kv_spec, tok_spec, par_spec, par_spec],
        out_specs=tok_spec,
        out_shape=jax.ShapeDtypeStruct((n_batch, n_lat, HEADS * HEAD_W), BF16),
        scratch_shapes=[pltpu.VMEM((n_tok, 2 * HEAD_W), BF16), pltpu.VMEM((HEAD_W, n_tok), BF16),
                        pltpu.VMEM((8, LANES), F32)],
        compiler_params=pltpu.CompilerParams(
            dimension_semantics=("arbitrary", "arbitrary", "arbitrary"), vmem_limit_bytes=VMEM_LIMIT),
        name="diff_attention",
    )(q_t, kn, v_b, z_b, lam_par, norm_w)


def _mixer_out_kernel(x_ref, of_ref, ob_ref, za_ref, yb_ref, gates_ref, mod_ref, normw_ref,
                      woa_ref, wob_ref, wout_ref, o_ref, ya_ref):
    b = pl.program_id(0)
    for h in range(HEADS):
        cols = slice(h * HEAD_W, (h + 1) * HEAD_W)
        o_a = of_ref[0, :, cols].astype(F32) + ob_ref[0, :, cols].astype(F32)
        y = o_a * lax.rsqrt(jnp.mean(o_a * o_a, axis=-1, keepdims=True) + EPS) * normw_ref[0:1, :]
        ya_ref[:, cols] = (y * _silu(za_ref[0, :, cols].astype(F32))).astype(BF16)
    gates = _sigmoid(gates_ref[0].astype(F32))
    merged = (gates[:, :D_MODEL] * jnp.dot(ya_ref[...], woa_ref[...], preferred_element_type=F32)
              + gates[:, D_MODEL:] * jnp.dot(yb_ref[0], wob_ref[...], preferred_element_type=F32))
    out = jnp.dot(merged.astype(BF16), wout_ref[...], preferred_element_type=F32)
    gate = mod_ref[pl.ds(b, 1), 2 * D_MODEL:3 * D_MODEL]
    o_ref[0] = x_ref[0] + gate * out


def _mixer_output(x, o_f, o_b, z_a, y_b, gates, mod, gdn_norm_w, w_oa, w_ob, w_out):
    n_batch, n_lat, _ = x.shape
    tok_spec = lambda w: pl.BlockSpec((1, ROW_TILE, w), lambda b, i: (b, i, 0))
    w_spec = pl.BlockSpec((D_MODEL, D_MODEL), lambda b, i: (0, 0))
    return pl.pallas_call(
        _mixer_out_kernel,
        grid=(n_batch, n_lat // ROW_TILE),
        in_specs=[tok_spec(D_MODEL), tok_spec(GDN_WIDTH), tok_spec(GDN_WIDTH), tok_spec(GDN_WIDTH),
                  tok_spec(D_MODEL), tok_spec(2 * D_MODEL),
                  pl.BlockSpec((8, 3 * D_MODEL), lambda b, i: (0, 0)),
                  pl.BlockSpec((8, LANES), lambda b, i: (0, 0)),
                  w_spec, w_spec, w_spec],
        out_specs=tok_spec(D_MODEL),
        out_shape=jax.ShapeDtypeStruct(x.shape, F32),
        scratch_shapes=[pltpu.VMEM((ROW_TILE, GDN_WIDTH), BF16)],
        compiler_params=pltpu.CompilerParams(
            dimension_semantics=("arbitrary", "arbitrary"), vmem_limit_bytes=VMEM_LIMIT),
        name="mixer_output",
    )(x, o_f, o_b, z_a, y_b, gates, mod, gdn_norm_w, w_oa, w_ob, w_out)


def _pad_rows(a, rows=8):
    return jnp.pad(a, ((0, rows - a.shape[0]), (0, 0)))


def _pad_lanes(a, lanes=LANES):
    return jnp.pad(a, ((0, 0), (0, lanes - a.shape[1])))


def _rope_tables(n_lat, n_ctx):
    rows = n_lat // GRID_W
    row = jnp.broadcast_to(jnp.arange(rows)[:, None], (rows, GRID_W)).reshape(-1).astype(F32)
    col = jnp.broadcast_to(jnp.arange(GRID_W)[None, :], (rows, GRID_W)).reshape(-1).astype(F32)
    n_freq = DIFF_DQK // 4
    inv_freq = ROPE_THETA ** (-jnp.arange(n_freq, dtype=F32) / n_freq)
    ang_r = row[:, None] * inv_freq
    ang_c = col[:, None] * inv_freq
    ang = jnp.concatenate([ang_r, ang_r, ang_c, ang_c] * 2, axis=-1)
    cos = jnp.concatenate([jnp.cos(ang), jnp.ones((n_ctx, LANES), F32)], axis=0)
    sin = jnp.concatenate([jnp.sin(ang), jnp.zeros((n_ctx, LANES), F32)], axis=0)
    return cos, sin


def kernel(x, c, ctx, c_ctx, w_ada, b_ada, w_in, conv_w, a_log, dt_bias, gdn_norm_w, q_norm_w, k_norm_w,
           lambda_q1, lambda_k1, lambda_q2, lambda_k2, diff_norm_w, w_oa, w_ob, w_out):
    assert w_ada.shape[0] == 1, "single-layer block"
    n_batch, n_lat, _ = x.shape
    n_ctx = ctx.shape[1]
    assert n_ctx == ROW_TILE and n_lat % ROW_TILE == 0 and n_batch < 8

    bounds = np.cumsum((0,) + IN_SIZES)
    piece = lambda j: w_in[0][:, bounds[j]:bounds[j + 1]]
    ab_cols = _pad_lanes(jnp.concatenate([piece(2), piece(3)], axis=1))
    w_packed = jnp.concatenate([piece(0), piece(1), piece(4), piece(5), piece(6), piece(7), piece(8)],
                               axis=1).astype(BF16)
    w_packed = jnp.concatenate([w_packed, ab_cols.astype(BF16)], axis=1)
    cc = _pad_rows(jnp.concatenate([c, c_ctx[None, :]], axis=0))
    conv_w8 = _pad_rows(conv_w[0])
    gpar = _pad_rows(_pad_lanes(jnp.stack([a_log[0].reshape(-1), dt_bias[0].reshape(-1)])))
    gains = _pad_rows(jnp.stack([jnp.tile(q_norm_w[0], 2), jnp.tile(k_norm_w[0], 2)]))
    lam_par = _pad_rows(_pad_lanes(jnp.stack([lambda_q1[0], lambda_k1[0], lambda_q2[0], lambda_k2[0]])))
    cos_tab, sin_tab = _rope_tables(n_lat, n_ctx)

    mod = _ada_modulation(cc, w_ada[0], b_ada)
    qkv_a, z_a, q_b, k_b, v_b, z_b, gates, ab = _input_projection(x, ctx, mod, w_packed)
    q_a, k_a, v_a, gb = _gdn_prepare(qkv_a, ab, conv_w8, gpar, n_lat)
    o_f, o_b = _gdn_scan(q_a, k_a, v_a, gb, n_lat)
    q_t, kn = _qk_prepare(q_b, k_b, cos_tab, sin_tab, gains, n_lat)
    y_b = _diff_attention(q_t, kn, v_b, z_b, lam_par, _pad_rows(diff_norm_w), n_lat)
    return _mixer_output(x, o_f, o_b, z_a, y_b, gates, mod, _pad_rows(gdn_norm_w),
                         w_oa[0].astype(BF16), w_ob[0].astype(BF16), w_out[0].astype(BF16))
```

```python
import functools
import math

import jax
import jax.numpy as jnp
import numpy as np
from jax import lax
from jax.experimental import pallas as pl
from jax.experimental.pallas import tpu as pltpu

F32 = jnp.float32
BF16 = jnp.bfloat16

D_MODEL = 1024
GRID_W = 64
EPS = 1e-6
HEADS = 8
HEAD_W = 128
GDN_WIDTH = HEADS * HEAD_W
CONV_K = 5
CHUNK = 64
DIFF_DQK = 64
ROPE_THETA = 10000.0
LAMBDA_INIT = 0.8 - 0.6 * math.exp(-0.3 * 0)
IN_SIZES = (3 * GDN_WIDTH, GDN_WIDTH, 2 * HEADS, 2 * HEADS, 1024, 1024, 1024, 1024, 2 * D_MODEL)

ROW_TILE = 256
HALO = 16
LANES = 128
VMEM_LIMIT = 56 * 1024 * 1024


def _bdot(a, b):
    return jnp.dot(a.astype(BF16), b.astype(BF16), preferred_element_type=F32)


def _bdot_nt(a, b):
    return lax.dot_general(a.astype(BF16), b.astype(BF16), (((1,), (1,)), ((), ())),
                           preferred_element_type=F32)


def _bdot_tn(a, b):
    return lax.dot_general(a.astype(BF16), b.astype(BF16), (((0,), (0,)), ((), ())),
                           preferred_element_type=F32)


def _silu(x):
    return x * (1.0 / (1.0 + jnp.exp(-x)))


def _sigmoid(x):
    return 1.0 / (1.0 + jnp.exp(-x))


def _split3(x):
    hi = x.astype(BF16)
    r1 = x - hi.astype(F32)
    mid = r1.astype(BF16)
    lo = (r1 - mid.astype(F32)).astype(BF16)
    return hi, mid, lo


def _ada_kernel(c_ref, w_ref, b_ref, o_ref):
    a = _split3(_silu(c_ref[...]))
    w = _split3(w_ref[...])
    acc = b_ref[...]
    for i, j in ((2, 0), (1, 1), (0, 2), (1, 0), (0, 1), (0, 0)):
        acc = acc + jnp.dot(a[i], w[j], preferred_element_type=F32)
    o_ref[...] = acc


def _ada_modulation(cc, w_ada, b_ada):
    n_col = w_ada.shape[1] // D_MODEL
    return pl.pallas_call(
        _ada_kernel,
        grid=(n_col,),
        in_specs=[pl.BlockSpec((8, D_MODEL), lambda j: (0, 0)),
                  pl.BlockSpec((D_MODEL, D_MODEL), lambda j: (0, j)),
                  pl.BlockSpec((1, D_MODEL), lambda j: (0, j))],
        out_specs=pl.BlockSpec((8, D_MODEL), lambda j: (0, j)),
        out_shape=jax.ShapeDtypeStruct((8, w_ada.shape[1]), F32),
        name="ada_modulation",
    )(cc, w_ada, b_ada)


IN_GROUPS = (("qkv_a", 3 * GDN_WIDTH, BF16), ("z_a", GDN_WIDTH, BF16), ("q_b", 1024, BF16),
             ("k_b", 1024, BF16), ("v_b", 1024, BF16), ("z_b", 1024, BF16),
             ("gates", 2 * D_MODEL, BF16), ("ab", LANES, F32))
IN_COL_CHUNK = 1024


LOG2E = 1.4426950408889634


def _qk_norm_rope(x, gain, cos, sin):
    lane = lax.broadcasted_iota(jnp.int32, x.shape, 1)
    first_comp = lane < DIFF_DQK
    first_half = (lane % (DIFF_DQK // 2)) < (DIFF_DQK // 4)
    xx = x * x
    s1 = jnp.sum(jnp.where(first_comp, xx, 0.0), axis=-1, keepdims=True)
    s2 = jnp.sum(jnp.where(first_comp, 0.0, xx), axis=-1, keepdims=True)
    ms = jnp.where(first_comp, s1, s2) * (1.0 / DIFF_DQK)
    t = x * lax.rsqrt(ms + EPS) * gain
    rot = jnp.where(first_half, -pltpu.roll(t, LANES - DIFF_DQK // 4, 1), pltpu.roll(t, DIFF_DQK // 4, 1))
    return t * cos + rot * sin


def _inproj_kernel(n_lat_tiles, n_batch, x_ref, ctx_ref, mod_ref, w_ref, cos_ref, sin_ref, gain_ref,
                   *out_refs):
    b = pl.program_id(0)
    i = pl.program_id(1)
    is_ctx = i == n_lat_tiles
    xt = jnp.where(is_ctx, ctx_ref[0], x_ref[0])
    mod = mod_ref[pl.ds(jnp.where(is_ctx, n_batch, b), 1), :]
    shift = mod[:, 0:D_MODEL]
    scale = mod[:, D_MODEL:2 * D_MODEL]
    ms = jnp.mean(xt * xt, axis=-1, keepdims=True)
    h = (xt * lax.rsqrt(ms + EPS) * (1.0 + scale) + shift).astype(BF16)
    cos = cos_ref[...]
    sin = sin_ref[...]
    off = 0
    for o_ref, (name, width, _) in zip(out_refs, IN_GROUPS):
        for c0 in range(0, width, IN_COL_CHUNK):
            cw = min(IN_COL_CHUNK, width - c0)
            res = jnp.dot(h, w_ref[:, off + c0:off + c0 + cw], preferred_element_type=F32)
            if name == "k_b":
                for hd in range(HEADS):
                    cols = slice(hd * HEAD_W, (hd + 1) * HEAD_W)
                    o_ref[0, :, cols] = _qk_norm_rope(res[:, cols], gain_ref[1:2, :], cos, sin).astype(o_ref.dtype)
            elif name == "q_b":
                for hd in range(HEADS):
                    cols = slice(hd * HEAD_W, (hd + 1) * HEAD_W)
                    y = _qk_norm_rope(res[:, cols], gain_ref[0:1, :], cos, sin) * (DIFF_DQK ** -0.5 * LOG2E)
                    o_ref[0, cols, :] = y.T.astype(o_ref.dtype)
            else:
                o_ref[0, :, c0:c0 + cw] = res.astype(o_ref.dtype)
        off += width


def _input_projection(x, ctx, mod, w_packed, cos_tab, sin_tab, gains):
    n_batch, n_lat, _ = x.shape
    n_lat_tiles = n_lat // ROW_TILE
    n_tok = n_lat + ctx.shape[1]
    total_cols = w_packed.shape[1]
    out_shape, out_specs = [], []
    for name, w, dt in IN_GROUPS:
        if name == "q_b":
            out_shape.append(jax.ShapeDtypeStruct((n_batch, w, n_tok), dt))
            out_specs.append(pl.BlockSpec((1, w, ROW_TILE), lambda b, i: (b, 0, i)))
        else:
            out_shape.append(jax.ShapeDtypeStruct((n_batch, n_tok, w), dt))
            out_specs.append(pl.BlockSpec((1, ROW_TILE, w), lambda b, i: (b, i, 0)))
    tab_spec = pl.BlockSpec((ROW_TILE, LANES), lambda b, i: (i, 0))
    return pl.pallas_call(
        functools.partial(_inproj_kernel, n_lat_tiles, n_batch),
        grid=(n_batch, n_lat_tiles + 1),
        in_specs=[
            pl.BlockSpec((1, ROW_TILE, D_MODEL), lambda b, i: (b, jnp.minimum(i, n_lat_tiles - 1), 0)),
            pl.BlockSpec((1, ROW_TILE, D_MODEL), lambda b, i: (b, 0, 0)),
            pl.BlockSpec((8, 3 * D_MODEL), lambda b, i: (0, 0)),
            pl.BlockSpec((D_MODEL, total_cols), lambda b, i: (0, 0), pipeline_mode=pl.Buffered(1)),
            tab_spec, tab_spec, pl.BlockSpec((8, LANES), lambda b, i: (0, 0)),
        ],
        out_specs=out_specs,
        out_shape=out_shape,
        compiler_params=pltpu.CompilerParams(
            dimension_semantics=("arbitrary", "arbitrary"), vmem_limit_bytes=VMEM_LIMIT),
        name="input_projection",
    )(x, ctx, mod, w_packed, cos_tab, sin_tab, gains)


def _gdn_prep_kernel(n_lat_tiles, cur_ref, prev_ref, next_ref, convw_ref, ab_ref, gpar_ref,
                     q_ref, k_ref, v_ref, gb_ref, xp_ref):
    i = pl.program_id(1)
    zero_prev = (i == 0) | (i == n_lat_tiles)
    zero_next = i >= n_lat_tiles - 1
    xp_ref[0:HALO, :] = jnp.where(zero_prev, 0.0, prev_ref[0].astype(F32))
    xp_ref[HALO:HALO + ROW_TILE, :] = cur_ref[0].astype(F32)
    xp_ref[HALO + ROW_TILE:, :] = jnp.where(zero_next, 0.0, next_ref[0].astype(F32))
    pad = CONV_K // 2
    outs = (q_ref, k_ref, v_ref)
    for g in range(3 * HEADS):
        cols = slice(g * HEAD_W, (g + 1) * HEAD_W)
        y = None
        for j in range(CONV_K):
            term = xp_ref[HALO - pad + j:HALO - pad + j + ROW_TILE, cols] * convw_ref[j:j + 1, cols]
            y = term if y is None else y + term
        y = _silu(y)
        which, head = divmod(g, HEADS)
        if which < 2:
            y = y * lax.rsqrt(jnp.sum(y * y, axis=-1, keepdims=True) + EPS)
        if which == 0:
            y = y * (HEAD_W ** -0.5)
        outs[which][0, :, head * HEAD_W:(head + 1) * HEAD_W] = y.astype(BF16)
    ab = ab_ref[0]
    lane = lax.broadcasted_iota(jnp.int32, ab.shape, 1)
    a_log = gpar_ref[0:1, :]
    dt_bias = gpar_ref[1:2, :]
    xs = ab + dt_bias
    softplus = jnp.maximum(xs, 0.0) + jnp.log1p(jnp.exp(-jnp.abs(xs)))
    g_all = -jnp.exp(a_log) * softplus
    gb_ref[0] = jnp.where(lane < 2 * HEADS, g_all, jnp.where(lane < 4 * HEADS, _sigmoid(ab), 0.0))


def _gdn_prepare(qkv_a, ab, conv_w8, gpar, n_lat):
    n_batch, n_tok, width = qkv_a.shape
    n_lat_tiles = n_lat // ROW_TILE
    n_tiles = n_tok // ROW_TILE
    halo_per_tile = ROW_TILE // HALO
    n_halo = n_tok // HALO
    tok_spec = lambda w: pl.BlockSpec((1, ROW_TILE, w), lambda b, i: (b, i, 0))
    return pl.pallas_call(
        functools.partial(_gdn_prep_kernel, n_lat_tiles),
        grid=(n_batch, n_tiles),
        in_specs=[
            tok_spec(width),
            pl.BlockSpec((1, HALO, width), lambda b, i: (b, jnp.maximum(i * halo_per_tile - 1, 0), 0)),
            pl.BlockSpec((1, HALO, width),
                         lambda b, i: (b, jnp.minimum((i + 1) * halo_per_tile, n_halo - 1), 0)),
            pl.BlockSpec((8, width), lambda b, i: (0, 0)),
            tok_spec(LANES),
            pl.BlockSpec((8, LANES), lambda b, i: (0, 0)),
        ],
        out_specs=[tok_spec(GDN_WIDTH), tok_spec(GDN_WIDTH), tok_spec(GDN_WIDTH), tok_spec(LANES)],
        out_shape=[jax.ShapeDtypeStruct((n_batch, n_tok, GDN_WIDTH), BF16)] * 3
        + [jax.ShapeDtypeStruct((n_batch, n_tok, LANES), F32)],
        scratch_shapes=[pltpu.VMEM((ROW_TILE + 2 * HALO, width), F32)],
        compiler_params=pltpu.CompilerParams(
            dimension_semantics=("arbitrary", "arbitrary"), vmem_limit_bytes=VMEM_LIMIT),
        name="gdn_prepare",
    )(qkv_a, qkv_a, qkv_a, conv_w8, ab, gpar)


GROUP = 4
GSIZE = GROUP * CHUNK
N_LEVELS = 5
M_EYE, M_PAIR, M_LEVEL0 = 0, 1, 2
M_INCL = (M_LEVEL0 + N_LEVELS, M_LEVEL0 + N_LEVELS + 2)
M_STRICT = (M_LEVEL0 + N_LEVELS + 1, M_LEVEL0 + N_LEVELS + 3)


def _scan_masks():
    r = np.arange(GSIZE)[:, None]
    c = np.arange(GSIZE)[None, :]
    blk = (r // CHUNK) == (c // CHUNK)
    planes = [r == c, (r >> 1) == (c >> 1)]
    for shift in range(1, N_LEVELS + 1):
        planes.append(((r >> (shift + 1)) == (c >> (shift + 1))) & ((r >> shift) != (c >> shift)))
    planes += [blk & (r >= c), blk & (r > c), blk & (r <= c), blk & (r < c)]
    return np.stack(planes).astype(np.float32)


def _gdn_scan_kernel(qf_ref, kf_ref, vf_ref, gbf_ref, qb_ref, kb_ref, vb_ref, gbb_ref, mask_ref,
                     of_ref, ob_ref, s_ref):
    @pl.when(pl.program_id(1) == 0)
    def _():
        s_ref[...] = jnp.zeros_like(s_ref)

    rows_of = lambda c: slice(c * CHUNK, (c + 1) * CHUNK)
    stack_rows = lambda parts: jnp.concatenate(parts, axis=0)

    groups = []
    for d, (q_ref, k_ref, v_ref, gb_ref, o_ref) in enumerate(
            ((qf_ref, kf_ref, vf_ref, gbf_ref, of_ref), (qb_ref, kb_ref, vb_ref, gbb_ref, ob_ref))):
        last = CHUNK - 1 if d == 0 else 0
        gb = gb_ref[0]
        tri = mask_ref[M_INCL[d], 0:CHUNK, 0:CHUNK].astype(BF16)
        gc = None
        for part in _split3(gb):
            term = jnp.dot(tri, part, preferred_element_type=F32)
            gc = term if gc is None else gc + term
        gc_t = gc.T
        for half in range(HEADS // GROUP):
            heads = [half * GROUP + c for c in range(GROUP)]
            lanes = [d * HEADS + h for h in heads]
            head_cols = [slice(h * HEAD_W, (h + 1) * HEAD_W) for h in heads]
            groups.append(dict(
                d=d, lanes=lanes, head_cols=head_cols, o_ref=o_ref,
                q=stack_rows([q_ref[0, :, cs].astype(F32) for cs in head_cols]),
                k=stack_rows([k_ref[0, :, cs].astype(F32) for cs in head_cols]),
                v=stack_rows([v_ref[0, :, cs].astype(F32) for cs in head_cols]),
                g_col=stack_rows([gc[:, l:l + 1] for l in lanes]),
                g_row=jnp.concatenate([gc_t[l:l + 1, :] for l in lanes], axis=1),
                g_last=stack_rows([jnp.broadcast_to(gc[last:last + 1, l:l + 1], (CHUNK, 1)) for l in lanes]),
                beta=stack_rows([gb[:, 2 * HEADS + l:2 * HEADS + l + 1] for l in lanes])))

    for g in groups:
        incl = mask_ref[M_INCL[g["d"]]]
        decay = jnp.exp((g["g_col"] - g["g_row"]) * incl) * incl
        kq = _bdot_nt(stack_rows([g["k"] * g["beta"], g["q"]]), g["k"])
        g["a"] = kq[:GSIZE] * decay * mask_ref[M_STRICT[g["d"]]]
        g["qk"] = kq[GSIZE:] * decay
        g["t"] = mask_ref[M_EYE] - g["a"] * mask_ref[M_PAIR]
    for level in range(N_LEVELS):
        for g in groups:
            g["x"] = _bdot(g["a"] * mask_ref[M_LEVEL0 + level], g["t"])
        for g in groups:
            g["t"] = g["t"] - _bdot(g["t"], g["x"])
    for g in groups:
        e_g = jnp.exp(g["g_col"])
        uw = _bdot(g["t"], jnp.concatenate([g["v"] * g["beta"], g["k"] * (g["beta"] * e_g)], axis=1))
        g["u"] = uw[:, :HEAD_W]
        g["w"] = uw[:, HEAD_W:]
        g["qd"] = g["q"] * e_g
        g["k_dec"] = g["k"] * jnp.exp(g["g_last"] - g["g_col"])

    for g in groups:
        v_new, o_inter = [], []
        for c, lane in enumerate(g["lanes"]):
            s = s_ref[lane]
            ws_qs = _bdot(stack_rows([g["w"][rows_of(c)], g["qd"][rows_of(c)]]), s)
            v_new.append(g["u"][rows_of(c)] - ws_qs[:CHUNK])
            o_inter.append(ws_qs[CHUNK:])
            s_ref[lane] = (s * jnp.exp(g["g_last"][c * CHUNK:c * CHUNK + 1, :])
                           + _bdot_tn(g["k_dec"][rows_of(c)], v_new[-1]))
        o = stack_rows(o_inter) + _bdot(g["qk"], stack_rows(v_new))
        for c, cs in enumerate(g["head_cols"]):
            g["o_ref"][0, :, cs] = o[rows_of(c)].astype(g["o_ref"].dtype)


def _gdn_scan(q, k, v, gb, n_lat):
    n_batch, n_tok, _ = q.shape
    n_lat_chunks = n_lat // CHUNK
    n_ctx_chunks = (n_tok - n_lat) // CHUNK
    n_steps = n_tok // CHUNK

    def fwd_chunk(s):
        return jnp.where(s < n_ctx_chunks, n_lat_chunks + s, s - n_ctx_chunks)

    def bwd_chunk(s):
        return jnp.where(s < n_ctx_chunks, n_lat_chunks + n_ctx_chunks - 1 - s,
                         n_lat_chunks - 1 - (s - n_ctx_chunks))

    f_spec = lambda w: pl.BlockSpec((1, CHUNK, w), lambda b, s: (b, fwd_chunk(s), 0))
    b_spec = lambda w: pl.BlockSpec((1, CHUNK, w), lambda b, s: (b, bwd_chunk(s), 0))
    masks = jnp.asarray(_scan_masks())
    return pl.pallas_call(
        _gdn_scan_kernel,
        grid=(n_batch, n_steps),
        in_specs=[f_spec(GDN_WIDTH)] * 3 + [f_spec(LANES)] + [b_spec(GDN_WIDTH)] * 3 + [b_spec(LANES)]
        + [pl.BlockSpec(masks.shape, lambda b, s: (0, 0, 0))],
        out_specs=[f_spec(GDN_WIDTH), b_spec(GDN_WIDTH)],
        out_shape=[jax.ShapeDtypeStruct((n_batch, n_tok, GDN_WIDTH), BF16)] * 2,
        scratch_shapes=[pltpu.VMEM((2 * HEADS, HEAD_W, HEAD_W), F32)],
        compiler_params=pltpu.CompilerParams(
            dimension_semantics=("arbitrary", "arbitrary"), vmem_limit_bytes=VMEM_LIMIT),
        name="gdn_scan",
    )(q, k, v, gb, q, k, v, gb, masks)


ATT_TQ = 512
ATT_TK = 768


BOUND_LIMIT = 50.0


def _diff_attn_kernel(n_kv, q_ref, k_ref, v_ref, z_ref, lam_ref, normw_ref, o_ref, kx_ref, vt_ref, kmax_ref):
    tq = q_ref.shape[2]

    @pl.when(pl.program_id(2) == 0)
    def _():
        lane_k = lax.broadcasted_iota(jnp.int32, (ATT_TK, HEAD_W), 1)
        unit = jnp.where(lane_k == 0, 1.0, 0.0).astype(BF16)

        def widen(j, carry):
            n1, n2 = carry
            rows = pl.ds(pl.multiple_of(j * ATT_TK, ATT_TK), ATT_TK)
            kb = k_ref[0, rows, :]
            kx_ref[rows, 0:HEAD_W] = kb
            kx_ref[rows, HEAD_W:2 * HEAD_W] = unit
            vt_ref[:, rows] = v_ref[0, rows, :].astype(F32).T.astype(BF16)
            sq = kb.astype(F32) * kb.astype(F32)
            a1 = jnp.sum(jnp.where(lane_k < DIFF_DQK, sq, 0.0), axis=1, keepdims=True)
            a2 = jnp.sum(jnp.where(lane_k < DIFF_DQK, 0.0, sq), axis=1, keepdims=True)
            return (jnp.maximum(n1, jnp.max(a1, axis=0, keepdims=True)),
                    jnp.maximum(n2, jnp.max(a2, axis=0, keepdims=True)))

        zero11 = jnp.zeros((1, 1), F32)
        n1, n2 = lax.fori_loop(0, n_kv, widen, (zero11, zero11))
        kmax_ref[0:1, :] = jnp.broadcast_to(jnp.sqrt(n1), (1, LANES))
        kmax_ref[1:2, :] = jnp.broadcast_to(jnp.sqrt(n2), (1, LANES))

    qt = q_ref[0]
    feat = lax.broadcasted_iota(jnp.int32, (HEAD_W, tq), 0)
    first = feat < DIFF_DQK
    zero = jnp.zeros_like(qt)
    qs_t = jnp.concatenate([jnp.where(first, qt, zero), jnp.where(first, zero, qt)], axis=1)
    sq = qt.astype(F32) * qt.astype(F32)
    c1 = jnp.sqrt(jnp.sum(jnp.where(first, sq, 0.0), axis=0, keepdims=True)) * kmax_ref[0:1, 0:1]
    c2 = jnp.sqrt(jnp.sum(jnp.where(first, 0.0, sq), axis=0, keepdims=True)) * kmax_ref[1:2, 0:1]
    c = jnp.concatenate([c1, c2], axis=1)
    bound = jnp.max(c)

    def shifted_by_bound():
        feat2 = lax.broadcasted_iota(jnp.int32, (HEAD_W, 2 * tq), 0)
        aug = jnp.where(feat2 == 0, -c, 0.0).astype(BF16)
        qx_t = jnp.concatenate([qs_t, aug], axis=0)
        l = jnp.zeros((1, 2 * tq), F32)
        acc = jnp.zeros((HEAD_W, 2 * tq), F32)
        for j in range(n_kv):
            rows = slice(j * ATT_TK, (j + 1) * ATT_TK)
            p = jnp.exp2(jnp.dot(kx_ref[rows, :], qx_t, preferred_element_type=F32))
            l = l + jnp.sum(p, axis=0, keepdims=True)
            acc = acc + jnp.dot(vt_ref[:, rows], p.astype(BF16), preferred_element_type=F32)
        return acc / l

    def running_max():
        m = jnp.full((1, 2 * tq), -jnp.inf, F32)
        l = jnp.zeros((1, 2 * tq), F32)
        acc = jnp.zeros((HEAD_W, 2 * tq), F32)
        for j in range(n_kv):
            rows = slice(j * ATT_TK, (j + 1) * ATT_TK)
            s = jnp.dot(k_ref[0, rows, :], qs_t, preferred_element_type=F32)
            m_new = jnp.maximum(m, jnp.max(s, axis=0, keepdims=True))
            alpha = jnp.exp2(m - m_new)
            p = jnp.exp2(s - m_new)
            l = alpha * l + jnp.sum(p, axis=0, keepdims=True)
            acc = alpha * acc + jnp.dot(vt_ref[:, rows], p.astype(BF16), preferred_element_type=F32)
            m = m_new
        return acc / l

    o_t = lax.cond(bound < BOUND_LIMIT, shifted_by_bound, running_max)
    lam_p = lam_ref[...]
    lam = (jnp.exp(jnp.sum(lam_p[0:1, :] * lam_p[1:2, :], axis=-1, keepdims=True))
           - jnp.exp(jnp.sum(lam_p[2:3, :] * lam_p[3:4, :], axis=-1, keepdims=True)) + LAMBDA_INIT)
    o_t = o_t[:, :tq] - lam * o_t[:, tq:]
    y_t = o_t * lax.rsqrt(jnp.mean(o_t * o_t, axis=0, keepdims=True) + EPS)
    y = y_t.T * normw_ref[0:1, :]
    o_ref[0] = (y * (1.0 - LAMBDA_INIT) * _silu(z_ref[0].astype(F32))).astype(o_ref.dtype)


def _diff_attention(q_t, kn, v_b, z_b, lam_par, norm_w, n_lat):
    n_batch, n_tok, _ = kn.shape
    qt_spec = pl.BlockSpec((1, HEAD_W, ATT_TQ), lambda b, h, i: (b, h, i))
    tok_spec = pl.BlockSpec((1, ATT_TQ, HEAD_W), lambda b, h, i: (b, i, h))
    kv_spec = pl.BlockSpec((1, n_tok, HEAD_W), lambda b, h, i: (b, 0, h))
    par_spec = pl.BlockSpec((8, LANES), lambda b, h, i: (0, 0))
    return pl.pallas_call(
        functools.partial(_diff_attn_kernel, n_tok // ATT_TK),
        grid=(n_batch, HEADS, n_lat // ATT_TQ),
        in_specs=[qt_spec, kv_spec, kv_spec, tok_spec, par_spec, par_spec],
        out_specs=tok_spec,
        out_shape=jax.ShapeDtypeStruct((n_batch, n_lat, HEADS * HEAD_W), BF16),
        scratch_shapes=[pltpu.VMEM((n_tok, 2 * HEAD_W), BF16), pltpu.VMEM((HEAD_W, n_tok), BF16),
                        pltpu.VMEM((8, LANES), F32)],
        compiler_params=pltpu.CompilerParams(
            dimension_semantics=("arbitrary", "arbitrary", "arbitrary"), vmem_limit_bytes=VMEM_LIMIT),
        name="diff_attention",
    )(q_t, kn, v_b, z_b, lam_par, norm_w)


def _mixer_out_kernel(x_ref, of_ref, ob_ref, za_ref, yb_ref, gates_ref, mod_ref, normw_ref,
                      woa_ref, wob_ref, wout_ref, o_ref, ya_ref):
    b = pl.program_id(0)
    for h in range(HEADS):
        cols = slice(h * HEAD_W, (h + 1) * HEAD_W)
        o_a = of_ref[0, :, cols].astype(F32) + ob_ref[0, :, cols].astype(F32)
        y = o_a * lax.rsqrt(jnp.mean(o_a * o_a, axis=-1, keepdims=True) + EPS) * normw_ref[0:1, :]
        ya_ref[:, cols] = (y * _silu(za_ref[0, :, cols].astype(F32))).astype(BF16)
    gates = _sigmoid(gates_ref[0].astype(F32))
    merged = (gates[:, :D_MODEL] * jnp.dot(ya_ref[...], woa_ref[...], preferred_element_type=F32)
              + gates[:, D_MODEL:] * jnp.dot(yb_ref[0], wob_ref[...], preferred_element_type=F32))
    out = jnp.dot(merged.astype(BF16), wout_ref[...], preferred_element_type=F32)
    gate = mod_ref[pl.ds(b, 1), 2 * D_MODEL:3 * D_MODEL]
    o_ref[0] = x_ref[0] + gate * out


def _mixer_output(x, o_f, o_b, z_a, y_b, gates, mod, gdn_norm_w, w_oa, w_ob, w_out):
    n_batch, n_lat, _ = x.shape
    tok_spec = lambda w: pl.BlockSpec((1, ROW_TILE, w), lambda b, i: (b, i, 0))
    w_spec = pl.BlockSpec((D_MODEL, D_MODEL), lambda b, i: (0, 0))
    return pl.pallas_call(
        _mixer_out_kernel,
        grid=(n_batch, n_lat // ROW_TILE),
        in_specs=[tok_spec(D_MODEL), tok_spec(GDN_WIDTH), tok_spec(GDN_WIDTH), tok_spec(GDN_WIDTH),
                  tok_spec(D_MODEL), tok_spec(2 * D_MODEL),
                  pl.BlockSpec((8, 3 * D_MODEL), lambda b, i: (0, 0)),
                  pl.BlockSpec((8, LANES), lambda b, i: (0, 0)),
                  w_spec, w_spec, w_spec],
        out_specs=tok_spec(D_MODEL),
        out_shape=jax.ShapeDtypeStruct(x.shape, F32),
        scratch_shapes=[pltpu.VMEM((ROW_TILE, GDN_WIDTH), BF16)],
        compiler_params=pltpu.CompilerParams(
            dimension_semantics=("arbitrary", "arbitrary"), vmem_limit_bytes=VMEM_LIMIT),
        name="mixer_output",
    )(x, o_f, o_b, z_a, y_b, gates, mod, gdn_norm_w, w_oa, w_ob, w_out)


def _pad_rows(a, rows=8):
    return jnp.pad(a, ((0, rows - a.shape[0]), (0, 0)))


def _pad_lanes(a, lanes=LANES):
    return jnp.pad(a, ((0, 0), (0, lanes - a.shape[1])))


def _rope_tables(n_lat, n_ctx):
    rows = n_lat // GRID_W
    row = jnp.broadcast_to(jnp.arange(rows)[:, None], (rows, GRID_W)).reshape(-1).astype(F32)
    col = jnp.broadcast_to(jnp.arange(GRID_W)[None, :], (rows, GRID_W)).reshape(-1).astype(F32)
    n_freq = DIFF_DQK // 4
    inv_freq = ROPE_THETA ** (-jnp.arange(n_freq, dtype=F32) / n_freq)
    ang_r = row[:, None] * inv_freq
    ang_c = col[:, None] * inv_freq
    ang = jnp.concatenate([ang_r, ang_r, ang_c, ang_c] * 2, axis=-1)
    cos = jnp.concatenate([jnp.cos(ang), jnp.ones((n_ctx, LANES), F32)], axis=0)
    sin = jnp.concatenate([jnp.sin(ang), jnp.zeros((n_ctx, LANES), F32)], axis=0)
    return cos, sin


def kernel(x, c, ctx, c_ctx, w_ada, b_ada, w_in, conv_w, a_log, dt_bias, gdn_norm_w, q_norm_w, k_norm_w,
           lambda_q1, lambda_k1, lambda_q2, lambda_k2, diff_norm_w, w_oa, w_ob, w_out):
    assert w_ada.shape[0] == 1, "single-layer block"
    n_batch, n_lat, _ = x.shape
    n_ctx = ctx.shape[1]
    assert n_ctx == ROW_TILE and n_lat % ROW_TILE == 0 and n_batch < 8

    bounds = np.cumsum((0,) + IN_SIZES)
    w_in_bf = w_in[0].astype(BF16)
    piece = lambda j: w_in_bf[:, bounds[j]:bounds[j + 1]]
    w_packed = jnp.concatenate([piece(0), piece(1), piece(4), piece(5), piece(6), piece(7), piece(8),
                                _pad_lanes(jnp.concatenate([piece(2), piece(3)], axis=1))], axis=1)
    cc = _pad_rows(jnp.concatenate([c, c_ctx[None, :]], axis=0))
    conv_w8 = _pad_rows(conv_w[0])
    gpar = _pad_rows(_pad_lanes(jnp.stack([a_log[0].reshape(-1), dt_bias[0].reshape(-1)])))
    gains = _pad_rows(jnp.stack([jnp.tile(q_norm_w[0], 2), jnp.tile(k_norm_w[0], 2)]))
    lam_par = _pad_rows(_pad_lanes(jnp.stack([lambda_q1[0], lambda_k1[0], lambda_q2[0], lambda_k2[0]])))
    cos_tab, sin_tab = _rope_tables(n_lat, n_ctx)

    mod = _ada_modulation(cc, w_ada[0], b_ada)
    qkv_a, z_a, q_t, kn, v_b, z_b, gates, ab = _input_projection(x, ctx, mod, w_packed, cos_tab, sin_tab, gains)
    q_a, k_a, v_a, gb = _gdn_prepare(qkv_a, ab, conv_w8, gpar, n_lat)
    o_f, o_b = _gdn_scan(q_a, k_a, v_a, gb, n_lat)
    y_b = _diff_attention(q_t, kn, v_b, z_b, lam_par, _pad_rows(diff_norm_w), n_lat)
    return _mixer_output(x, o_f, o_b, z_a, y_b, gates, mod, _pad_rows(gdn_norm_w),
                         w_oa[0].astype(BF16), w_ob[0].astype(BF16), w_out[0].astype(BF16))
```

```python
import functools
import math

import jax
import jax.numpy as jnp
import numpy as np
from jax import lax
from jax.experimental import pallas as pl
from jax.experimental.pallas import tpu as pltpu

F32 = jnp.float32
BF16 = jnp.bfloat16

D_MODEL = 1024
GRID_W = 64
EPS = 1e-6
HEADS = 8
HEAD_W = 128
GDN_WIDTH = HEADS * HEAD_W
CONV_K = 5
CHUNK = 64
DIFF_DQK = 64
ROPE_THETA = 10000.0
LAMBDA_INIT = 0.8 - 0.6 * math.exp(-0.3 * 0)
IN_SIZES = (3 * GDN_WIDTH, GDN_WIDTH, 2 * HEADS, 2 * HEADS, 1024, 1024, 1024, 1024, 2 * D_MODEL)

ROW_TILE = 256
HALO = 16
LANES = 128
VMEM_LIMIT = 56 * 1024 * 1024


def _bdot(a, b):
    return jnp.dot(a.astype(BF16), b.astype(BF16), preferred_element_type=F32)


def _bdot_nt(a, b):
    return lax.dot_general(a.astype(BF16), b.astype(BF16), (((1,), (1,)), ((), ())),
                           preferred_element_type=F32)


def _bdot_tn(a, b):
    return lax.dot_general(a.astype(BF16), b.astype(BF16), (((0,), (0,)), ((), ())),
                           preferred_element_type=F32)


def _silu(x):
    return x * (1.0 / (1.0 + jnp.exp(-x)))


def _sigmoid(x):
    return 1.0 / (1.0 + jnp.exp(-x))


def _split3(x):
    hi = x.astype(BF16)
    r1 = x - hi.astype(F32)
    mid = r1.astype(BF16)
    lo = (r1 - mid.astype(F32)).astype(BF16)
    return hi, mid, lo


def _ada_kernel(c_ref, w_ref, b_ref, o_ref):
    a = _split3(_silu(c_ref[...]))
    w = _split3(w_ref[...])
    acc = b_ref[...]
    for i, j in ((2, 0), (1, 1), (0, 2), (1, 0), (0, 1), (0, 0)):
        acc = acc + jnp.dot(a[i], w[j], preferred_element_type=F32)
    o_ref[...] = acc


def _ada_modulation(cc, w_ada, b_ada):
    n_col = w_ada.shape[1] // D_MODEL
    return pl.pallas_call(
        _ada_kernel,
        grid=(n_col,),
        in_specs=[pl.BlockSpec((8, D_MODEL), lambda j: (0, 0)),
                  pl.BlockSpec((D_MODEL, D_MODEL), lambda j: (0, j)),
                  pl.BlockSpec((1, D_MODEL), lambda j: (0, j))],
        out_specs=pl.BlockSpec((8, D_MODEL), lambda j: (0, j)),
        out_shape=jax.ShapeDtypeStruct((8, w_ada.shape[1]), F32),
        name="ada_modulation",
    )(cc, w_ada, b_ada)


LOG2E = 1.4426950408889634


def _qk_norm_rope(x, gain, cos, sin):
    lane = lax.broadcasted_iota(jnp.int32, x.shape, 1)
    first_comp = lane < DIFF_DQK
    first_half = (lane % (DIFF_DQK // 2)) < (DIFF_DQK // 4)
    xx = x * x
    s1 = jnp.sum(jnp.where(first_comp, xx, 0.0), axis=-1, keepdims=True)
    s2 = jnp.sum(jnp.where(first_comp, 0.0, xx), axis=-1, keepdims=True)
    ms = jnp.where(first_comp, s1, s2) * (1.0 / DIFF_DQK)
    t = x * lax.rsqrt(ms + EPS) * gain
    rot = jnp.where(first_half, -pltpu.roll(t, LANES - DIFF_DQK // 4, 1), pltpu.roll(t, DIFF_DQK // 4, 1))
    return t * cos + rot * sin


def _inproj_kernel(n_lat_tiles, n_batch, x_ref, ctx_ref, mod_ref, wl_ref, wr_ref, wab_ref, cos_ref, sin_ref,
                   gain_ref, convw_ref, gpar_ref,
                   qa_ref, ka_ref, va_ref, gb_ref, za_ref, qt_ref, kn_ref, vb_ref, zb_ref, gates_ref, xp_ref):
    b = pl.program_id(0)
    i = pl.program_id(1)

    @pl.when(i == 0)
    def _():
        xp_ref[...] = jnp.zeros_like(xp_ref)

    is_ctx = i >= n_lat_tiles
    xt = jnp.where(is_ctx, ctx_ref[0], x_ref[0])
    mod = mod_ref[pl.ds(jnp.where(is_ctx, n_batch, b), 1), :]
    shift = mod[:, 0:D_MODEL]
    scale = mod[:, D_MODEL:2 * D_MODEL]
    ms = jnp.mean(xt * xt, axis=-1, keepdims=True)
    h = (xt * lax.rsqrt(ms + EPS) * (1.0 + scale) + shift).astype(BF16)
    proj = lambda w_ref, c0, cw: jnp.dot(h, w_ref[:, c0:c0 + cw], preferred_element_type=F32)
    head_cols = lambda hd: slice(hd * HEAD_W, (hd + 1) * HEAD_W)

    cur_ends_seq = (i == n_lat_tiles) | (i == n_lat_tiles + 1)
    next_starts_seq = i == n_lat_tiles
    pad = CONV_K // 2
    for which, o_ref in enumerate((qa_ref, ka_ref, va_ref)):
        res = proj(wl_ref, which * GDN_WIDTH, GDN_WIDTH)
        for hd in range(HEADS):
            cols = slice(which * GDN_WIDTH + hd * HEAD_W, which * GDN_WIDTH + (hd + 1) * HEAD_W)
            xp_ref[HALO + ROW_TILE:, cols] = jnp.where(cur_ends_seq, 0.0, res[0:HALO, head_cols(hd)])
            slab = xp_ref[:, cols].reshape(-1, 8, HEAD_W)
            sub = lax.broadcasted_iota(jnp.int32, slab.shape, 1)
            y = None
            for j in range(CONV_K):
                s = pad - j
                if s == 0:
                    shifted = slab
                else:
                    rot = pltpu.roll(slab, s % 8, 1)
                    if s > 0:
                        shifted = jnp.where(sub >= s, rot, jnp.concatenate([rot[-1:], rot[:-1]], axis=0))
                    else:
                        shifted = jnp.where(sub < 8 + s, rot, jnp.concatenate([rot[1:], rot[:1]], axis=0))
                body = shifted[HALO // 8:(HALO + ROW_TILE) // 8].reshape(ROW_TILE, HEAD_W)
                term = body * convw_ref[j:j + 1, cols]
                y = term if y is None else y + term
            y = _silu(y)
            if which < 2:
                y = y * lax.rsqrt(jnp.sum(y * y, axis=-1, keepdims=True) + EPS)
            if which == 0:
                y = y * (HEAD_W ** -0.5)
            o_ref[0, :, head_cols(hd)] = y.astype(o_ref.dtype)
            xp_ref[0:HALO, cols] = jnp.where(next_starts_seq, 0.0, xp_ref[ROW_TILE:ROW_TILE + HALO, cols])
            xp_ref[HALO:HALO + ROW_TILE, cols] = res[:, head_cols(hd)]
    za_ref[0] = proj(wl_ref, 3 * GDN_WIDTH, GDN_WIDTH).astype(za_ref.dtype)

    ab = proj(wab_ref, 0, LANES)
    lane = lax.broadcasted_iota(jnp.int32, ab.shape, 1)
    xs = ab + gpar_ref[1:2, :]
    softplus = jnp.maximum(xs, 0.0) + jnp.log1p(jnp.exp(-jnp.abs(xs)))
    g_all = -jnp.exp(gpar_ref[0:1, :]) * softplus
    gb_ref[0] = jnp.where(lane < 2 * HEADS, g_all, jnp.where(lane < 4 * HEADS, _sigmoid(ab), 0.0))

    cos = cos_ref[...]
    sin = sin_ref[...]
    res = proj(wr_ref, 0, D_MODEL)
    for hd in range(HEADS):
        y = _qk_norm_rope(res[:, head_cols(hd)], gain_ref[0:1, :], cos, sin) * (DIFF_DQK ** -0.5 * LOG2E)
        qt_ref[0, head_cols(hd), :] = y.T.astype(qt_ref.dtype)
    res = proj(wr_ref, D_MODEL, D_MODEL)
    for hd in range(HEADS):
        kn_ref[0, :, head_cols(hd)] = _qk_norm_rope(
            res[:, head_cols(hd)], gain_ref[1:2, :], cos, sin).astype(kn_ref.dtype)
    vb_ref[0] = proj(wr_ref, 2 * D_MODEL, D_MODEL).astype(vb_ref.dtype)
    zb_ref[0] = proj(wr_ref, 3 * D_MODEL, D_MODEL).astype(zb_ref.dtype)
    for c0 in range(0, 2 * D_MODEL, D_MODEL):
        gates_ref[0, :, c0:c0 + D_MODEL] = proj(wr_ref, 4 * D_MODEL + c0, D_MODEL).astype(gates_ref.dtype)


def _input_projection(x, ctx, mod, w_left, w_right, w_ab, cos_tab, sin_tab, gains, conv_w8, gpar):
    n_batch, n_lat, _ = x.shape
    n_lat_tiles = n_lat // ROW_TILE
    n_tiles = n_lat_tiles + 1
    n_tok = n_lat + ctx.shape[1]
    last = n_tiles - 1
    now = lambda b, i: (b, jnp.minimum(i, last), 0)
    trailing = lambda b, i: (b, jnp.maximum(i - 1, 0), 0)
    tok = lambda w, dt: jax.ShapeDtypeStruct((n_batch, n_tok, w), dt)
    out_shape = [tok(GDN_WIDTH, BF16)] * 3 + [tok(LANES, F32), tok(GDN_WIDTH, BF16),
                                              jax.ShapeDtypeStruct((n_batch, D_MODEL, n_tok), BF16),
                                              tok(D_MODEL, BF16), tok(D_MODEL, BF16), tok(D_MODEL, BF16),
                                              tok(2 * D_MODEL, BF16)]
    out_specs = ([pl.BlockSpec((1, ROW_TILE, GDN_WIDTH), trailing)] * 3
                 + [pl.BlockSpec((1, ROW_TILE, LANES), now), pl.BlockSpec((1, ROW_TILE, GDN_WIDTH), now),
                    pl.BlockSpec((1, D_MODEL, ROW_TILE), lambda b, i: (b, 0, jnp.minimum(i, last)))]
                 + [pl.BlockSpec((1, ROW_TILE, D_MODEL), now)] * 3
                 + [pl.BlockSpec((1, ROW_TILE, 2 * D_MODEL), now)])
    resident = lambda a: pl.BlockSpec(a.shape, lambda b, i: (0, 0), pipeline_mode=pl.Buffered(1))
    tab_spec = pl.BlockSpec((ROW_TILE, LANES), lambda b, i: (jnp.minimum(i, last), 0))
    small = lambda a: pl.BlockSpec(a.shape, lambda b, i: (0, 0))
    return pl.pallas_call(
        functools.partial(_inproj_kernel, n_lat_tiles, n_batch),
        grid=(n_batch, n_tiles + 1),
        in_specs=[
            pl.BlockSpec((1, ROW_TILE, D_MODEL), lambda b, i: (b, jnp.minimum(i, n_lat_tiles - 1), 0)),
            pl.BlockSpec((1, ROW_TILE, D_MODEL), lambda b, i: (b, 0, 0)),
            small(mod), resident(w_left), resident(w_right), resident(w_ab),
            tab_spec, tab_spec, small(gains), small(conv_w8), small(gpar),
        ],
        out_specs=out_specs,
        out_shape=out_shape,
        scratch_shapes=[pltpu.VMEM((ROW_TILE + 2 * HALO, 3 * GDN_WIDTH), F32)],
        compiler_params=pltpu.CompilerParams(
            dimension_semantics=("arbitrary", "arbitrary"), vmem_limit_bytes=VMEM_LIMIT),
        name="input_projection",
    )(x, ctx, mod, w_left, w_right, w_ab, cos_tab, sin_tab, gains, conv_w8, gpar)


GROUP = 4
GSIZE = GROUP * CHUNK
N_LEVELS = 5
M_EYE, M_PAIR, M_LEVEL0 = 0, 1, 2
M_INCL = (M_LEVEL0 + N_LEVELS, M_LEVEL0 + N_LEVELS + 2)
M_STRICT = (M_LEVEL0 + N_LEVELS + 1, M_LEVEL0 + N_LEVELS + 3)


def _scan_masks():
    r = np.arange(GSIZE)[:, None]
    c = np.arange(GSIZE)[None, :]
    blk = (r // CHUNK) == (c // CHUNK)
    planes = [r == c, (r >> 1) == (c >> 1)]
    for shift in range(1, N_LEVELS + 1):
        planes.append(((r >> (shift + 1)) == (c >> (shift + 1))) & ((r >> shift) != (c >> shift)))
    planes += [blk & (r >= c), blk & (r > c), blk & (r <= c), blk & (r < c)]
    return np.stack(planes).astype(np.float32)


def _gdn_scan_kernel(qf_ref, kf_ref, vf_ref, gbf_ref, qb_ref, kb_ref, vb_ref, gbb_ref, mask_ref,
                     of_ref, ob_ref, s_ref):
    @pl.when(pl.program_id(1) == 0)
    def _():
        s_ref[...] = jnp.zeros_like(s_ref)

    rows_of = lambda c: slice(c * CHUNK, (c + 1) * CHUNK)
    stack_rows = lambda parts: jnp.concatenate(parts, axis=0)

    groups = []
    for d, (q_ref, k_ref, v_ref, gb_ref, o_ref) in enumerate(
            ((qf_ref, kf_ref, vf_ref, gbf_ref, of_ref), (qb_ref, kb_ref, vb_ref, gbb_ref, ob_ref))):
        last = CHUNK - 1 if d == 0 else 0
        gb = gb_ref[0]
        tri = mask_ref[M_INCL[d], 0:CHUNK, 0:CHUNK].astype(BF16)
        gc = None
        for part in _split3(gb):
            term = jnp.dot(tri, part, preferred_element_type=F32)
            gc = term if gc is None else gc + term
        gc_t = gc.T
        for half in range(HEADS // GROUP):
            heads = [half * GROUP + c for c in range(GROUP)]
            lanes = [d * HEADS + h for h in heads]
            head_cols = [slice(h * HEAD_W, (h + 1) * HEAD_W) for h in heads]
            groups.append(dict(
                d=d, lanes=lanes, head_cols=head_cols, o_ref=o_ref,
                q=stack_rows([q_ref[0, :, cs].astype(F32) for cs in head_cols]),
                k=stack_rows([k_ref[0, :, cs].astype(F32) for cs in head_cols]),
                v=stack_rows([v_ref[0, :, cs].astype(F32) for cs in head_cols]),
                g_col=stack_rows([gc[:, l:l + 1] for l in lanes]),
                g_row=jnp.concatenate([gc_t[l:l + 1, :] for l in lanes], axis=1),
                g_last=stack_rows([jnp.broadcast_to(gc[last:last + 1, l:l + 1], (CHUNK, 1)) for l in lanes]),
                beta=stack_rows([gb[:, 2 * HEADS + l:2 * HEADS + l + 1] for l in lanes])))

    for g in groups:
        incl = mask_ref[M_INCL[g["d"]]]
        decay = jnp.exp((g["g_col"] - g["g_row"]) * incl) * incl
        kq = _bdot_nt(stack_rows([g["k"] * g["beta"], g["q"]]), g["k"])
        g["a"] = kq[:GSIZE] * decay * mask_ref[M_STRICT[g["d"]]]
        g["qk"] = kq[GSIZE:] * decay
        g["t"] = mask_ref[M_EYE] - g["a"] * mask_ref[M_PAIR]
    for level in range(N_LEVELS):
        for g in groups:
            g["x"] = _bdot(g["a"] * mask_ref[M_LEVEL0 + level], g["t"])
        for g in groups:
            g["t"] = g["t"] - _bdot(g["t"], g["x"])
    for g in groups:
        e_g = jnp.exp(g["g_col"])
        uw = _bdot(g["t"], jnp.concatenate([g["v"] * g["beta"], g["k"] * (g["beta"] * e_g)], axis=1))
        g["u"] = uw[:, :HEAD_W]
        g["w"] = uw[:, HEAD_W:]
        g["qd"] = g["q"] * e_g
        g["k_dec"] = g["k"] * jnp.exp(g["g_last"] - g["g_col"])

    for g in groups:
        g["s"] = [s_ref[lane] for lane in g["lanes"]]
        g["ws_qs"] = [_bdot(stack_rows([g["w"][rows_of(c)], g["qd"][rows_of(c)]]), g["s"][c])
                      for c in range(GROUP)]
    for g in groups:
        g["v_new"] = [g["u"][rows_of(c)] - g["ws_qs"][c][:CHUNK] for c in range(GROUP)]
    for g in groups:
        for c, lane in enumerate(g["lanes"]):
            s_ref[lane] = (g["s"][c] * jnp.exp(g["g_last"][c * CHUNK:c * CHUNK + 1, :])
                           + _bdot_tn(g["k_dec"][rows_of(c)], g["v_new"][c]))
    for g in groups:
        o = (stack_rows([g["ws_qs"][c][CHUNK:] for c in range(GROUP)])
             + _bdot(g["qk"], stack_rows(g["v_new"])))
        for c, cs in enumerate(g["head_cols"]):
            g["o_ref"][0, :, cs] = o[rows_of(c)].astype(g["o_ref"].dtype)


def _gdn_scan(q, k, v, gb, n_lat):
    n_batch, n_tok, _ = q.shape
    n_lat_chunks = n_lat // CHUNK
    n_ctx_chunks = (n_tok - n_lat) // CHUNK
    n_steps = n_tok // CHUNK

    def fwd_chunk(s):
        return jnp.where(s < n_ctx_chunks, n_lat_chunks + s, s - n_ctx_chunks)

    def bwd_chunk(s):
        return jnp.where(s < n_ctx_chunks, n_lat_chunks + n_ctx_chunks - 1 - s,
                         n_lat_chunks - 1 - (s - n_ctx_chunks))

    f_spec = lambda w: pl.BlockSpec((1, CHUNK, w), lambda b, s: (b, fwd_chunk(s), 0))
    b_spec = lambda w: pl.BlockSpec((1, CHUNK, w), lambda b, s: (b, bwd_chunk(s), 0))
    masks = jnp.asarray(_scan_masks())
    return pl.pallas_call(
        _gdn_scan_kernel,
        grid=(n_batch, n_steps),
        in_specs=[f_spec(GDN_WIDTH)] * 3 + [f_spec(LANES)] + [b_spec(GDN_WIDTH)] * 3 + [b_spec(LANES)]
        + [pl.BlockSpec(masks.shape, lambda b, s: (0, 0, 0))],
        out_specs=[f_spec(GDN_WIDTH), b_spec(GDN_WIDTH)],
        out_shape=[jax.ShapeDtypeStruct((n_batch, n_tok, GDN_WIDTH), BF16)] * 2,
        scratch_shapes=[pltpu.VMEM((2 * HEADS, HEAD_W, HEAD_W), F32)],
        compiler_params=pltpu.CompilerParams(
            dimension_semantics=("arbitrary", "arbitrary"), vmem_limit_bytes=VMEM_LIMIT),
        name="gdn_scan",
    )(q, k, v, gb, q, k, v, gb, masks)


ATT_TQ = 512
ATT_TK = 768


BOUND_LIMIT = 50.0


def _diff_attn_kernel(n_kv, q_ref, k_ref, v_ref, z_ref, lam_ref, normw_ref, o_ref, kx_ref, vt_ref, kmax_ref):
    tq = q_ref.shape[2]

    @pl.when(pl.program_id(2) == 0)
    def _():
        lane_k = lax.broadcasted_iota(jnp.int32, (ATT_TK, HEAD_W), 1)
        unit = jnp.where(lane_k == 0, 1.0, 0.0).astype(BF16)

        def widen(j, carry):
            n1, n2 = carry
            rows = pl.ds(pl.multiple_of(j * ATT_TK, ATT_TK), ATT_TK)
            kb = k_ref[0, rows, :]
            kx_ref[rows, 0:HEAD_W] = kb
            kx_ref[rows, HEAD_W:2 * HEAD_W] = unit
            vt_ref[:, rows] = v_ref[0, rows, :].astype(F32).T.astype(BF16)
            sq = kb.astype(F32) * kb.astype(F32)
            a1 = jnp.sum(jnp.where(lane_k < DIFF_DQK, sq, 0.0), axis=1, keepdims=True)
            a2 = jnp.sum(jnp.where(lane_k < DIFF_DQK, 0.0, sq), axis=1, keepdims=True)
            return (jnp.maximum(n1, jnp.max(a1, axis=0, keepdims=True)),
                    jnp.maximum(n2, jnp.max(a2, axis=0, keepdims=True)))

        zero11 = jnp.zeros((1, 1), F32)
        n1, n2 = lax.fori_loop(0, n_kv, widen, (zero11, zero11))
        kmax_ref[0:1, :] = jnp.broadcast_to(jnp.sqrt(n1), (1, LANES))
        kmax_ref[1:2, :] = jnp.broadcast_to(jnp.sqrt(n2), (1, LANES))

    qt = q_ref[0]
    feat = lax.broadcasted_iota(jnp.int32, (HEAD_W, tq), 0)
    first = feat < DIFF_DQK
    zero = jnp.zeros_like(qt)
    qs_t = jnp.concatenate([jnp.where(first, qt, zero), jnp.where(first, zero, qt)], axis=1)
    sq = qt.astype(F32) * qt.astype(F32)
    c1 = jnp.sqrt(jnp.sum(jnp.where(first, sq, 0.0), axis=0, keepdims=True)) * kmax_ref[0:1, 0:1]
    c2 = jnp.sqrt(jnp.sum(jnp.where(first, 0.0, sq), axis=0, keepdims=True)) * kmax_ref[1:2, 0:1]
    c = jnp.concatenate([c1, c2], axis=1)
    bound = jnp.max(c)

    def shifted_by_bound():
        feat2 = lax.broadcasted_iota(jnp.int32, (HEAD_W, 2 * tq), 0)
        aug = jnp.where(feat2 == 0, -c, 0.0).astype(BF16)
        qx_t = jnp.concatenate([qs_t, aug], axis=0)
        l = jnp.zeros((1, 2 * tq), F32)
        acc = jnp.zeros((HEAD_W, 2 * tq), F32)
        for j in range(n_kv):
            rows = slice(j * ATT_TK, (j + 1) * ATT_TK)
            p = jnp.exp2(jnp.dot(kx_ref[rows, :], qx_t, preferred_element_type=F32))
            l = l + jnp.sum(p, axis=0, keepdims=True)
            acc = acc + jnp.dot(vt_ref[:, rows], p.astype(BF16), preferred_element_type=F32)
        return acc / l

    def running_max():
        m = jnp.full((1, 2 * tq), -jnp.inf, F32)
        l = jnp.zeros((1, 2 * tq), F32)
        acc = jnp.zeros((HEAD_W, 2 * tq), F32)
        for j in range(n_kv):
            rows = slice(j * ATT_TK, (j + 1) * ATT_TK)
            s = jnp.dot(k_ref[0, rows, :], qs_t, preferred_element_type=F32)
            m_new = jnp.maximum(m, jnp.max(s, axis=0, keepdims=True))
            alpha = jnp.exp2(m - m_new)
            p = jnp.exp2(s - m_new)
            l = alpha * l + jnp.sum(p, axis=0, keepdims=True)
            acc = alpha * acc + jnp.dot(vt_ref[:, rows], p.astype(BF16), preferred_element_type=F32)
            m = m_new
        return acc / l

    o_t = lax.cond(bound < BOUND_LIMIT, shifted_by_bound, running_max)
    lam_p = lam_ref[...]
    lam = (jnp.exp(jnp.sum(lam_p[0:1, :] * lam_p[1:2, :], axis=-1, keepdims=True))
           - jnp.exp(jnp.sum(lam_p[2:3, :] * lam_p[3:4, :], axis=-1, keepdims=True)) + LAMBDA_INIT)
    o_t = o_t[:, :tq] - lam * o_t[:, tq:]
    y_t = o_t * lax.rsqrt(jnp.mean(o_t * o_t, axis=0, keepdims=True) + EPS)
    y = y_t.T * normw_ref[0:1, :]
    o_ref[0] = (y * (1.0 - LAMBDA_INIT) * _silu(z_ref[0].astype(F32))).astype(o_ref.dtype)


def _diff_attention(q_t, kn, v_b, z_b, lam_par, norm_w, n_lat):
    n_batch, n_tok, _ = kn.shape
    qt_spec = pl.BlockSpec((1, HEAD_W, ATT_TQ), lambda b, h, i: (b, h, i))
    tok_spec = pl.BlockSpec((1, ATT_TQ, HEAD_W), lambda b, h, i: (b, i, h))
    kv_spec = pl.BlockSpec((1, n_tok, HEAD_W), lambda b, h, i: (b, 0, h))
    par_spec = pl.BlockSpec((8, LANES), lambda b, h, i: (0, 0))
    return pl.pallas_call(
        functools.partial(_diff_attn_kernel, n_tok // ATT_TK),
        grid=(n_batch, HEADS, n_lat // ATT_TQ),
        in_specs=[qt_spec, kv_spec, kv_spec, tok_spec, par_spec, par_spec],
        out_specs=tok_spec,
        out_shape=jax.ShapeDtypeStruct((n_batch, n_lat, HEADS * HEAD_W), BF16),
        scratch_shapes=[pltpu.VMEM((n_tok, 2 * HEAD_W), BF16), pltpu.VMEM((HEAD_W, n_tok), BF16),
                        pltpu.VMEM((8, LANES), F32)],
        compiler_params=pltpu.CompilerParams(
            dimension_semantics=("arbitrary", "arbitrary", "arbitrary"), vmem_limit_bytes=VMEM_LIMIT),
        name="diff_attention",
    )(q_t, kn, v_b, z_b, lam_par, norm_w)


def _mixer_out_kernel(x_ref, of_ref, ob_ref, za_ref, yb_ref, gates_ref, mod_ref, normw_ref,
                      woa_ref, wob_ref, wout_ref, o_ref, ya_ref):
    b = pl.program_id(0)
    for h in range(HEADS):
        cols = slice(h * HEAD_W, (h + 1) * HEAD_W)
        o_a = of_ref[0, :, cols].astype(F32) + ob_ref[0, :, cols].astype(F32)
        y = o_a * lax.rsqrt(jnp.mean(o_a * o_a, axis=-1, keepdims=True) + EPS) * normw_ref[0:1, :]
        ya_ref[:, cols] = (y * _silu(za_ref[0, :, cols].astype(F32))).astype(BF16)
    gates = _sigmoid(gates_ref[0].astype(F32))
    merged = (gates[:, :D_MODEL] * jnp.dot(ya_ref[...], woa_ref[...], preferred_element_type=F32)
              + gates[:, D_MODEL:] * jnp.dot(yb_ref[0], wob_ref[...], preferred_element_type=F32))
    out = jnp.dot(merged.astype(BF16), wout_ref[...], preferred_element_type=F32)
    gate = mod_ref[pl.ds(b, 1), 2 * D_MODEL:3 * D_MODEL]
    o_ref[0] = x_ref[0] + gate * out


def _mixer_output(x, o_f, o_b, z_a, y_b, gates, mod, gdn_norm_w, w_oa, w_ob, w_out):
    n_batch, n_lat, _ = x.shape
    tok_spec = lambda w: pl.BlockSpec((1, ROW_TILE, w), lambda b, i: (b, i, 0))
    w_spec = pl.BlockSpec((D_MODEL, D_MODEL), lambda b, i: (0, 0))
    return pl.pallas_call(
        _mixer_out_kernel,
        grid=(n_batch, n_lat // ROW_TILE),
        in_specs=[tok_spec(D_MODEL), tok_spec(GDN_WIDTH), tok_spec(GDN_WIDTH), tok_spec(GDN_WIDTH),
                  tok_spec(D_MODEL), tok_spec(2 * D_MODEL),
                  pl.BlockSpec((8, 3 * D_MODEL), lambda b, i: (0, 0)),
                  pl.BlockSpec((8, LANES), lambda b, i: (0, 0)),
                  w_spec, w_spec, w_spec],
        out_specs=tok_spec(D_MODEL),
        out_shape=jax.ShapeDtypeStruct(x.shape, F32),
        scratch_shapes=[pltpu.VMEM((ROW_TILE, GDN_WIDTH), BF16)],
        compiler_params=pltpu.CompilerParams(
            dimension_semantics=("arbitrary", "arbitrary"), vmem_limit_bytes=VMEM_LIMIT),
        name="mixer_output",
    )(x, o_f, o_b, z_a, y_b, gates, mod, gdn_norm_w, w_oa, w_ob, w_out)


def _pad_rows(a, rows=8):
    return jnp.pad(a, ((0, rows - a.shape[0]), (0, 0)))


def _pad_lanes(a, lanes=LANES):
    return jnp.pad(a, ((0, 0), (0, lanes - a.shape[1])))


def _rope_tables(n_lat, n_ctx):
    rows = n_lat // GRID_W
    row = jnp.broadcast_to(jnp.arange(rows)[:, None], (rows, GRID_W)).reshape(-1).astype(F32)
    col = jnp.broadcast_to(jnp.arange(GRID_W)[None, :], (rows, GRID_W)).reshape(-1).astype(F32)
    n_freq = DIFF_DQK // 4
    inv_freq = ROPE_THETA ** (-jnp.arange(n_freq, dtype=F32) / n_freq)
    ang_r = row[:, None] * inv_freq
    ang_c = col[:, None] * inv_freq
    ang = jnp.concatenate([ang_r, ang_r, ang_c, ang_c] * 2, axis=-1)
    cos = jnp.concatenate([jnp.cos(ang), jnp.ones((n_ctx, LANES), F32)], axis=0)
    sin = jnp.concatenate([jnp.sin(ang), jnp.zeros((n_ctx, LANES), F32)], axis=0)
    return cos, sin


def kernel(x, c, ctx, c_ctx, w_ada, b_ada, w_in, conv_w, a_log, dt_bias, gdn_norm_w, q_norm_w, k_norm_w,
           lambda_q1, lambda_k1, lambda_q2, lambda_k2, diff_norm_w, w_oa, w_ob, w_out):
    assert w_ada.shape[0] == 1, "single-layer block"
    n_batch, n_lat, _ = x.shape
    n_ctx = ctx.shape[1]
    assert n_ctx == ROW_TILE and n_lat % ROW_TILE == 0 and n_batch < 8

    bounds = np.cumsum((0,) + IN_SIZES)
    w_left = w_in[0][:, :bounds[2]].astype(BF16)
    w_ab = _pad_lanes(w_in[0][:, bounds[2]:bounds[4]]).astype(BF16)
    w_right = w_in[0][:, bounds[4]:].astype(BF16)
    cc = _pad_rows(jnp.concatenate([c, c_ctx[None, :]], axis=0))
    conv_w8 = _pad_rows(conv_w[0])
    gpar = _pad_rows(_pad_lanes(jnp.stack([a_log[0].reshape(-1), dt_bias[0].reshape(-1)])))
    gains = _pad_rows(jnp.stack([jnp.tile(q_norm_w[0], 2), jnp.tile(k_norm_w[0], 2)]))
    lam_par = _pad_rows(_pad_lanes(jnp.stack([lambda_q1[0], lambda_k1[0], lambda_q2[0], lambda_k2[0]])))
    cos_tab, sin_tab = _rope_tables(n_lat, n_ctx)

    mod = _ada_modulation(cc, w_ada[0], b_ada)
    q_a, k_a, v_a, gb, z_a, q_t, kn, v_b, z_b, gates = _input_projection(
        x, ctx, mod, w_left, w_right, w_ab, cos_tab, sin_tab, gains, conv_w8, gpar)
    o_f, o_b = _gdn_scan(q_a, k_a, v_a, gb, n_lat)
    y_b = _diff_attention(q_t, kn, v_b, z_b, lam_par, _pad_rows(diff_norm_w), n_lat)
    return _mixer_output(x, o_f, o_b, z_a, y_b, gates, mod, _pad_rows(gdn_norm_w),
                         w_oa[0].astype(BF16), w_ob[0].astype(BF16), w_out[0].astype(BF16))
```

```python
import functools
import math

import jax
import jax.numpy as jnp
import numpy as np
from jax import lax
from jax.experimental import pallas as pl
from jax.experimental.pallas import tpu as pltpu

F32 = jnp.float32
BF16 = jnp.bfloat16

D_MODEL = 1024
GRID_W = 64
EPS = 1e-6
HEADS = 8
HEAD_W = 128
GDN_WIDTH = HEADS * HEAD_W
CONV_K = 5
CHUNK = 64
DIFF_DQK = 64
ROPE_THETA = 10000.0
LAMBDA_INIT = 0.8 - 0.6 * math.exp(-0.3 * 0)
IN_SIZES = (3 * GDN_WIDTH, GDN_WIDTH, 2 * HEADS, 2 * HEADS, 1024, 1024, 1024, 1024, 2 * D_MODEL)

ROW_TILE = 256
HALO = 16
LANES = 128
VMEM_LIMIT = 56 * 1024 * 1024


def _bdot(a, b):
    return jnp.dot(a.astype(BF16), b.astype(BF16), preferred_element_type=F32)


def _bdot_nt(a, b):
    return lax.dot_general(a.astype(BF16), b.astype(BF16), (((1,), (1,)), ((), ())),
                           preferred_element_type=F32)


def _bdot_tn(a, b):
    return lax.dot_general(a.astype(BF16), b.astype(BF16), (((0,), (0,)), ((), ())),
                           preferred_element_type=F32)


def _silu(x):
    return x * (1.0 / (1.0 + jnp.exp(-x)))


def _sigmoid(x):
    return 1.0 / (1.0 + jnp.exp(-x))


def _split3(x):
    hi = x.astype(BF16)
    r1 = x - hi.astype(F32)
    mid = r1.astype(BF16)
    lo = (r1 - mid.astype(F32)).astype(BF16)
    return hi, mid, lo


def _ada_kernel(c_ref, w_ref, b_ref, o_ref):
    a = _split3(_silu(c_ref[...]))
    w = _split3(w_ref[...])
    acc = b_ref[...]
    for i, j in ((2, 0), (1, 1), (0, 2), (1, 0), (0, 1), (0, 0)):
        acc = acc + jnp.dot(a[i], w[j], preferred_element_type=F32)
    o_ref[...] = acc


def _ada_modulation(cc, w_ada, b_ada):
    n_col = w_ada.shape[1] // D_MODEL
    return pl.pallas_call(
        _ada_kernel,
        grid=(n_col,),
        in_specs=[pl.BlockSpec((8, D_MODEL), lambda j: (0, 0)),
                  pl.BlockSpec((D_MODEL, D_MODEL), lambda j: (0, j)),
                  pl.BlockSpec((1, D_MODEL), lambda j: (0, j))],
        out_specs=pl.BlockSpec((8, D_MODEL), lambda j: (0, j)),
        out_shape=jax.ShapeDtypeStruct((8, w_ada.shape[1]), F32),
        name="ada_modulation",
    )(cc, w_ada, b_ada)


LOG2E = 1.4426950408889634


def _qk_norm_rope(x, gain, cos, sin):
    lane = lax.broadcasted_iota(jnp.int32, x.shape, 1)
    first_comp = lane < DIFF_DQK
    first_half = (lane % (DIFF_DQK // 2)) < (DIFF_DQK // 4)
    xx = x * x
    s1 = jnp.sum(jnp.where(first_comp, xx, 0.0), axis=-1, keepdims=True)
    s2 = jnp.sum(jnp.where(first_comp, 0.0, xx), axis=-1, keepdims=True)
    ms = jnp.where(first_comp, s1, s2) * (1.0 / DIFF_DQK)
    t = x * lax.rsqrt(ms + EPS) * gain
    rot = jnp.where(first_half, -pltpu.roll(t, LANES - DIFF_DQK // 4, 1), pltpu.roll(t, DIFF_DQK // 4, 1))
    return t * cos + rot * sin


def _inproj_kernel(n_lat_tiles, n_batch, x_ref, ctx_ref, mod_ref, wl_ref, wqvt_ref, wk_ref, wzg_ref,
                   cos_ref, sin_ref, cost_ref, sint_ref, gain_ref, qgain_ref, convw_ref, gpar_ref,
                   qa_ref, ka_ref, va_ref, gb_ref, za_ref, qt_ref, kn_ref, vt_ref, zb_ref, gates_ref, xp_ref):
    b = pl.program_id(0)
    i = pl.program_id(1)

    @pl.when(i == 0)
    def _():
        xp_ref[...] = jnp.zeros_like(xp_ref)

    is_ctx = i >= n_lat_tiles
    xt = jnp.where(is_ctx, ctx_ref[0], x_ref[0])
    mod = mod_ref[pl.ds(jnp.where(is_ctx, n_batch, b), 1), :]
    shift = mod[:, 0:D_MODEL]
    scale = mod[:, D_MODEL:2 * D_MODEL]
    ms = jnp.mean(xt * xt, axis=-1, keepdims=True)
    h = (xt * lax.rsqrt(ms + EPS) * (1.0 + scale) + shift).astype(BF16)
    proj = lambda w_ref, c0, cw: jnp.dot(h, w_ref[:, c0:c0 + cw], preferred_element_type=F32)
    head_cols = lambda hd: slice(hd * HEAD_W, (hd + 1) * HEAD_W)

    cur_ends_seq = (i == n_lat_tiles) | (i == n_lat_tiles + 1)
    next_starts_seq = i == n_lat_tiles
    pad = CONV_K // 2
    for which, o_ref in enumerate((qa_ref, ka_ref, va_ref)):
        res = proj(wl_ref, which * GDN_WIDTH, GDN_WIDTH)
        for hd in range(HEADS):
            cols = slice(which * GDN_WIDTH + hd * HEAD_W, which * GDN_WIDTH + (hd + 1) * HEAD_W)
            xp_ref[HALO + ROW_TILE:, cols] = jnp.where(cur_ends_seq, 0.0, res[0:HALO, head_cols(hd)])
            slab = xp_ref[:, cols].reshape(-1, 8, HEAD_W)
            sub = lax.broadcasted_iota(jnp.int32, slab.shape, 1)
            y = None
            for j in range(CONV_K):
                s = pad - j
                if s == 0:
                    shifted = slab
                else:
                    rot = pltpu.roll(slab, s % 8, 1)
                    if s > 0:
                        shifted = jnp.where(sub >= s, rot, jnp.concatenate([rot[-1:], rot[:-1]], axis=0))
                    else:
                        shifted = jnp.where(sub < 8 + s, rot, jnp.concatenate([rot[1:], rot[:1]], axis=0))
                body = shifted[HALO // 8:(HALO + ROW_TILE) // 8].reshape(ROW_TILE, HEAD_W)
                term = body * convw_ref[j:j + 1, cols]
                y = term if y is None else y + term
            y = _silu(y)
            if which < 2:
                y = y * lax.rsqrt(jnp.sum(y * y, axis=-1, keepdims=True) + EPS)
            if which == 0:
                y = y * (HEAD_W ** -0.5)
            o_ref[0, :, head_cols(hd)] = y.astype(o_ref.dtype)
            xp_ref[0:HALO, cols] = jnp.where(next_starts_seq, 0.0, xp_ref[ROW_TILE:ROW_TILE + HALO, cols])
            xp_ref[HALO:HALO + ROW_TILE, cols] = res[:, head_cols(hd)]
    za_ref[0] = proj(wl_ref, 3 * GDN_WIDTH, GDN_WIDTH).astype(za_ref.dtype)

    ab = proj(wl_ref, 4 * GDN_WIDTH, LANES)
    lane = lax.broadcasted_iota(jnp.int32, ab.shape, 1)
    xs = ab + gpar_ref[1:2, :]
    softplus = jnp.maximum(xs, 0.0) + jnp.log1p(jnp.exp(-jnp.abs(xs)))
    g_all = -jnp.exp(gpar_ref[0:1, :]) * softplus
    gb_ref[0] = jnp.where(lane < 2 * HEADS, g_all, jnp.where(lane < 4 * HEADS, _sigmoid(ab), 0.0))

    nt = (((1,), (1,)), ((), ()))
    q_t = lax.dot_general(wqvt_ref[0:D_MODEL, :], h, nt, preferred_element_type=F32)
    cos_t = cost_ref[...]
    sin_t = sint_ref[...]
    half_rows = DIFF_DQK // 4
    for hd in range(HEADS):
        t = q_t[head_cols(hd), :]
        tt = t * t
        r = jnp.concatenate(
            [jnp.broadcast_to(lax.rsqrt(jnp.mean(tt[c * DIFF_DQK:(c + 1) * DIFF_DQK], axis=0, keepdims=True) + EPS),
                              (DIFF_DQK, ROW_TILE)) for c in range(2)], axis=0)
        tn = t * r * qgain_ref[...]
        blocks = [tn[b * half_rows:(b + 1) * half_rows] for b in range(HEAD_W // half_rows)]
        partner = jnp.concatenate([blocks[b ^ 1] for b in range(len(blocks))], axis=0)
        qt_ref[0, head_cols(hd), :] = (tn * cos_t + partner * sin_t).astype(qt_ref.dtype)
    vt_ref[0] = lax.dot_general(wqvt_ref[D_MODEL:2 * D_MODEL, :], h, nt,
                                preferred_element_type=F32).astype(vt_ref.dtype)
    cos = cos_ref[...]
    sin = sin_ref[...]
    res = proj(wk_ref, 0, D_MODEL)
    for hd in range(HEADS):
        kn_ref[0, :, head_cols(hd)] = _qk_norm_rope(
            res[:, head_cols(hd)], gain_ref[1:2, :], cos, sin).astype(kn_ref.dtype)
    zb_ref[0] = proj(wzg_ref, 0, D_MODEL).astype(zb_ref.dtype)
    for c0 in range(0, 2 * D_MODEL, D_MODEL):
        gates_ref[0, :, c0:c0 + D_MODEL] = proj(wzg_ref, D_MODEL + c0, D_MODEL).astype(gates_ref.dtype)


def _input_projection(x, ctx, mod, w_all, w_qv_t, w_k, w_zg, cos_tab, sin_tab, cos_t, sin_t, gains, qgain_t,
                      conv_w8, gpar):
    n_batch, n_lat, _ = x.shape
    n_lat_tiles = n_lat // ROW_TILE
    n_tiles = n_lat_tiles + 1
    n_tok = n_lat + ctx.shape[1]
    last = n_tiles - 1
    now = lambda b, i: (b, jnp.minimum(i, last), 0)
    now_t = lambda b, i: (b, 0, jnp.minimum(i, last))
    trailing = lambda b, i: (b, jnp.maximum(i - 1, 0), 0)
    tok = lambda w, dt: jax.ShapeDtypeStruct((n_batch, n_tok, w), dt)
    feat = jax.ShapeDtypeStruct((n_batch, D_MODEL, n_tok), BF16)
    out_shape = [tok(GDN_WIDTH, BF16)] * 3 + [tok(LANES, F32), tok(GDN_WIDTH, BF16), feat,
                                              tok(D_MODEL, BF16), feat, tok(D_MODEL, BF16),
                                              tok(2 * D_MODEL, BF16)]
    tok_spec = lambda w: pl.BlockSpec((1, ROW_TILE, w), now)
    feat_spec = pl.BlockSpec((1, D_MODEL, ROW_TILE), now_t)
    out_specs = ([pl.BlockSpec((1, ROW_TILE, GDN_WIDTH), trailing)] * 3
                 + [tok_spec(LANES), tok_spec(GDN_WIDTH), feat_spec, tok_spec(D_MODEL), feat_spec,
                    tok_spec(D_MODEL), tok_spec(2 * D_MODEL)])
    resident = lambda a: pl.BlockSpec(a.shape, lambda b, i: (0, 0), pipeline_mode=pl.Buffered(1))
    tab_spec = pl.BlockSpec((ROW_TILE, LANES), lambda b, i: (jnp.minimum(i, last), 0))
    tab_t_spec = pl.BlockSpec((LANES, ROW_TILE), lambda b, i: (0, jnp.minimum(i, last)))
    small = lambda a: pl.BlockSpec(a.shape, lambda b, i: (0, 0))
    return pl.pallas_call(
        functools.partial(_inproj_kernel, n_lat_tiles, n_batch),
        grid=(n_batch, n_tiles + 1),
        in_specs=[
            pl.BlockSpec((1, ROW_TILE, D_MODEL), lambda b, i: (b, jnp.minimum(i, n_lat_tiles - 1), 0)),
            pl.BlockSpec((1, ROW_TILE, D_MODEL), lambda b, i: (b, 0, 0)),
            small(mod),
            pl.BlockSpec((D_MODEL, 4 * GDN_WIDTH + LANES), lambda b, i: (0, 0), pipeline_mode=pl.Buffered(1)),
            resident(w_qv_t), resident(w_k), resident(w_zg),
            tab_spec, tab_spec, tab_t_spec, tab_t_spec, small(gains), small(qgain_t), small(conv_w8), small(gpar),
        ],
        out_specs=out_specs,
        out_shape=out_shape,
        scratch_shapes=[pltpu.VMEM((ROW_TILE + 2 * HALO, 3 * GDN_WIDTH), F32)],
        compiler_params=pltpu.CompilerParams(
            dimension_semantics=("arbitrary", "arbitrary"), vmem_limit_bytes=VMEM_LIMIT),
        name="input_projection",
    )(x, ctx, mod, w_all, w_qv_t, w_k, w_zg, cos_tab, sin_tab, cos_t, sin_t, gains, qgain_t, conv_w8, gpar)


GROUP = 4
GSIZE = GROUP * CHUNK
N_LEVELS = 5
M_EYE, M_PAIR, M_LEVEL0 = 0, 1, 2
M_INCL = (M_LEVEL0 + N_LEVELS, M_LEVEL0 + N_LEVELS + 2)
M_STRICT = (M_LEVEL0 + N_LEVELS + 1, M_LEVEL0 + N_LEVELS + 3)
SCAN_CHUNKS = 1
SCAN_ROWS = SCAN_CHUNKS * CHUNK


def _scan_masks():
    r = np.arange(GSIZE)[:, None]
    c = np.arange(GSIZE)[None, :]
    blk = (r // CHUNK) == (c // CHUNK)
    planes = [r == c, (r >> 1) == (c >> 1)]
    for shift in range(1, N_LEVELS + 1):
        planes.append(((r >> (shift + 1)) == (c >> (shift + 1))) & ((r >> shift) != (c >> shift)))
    planes += [blk & (r >= c), blk & (r > c), blk & (r <= c), blk & (r < c)]
    return np.stack(planes).astype(np.float32)


def _gdn_scan_kernel(qf_ref, kf_ref, vf_ref, gbf_ref, qb_ref, kb_ref, vb_ref, gbb_ref, mask_ref,
                     of_ref, ob_ref, s_ref):
    @pl.when(pl.program_id(1) == 0)
    def _():
        s_ref[...] = jnp.zeros_like(s_ref)

    rows_of = lambda c: slice(c * CHUNK, (c + 1) * CHUNK)
    stack_rows = lambda parts: jnp.concatenate(parts, axis=0)

    groups = []
    for d, (q_ref, k_ref, v_ref, gb_ref, o_ref) in enumerate(
            ((qf_ref, kf_ref, vf_ref, gbf_ref, of_ref), (qb_ref, kb_ref, vb_ref, gbb_ref, ob_ref))):
        last = CHUNK - 1 if d == 0 else 0
        tri = mask_ref[M_INCL[d], 0:CHUNK, 0:CHUNK].astype(BF16)
        for pos in range(SCAN_CHUNKS):
            local = pos if d == 0 else SCAN_CHUNKS - 1 - pos
            rows = slice(local * CHUNK, (local + 1) * CHUNK)
            gb = gb_ref[0, rows, :]
            gc = None
            for part in _split3(gb):
                term = jnp.dot(tri, part, preferred_element_type=F32)
                gc = term if gc is None else gc + term
            gc_t = gc.T
            for half in range(HEADS // GROUP):
                heads = [half * GROUP + c for c in range(GROUP)]
                lanes = [d * HEADS + h for h in heads]
                head_cols = [slice(h * HEAD_W, (h + 1) * HEAD_W) for h in heads]
                groups.append(dict(
                    d=d, pos=pos, rows=rows, lanes=lanes, head_cols=head_cols, o_ref=o_ref,
                    q=stack_rows([q_ref[0, rows, cs].astype(F32) for cs in head_cols]),
                    k=stack_rows([k_ref[0, rows, cs].astype(F32) for cs in head_cols]),
                    v=stack_rows([v_ref[0, rows, cs].astype(F32) for cs in head_cols]),
                    g_col=stack_rows([gc[:, l:l + 1] for l in lanes]),
                    g_row=jnp.concatenate([gc_t[l:l + 1, :] for l in lanes], axis=1),
                    g_last=stack_rows([jnp.broadcast_to(gc[last:last + 1, l:l + 1], (CHUNK, 1))
                                       for l in lanes]),
                    beta=stack_rows([gb[:, 2 * HEADS + l:2 * HEADS + l + 1] for l in lanes])))

    for g in groups:
        incl = mask_ref[M_INCL[g["d"]]]
        decay = jnp.exp((g["g_col"] - g["g_row"]) * incl) * incl
        kq = _bdot_nt(stack_rows([g["k"] * g["beta"], g["q"]]), g["k"])
        g["a"] = kq[:GSIZE] * decay * mask_ref[M_STRICT[g["d"]]]
        g["qk"] = kq[GSIZE:] * decay
        g["t"] = mask_ref[M_EYE] - g["a"] * mask_ref[M_PAIR]
    for level in range(N_LEVELS):
        for g in groups:
            g["x"] = _bdot(g["a"] * mask_ref[M_LEVEL0 + level], g["t"])
        for g in groups:
            g["t"] = g["t"] - _bdot(g["t"], g["x"])
    for g in groups:
        e_g = jnp.exp(g["g_col"])
        uw = _bdot(g["t"], jnp.concatenate([g["v"] * g["beta"], g["k"] * (g["beta"] * e_g)], axis=1))
        g["u"] = uw[:, :HEAD_W]
        g["w"] = uw[:, HEAD_W:]
        g["qd"] = g["q"] * e_g
        g["k_dec"] = g["k"] * jnp.exp(g["g_last"] - g["g_col"])

    state = [s_ref[lane] for lane in range(2 * HEADS)]
    for pos in range(SCAN_CHUNKS):
        now = [g for g in groups if g["pos"] == pos]
        for g in now:
            g["ws_qs"] = [_bdot(stack_rows([g["w"][rows_of(c)], g["qd"][rows_of(c)]]), state[lane])
                          for c, lane in enumerate(g["lanes"])]
        for g in now:
            g["v_new"] = [g["u"][rows_of(c)] - g["ws_qs"][c][:CHUNK] for c in range(GROUP)]
        for g in now:
            for c, lane in enumerate(g["lanes"]):
                state[lane] = (state[lane] * jnp.exp(g["g_last"][c * CHUNK:c * CHUNK + 1, :])
                               + _bdot_tn(g["k_dec"][rows_of(c)], g["v_new"][c]))
        for g in now:
            o = (stack_rows([g["ws_qs"][c][CHUNK:] for c in range(GROUP)])
                 + _bdot(g["qk"], stack_rows(g["v_new"])))
            for c, cs in enumerate(g["head_cols"]):
                g["o_ref"][0, g["rows"], cs] = o[rows_of(c)].astype(g["o_ref"].dtype)
    for lane in range(2 * HEADS):
        s_ref[lane] = state[lane]


def _gdn_scan(q, k, v, gb, n_lat):
    n_batch, n_tok, _ = q.shape
    n_lat_blocks = n_lat // SCAN_ROWS
    n_ctx_blocks = (n_tok - n_lat) // SCAN_ROWS
    n_steps = n_tok // SCAN_ROWS

    def fwd_block(s):
        return jnp.where(s < n_ctx_blocks, n_lat_blocks + s, s - n_ctx_blocks)

    def bwd_block(s):
        return jnp.where(s < n_ctx_blocks, n_lat_blocks + n_ctx_blocks - 1 - s,
                         n_lat_blocks - 1 - (s - n_ctx_blocks))

    f_spec = lambda w: pl.BlockSpec((1, SCAN_ROWS, w), lambda b, s: (b, fwd_block(s), 0))
    b_spec = lambda w: pl.BlockSpec((1, SCAN_ROWS, w), lambda b, s: (b, bwd_block(s), 0))
    masks = jnp.asarray(_scan_masks())
    return pl.pallas_call(
        _gdn_scan_kernel,
        grid=(n_batch, n_steps),
        in_specs=[f_spec(GDN_WIDTH)] * 3 + [f_spec(LANES)] + [b_spec(GDN_WIDTH)] * 3 + [b_spec(LANES)]
        + [pl.BlockSpec(masks.shape, lambda b, s: (0, 0, 0))],
        out_specs=[f_spec(GDN_WIDTH), b_spec(GDN_WIDTH)],
        out_shape=[jax.ShapeDtypeStruct((n_batch, n_tok, GDN_WIDTH), BF16)] * 2,
        scratch_shapes=[pltpu.VMEM((2 * HEADS, HEAD_W, HEAD_W), F32)],
        compiler_params=pltpu.CompilerParams(
            dimension_semantics=("arbitrary", "arbitrary"), vmem_limit_bytes=VMEM_LIMIT),
        name="gdn_scan",
    )(q, k, v, gb, q, k, v, gb, masks)


ATT_TQ = 512
ATT_TK = 768


BOUND_LIMIT = 50.0


def _diff_attn_kernel(n_kv, q_ref, k_ref, vt_ref, z_ref, lam_ref, normw_ref, o_ref, kx_ref, kmax_ref):
    tq = q_ref.shape[2]

    @pl.when(pl.program_id(2) == 0)
    def _():
        lane_k = lax.broadcasted_iota(jnp.int32, (ATT_TK, HEAD_W), 1)
        unit = jnp.where(lane_k == 0, 1.0, 0.0).astype(BF16)

        def widen(j, carry):
            n1, n2 = carry
            rows = pl.ds(pl.multiple_of(j * ATT_TK, ATT_TK), ATT_TK)
            kb = k_ref[0, rows, :]
            kx_ref[rows, 0:HEAD_W] = kb
            kx_ref[rows, HEAD_W:2 * HEAD_W] = unit
            sq = kb.astype(F32) * kb.astype(F32)
            a1 = jnp.sum(jnp.where(lane_k < DIFF_DQK, sq, 0.0), axis=1, keepdims=True)
            a2 = jnp.sum(jnp.where(lane_k < DIFF_DQK, 0.0, sq), axis=1, keepdims=True)
            return (jnp.maximum(n1, jnp.max(a1, axis=0, keepdims=True)),
                    jnp.maximum(n2, jnp.max(a2, axis=0, keepdims=True)))

        zero11 = jnp.zeros((1, 1), F32)
        n1, n2 = lax.fori_loop(0, n_kv, widen, (zero11, zero11))
        kmax_ref[0:1, :] = jnp.broadcast_to(jnp.sqrt(n1), (1, LANES))
        kmax_ref[1:2, :] = jnp.broadcast_to(jnp.sqrt(n2), (1, LANES))

    qt = q_ref[0]
    feat = lax.broadcasted_iota(jnp.int32, (HEAD_W, tq), 0)
    first = feat < DIFF_DQK
    zero = jnp.zeros_like(qt)
    qs_t = jnp.concatenate([jnp.where(first, qt, zero), jnp.where(first, zero, qt)], axis=1)
    sq = qt.astype(F32) * qt.astype(F32)
    c1 = jnp.sqrt(jnp.sum(jnp.where(first, sq, 0.0), axis=0, keepdims=True)) * kmax_ref[0:1, 0:1]
    c2 = jnp.sqrt(jnp.sum(jnp.where(first, 0.0, sq), axis=0, keepdims=True)) * kmax_ref[1:2, 0:1]
    c = jnp.concatenate([c1, c2], axis=1)
    bound = jnp.max(c)

    def shifted_by_bound():
        feat2 = lax.broadcasted_iota(jnp.int32, (HEAD_W, 2 * tq), 0)
        aug = jnp.where(feat2 == 0, -c, 0.0).astype(BF16)
        qx_t = jnp.concatenate([qs_t, aug], axis=0)
        l = jnp.zeros((1, 2 * tq), F32)
        acc = jnp.zeros((HEAD_W, 2 * tq), F32)
        for j in range(n_kv):
            rows = slice(j * ATT_TK, (j + 1) * ATT_TK)
            p = jnp.exp2(jnp.dot(kx_ref[rows, :], qx_t, preferred_element_type=F32))
            l = l + jnp.sum(p, axis=0, keepdims=True)
            acc = acc + jnp.dot(vt_ref[0, :, rows], p.astype(BF16), preferred_element_type=F32)
        return acc / l

    def running_max():
        m = jnp.full((1, 2 * tq), -jnp.inf, F32)
        l = jnp.zeros((1, 2 * tq), F32)
        acc = jnp.zeros((HEAD_W, 2 * tq), F32)
        for j in range(n_kv):
            rows = slice(j * ATT_TK, (j + 1) * ATT_TK)
            s = jnp.dot(k_ref[0, rows, :], qs_t, preferred_element_type=F32)
            m_new = jnp.maximum(m, jnp.max(s, axis=0, keepdims=True))
            alpha = jnp.exp2(m - m_new)
            p = jnp.exp2(s - m_new)
            l = alpha * l + jnp.sum(p, axis=0, keepdims=True)
            acc = alpha * acc + jnp.dot(vt_ref[0, :, rows], p.astype(BF16), preferred_element_type=F32)
            m = m_new
        return acc / l

    o_t = lax.cond(bound < BOUND_LIMIT, shifted_by_bound, running_max)
    lam_p = lam_ref[...]
    lam = (jnp.exp(jnp.sum(lam_p[0:1, :] * lam_p[1:2, :], axis=-1, keepdims=True))
           - jnp.exp(jnp.sum(lam_p[2:3, :] * lam_p[3:4, :], axis=-1, keepdims=True)) + LAMBDA_INIT)
    o_t = o_t[:, :tq] - lam * o_t[:, tq:]
    y_t = o_t * lax.rsqrt(jnp.mean(o_t * o_t, axis=0, keepdims=True) + EPS)
    y = y_t.T * normw_ref[0:1, :]
    o_ref[0] = (y * (1.0 - LAMBDA_INIT) * _silu(z_ref[0].astype(F32))).astype(o_ref.dtype)


def _diff_attention(q_t, kn, v_b, z_b, lam_par, norm_w, n_lat):
    n_batch, n_tok, _ = kn.shape
    qt_spec = pl.BlockSpec((1, HEAD_W, ATT_TQ), lambda b, h, i: (b, h, i))
    tok_spec = pl.BlockSpec((1, ATT_TQ, HEAD_W), lambda b, h, i: (b, i, h))
    kv_spec = pl.BlockSpec((1, n_tok, HEAD_W), lambda b, h, i: (b, 0, h))
    par_spec = pl.BlockSpec((8, LANES), lambda b, h, i: (0, 0))
    return pl.pallas_call(
        functools.partial(_diff_attn_kernel, n_tok // ATT_TK),
        grid=(n_batch, HEADS, n_lat // ATT_TQ),
        in_specs=[qt_spec, kv_spec, pl.BlockSpec((1, HEAD_W, n_tok), lambda b, h, i: (b, h, 0)), tok_spec,
                  par_spec, par_spec],
        out_specs=tok_spec,
        out_shape=jax.ShapeDtypeStruct((n_batch, n_lat, HEADS * HEAD_W), BF16),
        scratch_shapes=[pltpu.VMEM((n_tok, 2 * HEAD_W), BF16), pltpu.VMEM((8, LANES), F32)],
        compiler_params=pltpu.CompilerParams(
            dimension_semantics=("arbitrary", "arbitrary", "arbitrary"), vmem_limit_bytes=VMEM_LIMIT),
        name="diff_attention",
    )(q_t, kn, v_b, z_b, lam_par, norm_w)


def _mixer_out_kernel(x_ref, of_ref, ob_ref, za_ref, yb_ref, gates_ref, mod_ref, normw_ref,
                      woa_ref, wob_ref, wout_ref, o_ref, ya_ref):
    b = pl.program_id(0)
    for h in range(HEADS):
        cols = slice(h * HEAD_W, (h + 1) * HEAD_W)
        o_a = of_ref[0, :, cols].astype(F32) + ob_ref[0, :, cols].astype(F32)
        y = o_a * lax.rsqrt(jnp.mean(o_a * o_a, axis=-1, keepdims=True) + EPS) * normw_ref[0:1, :]
        ya_ref[:, cols] = (y * _silu(za_ref[0, :, cols].astype(F32))).astype(BF16)
    gates = _sigmoid(gates_ref[0].astype(F32))
    merged = (gates[:, :D_MODEL] * jnp.dot(ya_ref[...], woa_ref[...], preferred_element_type=F32)
              + gates[:, D_MODEL:] * jnp.dot(yb_ref[0], wob_ref[...], preferred_element_type=F32))
    out = jnp.dot(merged.astype(BF16), wout_ref[...], preferred_element_type=F32)
    gate = mod_ref[pl.ds(b, 1), 2 * D_MODEL:3 * D_MODEL]
    o_ref[0] = x_ref[0] + gate * out


def _mixer_output(x, o_f, o_b, z_a, y_b, gates, mod, gdn_norm_w, w_oa, w_ob, w_out):
    n_batch, n_lat, _ = x.shape
    tok_spec = lambda w: pl.BlockSpec((1, ROW_TILE, w), lambda b, i: (b, i, 0))
    w_spec = pl.BlockSpec((D_MODEL, D_MODEL), lambda b, i: (0, 0))
    return pl.pallas_call(
        _mixer_out_kernel,
        grid=(n_batch, n_lat // ROW_TILE),
        in_specs=[tok_spec(D_MODEL), tok_spec(GDN_WIDTH), tok_spec(GDN_WIDTH), tok_spec(GDN_WIDTH),
                  tok_spec(D_MODEL), tok_spec(2 * D_MODEL),
                  pl.BlockSpec((8, 3 * D_MODEL), lambda b, i: (0, 0)),
                  pl.BlockSpec((8, LANES), lambda b, i: (0, 0)),
                  w_spec, w_spec, w_spec],
        out_specs=tok_spec(D_MODEL),
        out_shape=jax.ShapeDtypeStruct(x.shape, F32),
        scratch_shapes=[pltpu.VMEM((ROW_TILE, GDN_WIDTH), BF16)],
        compiler_params=pltpu.CompilerParams(
            dimension_semantics=("arbitrary", "arbitrary"), vmem_limit_bytes=VMEM_LIMIT),
        name="mixer_output",
    )(x, o_f, o_b, z_a, y_b, gates, mod, gdn_norm_w, w_oa, w_ob, w_out)


def _pad_rows(a, rows=8):
    return jnp.pad(a, ((0, rows - a.shape[0]), (0, 0)))


def _pad_lanes(a, lanes=LANES):
    return jnp.pad(a, ((0, 0), (0, lanes - a.shape[1])))


def _rope_tables(n_lat, n_ctx):
    rows = n_lat // GRID_W
    row = np.broadcast_to(np.arange(rows)[:, None], (rows, GRID_W)).reshape(-1).astype(np.float32)
    col = np.broadcast_to(np.arange(GRID_W)[None, :], (rows, GRID_W)).reshape(-1).astype(np.float32)
    n_freq = DIFF_DQK // 4
    inv_freq = (np.float32(ROPE_THETA) ** (-np.arange(n_freq, dtype=np.float32) / np.float32(n_freq))).astype(np.float32)
    ang_r = row[:, None] * inv_freq
    ang_c = col[:, None] * inv_freq
    ang = np.concatenate([ang_r, ang_r, ang_c, ang_c] * 2, axis=-1).astype(np.float32)
    cos = np.concatenate([np.cos(ang), np.ones((n_ctx, LANES), np.float32)], axis=0)
    sin = np.concatenate([np.sin(ang), np.zeros((n_ctx, LANES), np.float32)], axis=0)
    sign = np.where(np.arange(LANES) % (DIFF_DQK // 2) < DIFF_DQK // 4, -1.0, 1.0).astype(np.float32)
    return (jnp.asarray(cos, F32), jnp.asarray(sin, F32),
            jnp.asarray(np.ascontiguousarray(cos.T), F32), jnp.asarray(np.ascontiguousarray((sin * sign).T), F32))


def kernel(x, c, ctx, c_ctx, w_ada, b_ada, w_in, conv_w, a_log, dt_bias, gdn_norm_w, q_norm_w, k_norm_w,
           lambda_q1, lambda_k1, lambda_q2, lambda_k2, diff_norm_w, w_oa, w_ob, w_out):
    assert w_ada.shape[0] == 1, "single-layer block"
    n_batch, n_lat, _ = x.shape
    n_ctx = ctx.shape[1]
    assert n_ctx == ROW_TILE and n_lat % ROW_TILE == 0 and n_batch < 8

    bounds = np.cumsum((0,) + IN_SIZES)
    assert bounds[2] == 4 * GDN_WIDTH and bounds[4] - bounds[2] <= LANES
    w_all = w_in[0].astype(BF16)
    piece = lambda j0, j1: w_all[:, bounds[j0]:bounds[j1]]
    w_qv_t = jnp.concatenate([piece(4, 5).T, piece(6, 7).T], axis=0)
    w_k = piece(5, 6)
    w_zg = piece(7, 9)
    cc = _pad_rows(jnp.concatenate([c, c_ctx[None, :]], axis=0))
    conv_w8 = _pad_rows(conv_w[0])
    gpar = _pad_rows(_pad_lanes(jnp.stack([a_log[0].reshape(-1), dt_bias[0].reshape(-1)])))
    gains = _pad_rows(jnp.stack([jnp.tile(q_norm_w[0], 2), jnp.tile(k_norm_w[0], 2)]))
    qgain_t = jnp.broadcast_to((jnp.tile(q_norm_w[0], 2) * (DIFF_DQK ** -0.5 * LOG2E))[:, None],
                               (HEAD_W, ROW_TILE))
    lam_par = _pad_rows(_pad_lanes(jnp.stack([lambda_q1[0], lambda_k1[0], lambda_q2[0], lambda_k2[0]])))
    cos_tab, sin_tab, cos_t, sin_t = _rope_tables(n_lat, n_ctx)

    mod = _ada_modulation(cc, w_ada[0], b_ada)
    q_a, k_a, v_a, gb, z_a, q_t, kn, v_t, z_b, gates = _input_projection(
        x, ctx, mod, w_all, w_qv_t, w_k, w_zg, cos_tab, sin_tab, cos_t, sin_t, gains, qgain_t, conv_w8, gpar)
    o_f, o_b = _gdn_scan(q_a, k_a, v_a, gb, n_lat)
    y_b = _diff_attention(q_t, kn, v_t, z_b, lam_par, _pad_rows(diff_norm_w), n_lat)
    return _mixer_output(x, o_f, o_b, z_a, y_b, gates, mod, _pad_rows(gdn_norm_w),
                         w_oa[0].astype(BF16), w_ob[0].astype(BF16), w_out[0].astype(BF16))
```

```python
import functools
import math

import jax
import jax.numpy as jnp
import numpy as np
from jax import lax
from jax.experimental import pallas as pl
from jax.experimental.pallas import tpu as pltpu

F32 = jnp.float32
BF16 = jnp.bfloat16

D_MODEL = 1024
GRID_W = 64
EPS = 1e-6
HEADS = 8
HEAD_W = 128
GDN_WIDTH = HEADS * HEAD_W
CONV_K = 5
CHUNK = 64
DIFF_DQK = 64
ROPE_THETA = 10000.0
LAMBDA_INIT = 0.8 - 0.6 * math.exp(-0.3 * 0)
IN_SIZES = (3 * GDN_WIDTH, GDN_WIDTH, 2 * HEADS, 2 * HEADS, 1024, 1024, 1024, 1024, 2 * D_MODEL)

ROW_TILE = 256
HALO = 16
LANES = 128
VMEM_LIMIT = 56 * 1024 * 1024


def _bdot(a, b):
    return jnp.dot(a.astype(BF16), b.astype(BF16), preferred_element_type=F32)


def _bdot_nt(a, b):
    return lax.dot_general(a.astype(BF16), b.astype(BF16), (((1,), (1,)), ((), ())),
                           preferred_element_type=F32)


def _bdot_tn(a, b):
    return lax.dot_general(a.astype(BF16), b.astype(BF16), (((0,), (0,)), ((), ())),
                           preferred_element_type=F32)


def _silu(x):
    return x * (1.0 / (1.0 + jnp.exp(-x)))


def _sigmoid(x):
    return 1.0 / (1.0 + jnp.exp(-x))


def _split3(x):
    hi = x.astype(BF16)
    r1 = x - hi.astype(F32)
    mid = r1.astype(BF16)
    lo = (r1 - mid.astype(F32)).astype(BF16)
    return hi, mid, lo


def _ada_kernel(c_ref, w_ref, b_ref, o_ref):
    a = _split3(_silu(c_ref[...]))
    w = _split3(w_ref[...])
    acc = b_ref[...]
    for i, j in ((2, 0), (1, 1), (0, 2), (1, 0), (0, 1), (0, 0)):
        acc = acc + jnp.dot(a[i], w[j], preferred_element_type=F32)
    o_ref[...] = acc


def _ada_modulation(cc, w_ada, b_ada):
    n_col = w_ada.shape[1] // D_MODEL
    return pl.pallas_call(
        _ada_kernel,
        grid=(n_col,),
        in_specs=[pl.BlockSpec((8, D_MODEL), lambda j: (0, 0)),
                  pl.BlockSpec((D_MODEL, D_MODEL), lambda j: (0, j)),
                  pl.BlockSpec((1, D_MODEL), lambda j: (0, j))],
        out_specs=pl.BlockSpec((8, D_MODEL), lambda j: (0, j)),
        out_shape=jax.ShapeDtypeStruct((8, w_ada.shape[1]), F32),
        name="ada_modulation",
    )(cc, w_ada, b_ada)


LOG2E = 1.4426950408889634


def _qk_norm_rope(x, gain, cos, sin):
    lane = lax.broadcasted_iota(jnp.int32, x.shape, 1)
    first_comp = lane < DIFF_DQK
    first_half = (lane % (DIFF_DQK // 2)) < (DIFF_DQK // 4)
    xx = x * x
    s1 = jnp.sum(jnp.where(first_comp, xx, 0.0), axis=-1, keepdims=True)
    s2 = jnp.sum(jnp.where(first_comp, 0.0, xx), axis=-1, keepdims=True)
    ms = jnp.where(first_comp, s1, s2) * (1.0 / DIFF_DQK)
    t = x * lax.rsqrt(ms + EPS) * gain
    rot = jnp.where(first_half, -pltpu.roll(t, LANES - DIFF_DQK // 4, 1), pltpu.roll(t, DIFF_DQK // 4, 1))
    return t * cos + rot * sin


def _inproj_kernel(n_lat_tiles, n_batch, x_ref, ctx_ref, mod_ref, wl_ref, wqvt_ref, wk_ref, wzg_ref,
                   cos_ref, sin_ref, cost_ref, sint_ref, gain_ref, qgain_ref, convw_ref, gpar_ref,
                   qa_ref, ka_ref, va_ref, gb_ref, za_ref, qt_ref, kn_ref, vt_ref, zb_ref, gates_ref, xp_ref):
    b = pl.program_id(0)
    i = pl.program_id(1)

    @pl.when(i == 0)
    def _():
        xp_ref[...] = jnp.zeros_like(xp_ref)

    is_ctx = i >= n_lat_tiles
    xt = jnp.where(is_ctx, ctx_ref[0], x_ref[0])
    mod = mod_ref[pl.ds(jnp.where(is_ctx, n_batch, b), 1), :]
    shift = mod[:, 0:D_MODEL]
    scale = mod[:, D_MODEL:2 * D_MODEL]
    ms = jnp.mean(xt * xt, axis=-1, keepdims=True)
    h = (xt * lax.rsqrt(ms + EPS) * (1.0 + scale) + shift).astype(BF16)
    proj = lambda w_ref, c0, cw: jnp.dot(h, w_ref[:, c0:c0 + cw], preferred_element_type=F32)
    head_cols = lambda hd: slice(hd * HEAD_W, (hd + 1) * HEAD_W)

    cur_ends_seq = (i == n_lat_tiles) | (i == n_lat_tiles + 1)
    next_starts_seq = i == n_lat_tiles
    pad = CONV_K // 2
    for which, o_ref in enumerate((qa_ref, ka_ref, va_ref)):
        res = proj(wl_ref, which * GDN_WIDTH, GDN_WIDTH)
        for hd in range(HEADS):
            cols = slice(which * GDN_WIDTH + hd * HEAD_W, which * GDN_WIDTH + (hd + 1) * HEAD_W)
            xp_ref[HALO + ROW_TILE:, cols] = jnp.where(cur_ends_seq, 0.0, res[0:HALO, head_cols(hd)])
            slab = xp_ref[:, cols].reshape(-1, 8, HEAD_W)
            sub = lax.broadcasted_iota(jnp.int32, slab.shape, 1)
            y = None
            for j in range(CONV_K):
                s = pad - j
                if s == 0:
                    shifted = slab
                else:
                    rot = pltpu.roll(slab, s % 8, 1)
                    if s > 0:
                        shifted = jnp.where(sub >= s, rot, jnp.concatenate([rot[-1:], rot[:-1]], axis=0))
                    else:
                        shifted = jnp.where(sub < 8 + s, rot, jnp.concatenate([rot[1:], rot[:1]], axis=0))
                body = shifted[HALO // 8:(HALO + ROW_TILE) // 8].reshape(ROW_TILE, HEAD_W)
                term = body * convw_ref[j:j + 1, cols]
                y = term if y is None else y + term
            y = _silu(y)
            if which < 2:
                y = y * lax.rsqrt(jnp.sum(y * y, axis=-1, keepdims=True) + EPS)
            if which == 0:
                y = y * (HEAD_W ** -0.5)
            o_ref[0, :, head_cols(hd)] = y.astype(o_ref.dtype)
            xp_ref[0:HALO, cols] = jnp.where(next_starts_seq, 0.0, xp_ref[ROW_TILE:ROW_TILE + HALO, cols])
            xp_ref[HALO:HALO + ROW_TILE, cols] = res[:, head_cols(hd)]
    za_ref[0] = proj(wl_ref, 3 * GDN_WIDTH, GDN_WIDTH).astype(za_ref.dtype)

    ab = proj(wl_ref, 4 * GDN_WIDTH, LANES)
    lane = lax.broadcasted_iota(jnp.int32, ab.shape, 1)
    xs = ab + gpar_ref[1:2, :]
    softplus = jnp.maximum(xs, 0.0) + jnp.log1p(jnp.exp(-jnp.abs(xs)))
    g_all = -jnp.exp(gpar_ref[0:1, :]) * softplus
    gb_ref[0] = jnp.where(lane < 2 * HEADS, g_all, jnp.where(lane < 4 * HEADS, _sigmoid(ab), 0.0))

    nt = (((1,), (1,)), ((), ()))
    q_t = lax.dot_general(wqvt_ref[0:D_MODEL, :], h, nt, preferred_element_type=F32)
    cos_t = cost_ref[...]
    sin_t = sint_ref[...]
    half_rows = DIFF_DQK // 4
    for hd in range(HEADS):
        t = q_t[head_cols(hd), :]
        tt = t * t
        r = jnp.concatenate(
            [jnp.broadcast_to(lax.rsqrt(jnp.mean(tt[c * DIFF_DQK:(c + 1) * DIFF_DQK], axis=0, keepdims=True) + EPS),
                              (DIFF_DQK, ROW_TILE)) for c in range(2)], axis=0)
        tn = t * r * qgain_ref[...]
        blocks = [tn[b * half_rows:(b + 1) * half_rows] for b in range(HEAD_W // half_rows)]
        partner = jnp.concatenate([blocks[b ^ 1] for b in range(len(blocks))], axis=0)
        qt_ref[0, head_cols(hd), :] = (tn * cos_t + partner * sin_t).astype(qt_ref.dtype)
    vt_ref[0] = lax.dot_general(wqvt_ref[D_MODEL:2 * D_MODEL, :], h, nt,
                                preferred_element_type=F32).astype(vt_ref.dtype)
    cos = cos_ref[...]
    sin = sin_ref[...]
    res = proj(wk_ref, 0, D_MODEL)
    for hd in range(HEADS):
        kn_ref[0, :, head_cols(hd)] = _qk_norm_rope(
            res[:, head_cols(hd)], gain_ref[1:2, :], cos, sin).astype(kn_ref.dtype)
    zb_ref[0] = proj(wzg_ref, 0, D_MODEL).astype(zb_ref.dtype)
    for c0 in range(0, 2 * D_MODEL, D_MODEL):
        gates_ref[0, :, c0:c0 + D_MODEL] = proj(wzg_ref, D_MODEL + c0, D_MODEL).astype(gates_ref.dtype)


def _input_projection(x, ctx, mod, w_all, w_qv_t, w_k, w_zg, cos_tab, sin_tab, cos_t, sin_t, gains, qgain_t,
                      conv_w8, gpar):
    n_batch, n_lat, _ = x.shape
    n_lat_tiles = n_lat // ROW_TILE
    n_tiles = n_lat_tiles + 1
    n_tok = n_lat + ctx.shape[1]
    last = n_tiles - 1
    now = lambda b, i: (b, jnp.minimum(i, last), 0)
    now_t = lambda b, i: (b, 0, jnp.minimum(i, last))
    trailing = lambda b, i: (b, jnp.maximum(i - 1, 0), 0)
    tok = lambda w, dt: jax.ShapeDtypeStruct((n_batch, n_tok, w), dt)
    feat = jax.ShapeDtypeStruct((n_batch, D_MODEL, n_tok), BF16)
    out_shape = [tok(GDN_WIDTH, BF16)] * 3 + [tok(LANES, F32), tok(GDN_WIDTH, BF16), feat,
                                              tok(D_MODEL, BF16), feat, tok(D_MODEL, BF16),
                                              tok(2 * D_MODEL, BF16)]
    tok_spec = lambda w: pl.BlockSpec((1, ROW_TILE, w), now)
    feat_spec = pl.BlockSpec((1, D_MODEL, ROW_TILE), now_t)
    out_specs = ([pl.BlockSpec((1, ROW_TILE, GDN_WIDTH), trailing)] * 3
                 + [tok_spec(LANES), tok_spec(GDN_WIDTH), feat_spec, tok_spec(D_MODEL), feat_spec,
                    tok_spec(D_MODEL), tok_spec(2 * D_MODEL)])
    resident = lambda a: pl.BlockSpec(a.shape, lambda b, i: (0, 0), pipeline_mode=pl.Buffered(1))
    tab_spec = pl.BlockSpec((ROW_TILE, LANES), lambda b, i: (jnp.minimum(i, last), 0))
    tab_t_spec = pl.BlockSpec((LANES, ROW_TILE), lambda b, i: (0, jnp.minimum(i, last)))
    small = lambda a: pl.BlockSpec(a.shape, lambda b, i: (0, 0))
    return pl.pallas_call(
        functools.partial(_inproj_kernel, n_lat_tiles, n_batch),
        grid=(n_batch, n_tiles + 1),
        in_specs=[
            pl.BlockSpec((1, ROW_TILE, D_MODEL), lambda b, i: (b, jnp.minimum(i, n_lat_tiles - 1), 0)),
            pl.BlockSpec((1, ROW_TILE, D_MODEL), lambda b, i: (b, 0, 0)),
            small(mod),
            pl.BlockSpec((D_MODEL, 4 * GDN_WIDTH + LANES), lambda b, i: (0, 0), pipeline_mode=pl.Buffered(1)),
            resident(w_qv_t), resident(w_k), resident(w_zg),
            tab_spec, tab_spec, tab_t_spec, tab_t_spec, small(gains), small(qgain_t), small(conv_w8), small(gpar),
        ],
        out_specs=out_specs,
        out_shape=out_shape,
        scratch_shapes=[pltpu.VMEM((ROW_TILE + 2 * HALO, 3 * GDN_WIDTH), F32)],
        compiler_params=pltpu.CompilerParams(
            dimension_semantics=("arbitrary", "arbitrary"), vmem_limit_bytes=VMEM_LIMIT),
        name="input_projection",
    )(x, ctx, mod, w_all, w_qv_t, w_k, w_zg, cos_tab, sin_tab, cos_t, sin_t, gains, qgain_t, conv_w8, gpar)


GROUP = 4
GSIZE = GROUP * CHUNK
N_LEVELS = 5
M_EYE, M_PAIR, M_LEVEL0 = 0, 1, 2
M_INCL = (M_LEVEL0 + N_LEVELS, M_LEVEL0 + N_LEVELS + 2)
M_STRICT = (M_LEVEL0 + N_LEVELS + 1, M_LEVEL0 + N_LEVELS + 3)
SCAN_CHUNKS = 1
SCAN_ROWS = SCAN_CHUNKS * CHUNK


def _scan_masks():
    r = np.arange(GSIZE)[:, None]
    c = np.arange(GSIZE)[None, :]
    blk = (r // CHUNK) == (c // CHUNK)
    planes = [r == c, (r >> 1) == (c >> 1)]
    for shift in range(1, N_LEVELS + 1):
        planes.append(((r >> (shift + 1)) == (c >> (shift + 1))) & ((r >> shift) != (c >> shift)))
    planes += [blk & (r >= c), blk & (r > c), blk & (r <= c), blk & (r < c)]
    return np.stack(planes).astype(np.float32)


def _gdn_ctx_kernel(qf_ref, kf_ref, vf_ref, gbf_ref, qb_ref, kb_ref, vb_ref, gbb_ref, mask_ref,
                    of_ref, ob_ref, state_ref, s_ref):
    @pl.when(pl.program_id(1) == 0)
    def _():
        s_ref[...] = jnp.zeros_like(s_ref)

    _gdn_chunk_step(qf_ref, kf_ref, vf_ref, gbf_ref, qb_ref, kb_ref, vb_ref, gbb_ref, mask_ref,
                    of_ref, ob_ref, s_ref)
    state_ref[0] = s_ref[...]


def _gdn_chunk_step(qf_ref, kf_ref, vf_ref, gbf_ref, qb_ref, kb_ref, vb_ref, gbb_ref, mask_ref,
                    of_ref, ob_ref, s_ref):
    rows_of = lambda c: slice(c * CHUNK, (c + 1) * CHUNK)
    stack_rows = lambda parts: jnp.concatenate(parts, axis=0)

    groups = []
    for d, (q_ref, k_ref, v_ref, gb_ref, o_ref) in enumerate(
            ((qf_ref, kf_ref, vf_ref, gbf_ref, of_ref), (qb_ref, kb_ref, vb_ref, gbb_ref, ob_ref))):
        last = CHUNK - 1 if d == 0 else 0
        tri = mask_ref[M_INCL[d], 0:CHUNK, 0:CHUNK].astype(BF16)
        for pos in range(SCAN_CHUNKS):
            local = pos if d == 0 else SCAN_CHUNKS - 1 - pos
            rows = slice(local * CHUNK, (local + 1) * CHUNK)
            gb = gb_ref[0, rows, :]
            gc = None
            for part in _split3(gb):
                term = jnp.dot(tri, part, preferred_element_type=F32)
                gc = term if gc is None else gc + term
            gc_t = gc.T
            for half in range(HEADS // GROUP):
                heads = [half * GROUP + c for c in range(GROUP)]
                lanes = [d * HEADS + h for h in heads]
                head_cols = [slice(h * HEAD_W, (h + 1) * HEAD_W) for h in heads]
                groups.append(dict(
                    d=d, pos=pos, rows=rows, lanes=lanes, head_cols=head_cols, o_ref=o_ref,
                    q=stack_rows([q_ref[0, rows, cs].astype(F32) for cs in head_cols]),
                    k=stack_rows([k_ref[0, rows, cs].astype(F32) for cs in head_cols]),
                    v=stack_rows([v_ref[0, rows, cs].astype(F32) for cs in head_cols]),
                    g_col=stack_rows([gc[:, l:l + 1] for l in lanes]),
                    g_row=jnp.concatenate([gc_t[l:l + 1, :] for l in lanes], axis=1),
                    g_last=stack_rows([jnp.broadcast_to(gc[last:last + 1, l:l + 1], (CHUNK, 1))
                                       for l in lanes]),
                    beta=stack_rows([gb[:, 2 * HEADS + l:2 * HEADS + l + 1] for l in lanes])))

    for g in groups:
        incl = mask_ref[M_INCL[g["d"]]]
        decay = jnp.exp((g["g_col"] - g["g_row"]) * incl) * incl
        kq = _bdot_nt(stack_rows([g["k"] * g["beta"], g["q"]]), g["k"])
        g["a"] = kq[:GSIZE] * decay * mask_ref[M_STRICT[g["d"]]]
        g["qk"] = kq[GSIZE:] * decay
        g["t"] = mask_ref[M_EYE] - g["a"] * mask_ref[M_PAIR]
    for level in range(N_LEVELS):
        for g in groups:
            g["x"] = _bdot(g["a"] * mask_ref[M_LEVEL0 + level], g["t"])
        for g in groups:
            g["t"] = g["t"] - _bdot(g["t"], g["x"])
    for g in groups:
        e_g = jnp.exp(g["g_col"])
        uw = _bdot(g["t"], jnp.concatenate([g["v"] * g["beta"], g["k"] * (g["beta"] * e_g)], axis=1))
        g["u"] = uw[:, :HEAD_W]
        g["w"] = uw[:, HEAD_W:]
        g["qd"] = g["q"] * e_g
        g["k_dec"] = g["k"] * jnp.exp(g["g_last"] - g["g_col"])

    state = [s_ref[lane] for lane in range(2 * HEADS)]
    for pos in range(SCAN_CHUNKS):
        now = [g for g in groups if g["pos"] == pos]
        for g in now:
            g["ws_qs"] = [_bdot(stack_rows([g["w"][rows_of(c)], g["qd"][rows_of(c)]]), state[lane])
                          for c, lane in enumerate(g["lanes"])]
        for g in now:
            g["v_new"] = [g["u"][rows_of(c)] - g["ws_qs"][c][:CHUNK] for c in range(GROUP)]
        for g in now:
            for c, lane in enumerate(g["lanes"]):
                state[lane] = (state[lane] * jnp.exp(g["g_last"][c * CHUNK:c * CHUNK + 1, :])
                               + _bdot_tn(g["k_dec"][rows_of(c)], g["v_new"][c]))
        for g in now:
            o = (stack_rows([g["ws_qs"][c][CHUNK:] for c in range(GROUP)])
                 + _bdot(g["qk"], stack_rows(g["v_new"])))
            for c, cs in enumerate(g["head_cols"]):
                g["o_ref"][0, g["rows"], cs] = o[rows_of(c)].astype(g["o_ref"].dtype)
    for lane in range(2 * HEADS):
        s_ref[lane] = state[lane]


def _gdn_context_state(q, k, v, gb, masks, n_lat):
    assert SCAN_CHUNKS == 1
    n_batch, n_tok, _ = q.shape
    n_lat_chunks = n_lat // CHUNK
    n_ctx_chunks = (n_tok - n_lat) // CHUNK
    f_spec = lambda w: pl.BlockSpec((1, CHUNK, w), lambda b, s: (b, n_lat_chunks + s, 0))
    b_spec = lambda w: pl.BlockSpec((1, CHUNK, w), lambda b, s: (b, n_lat_chunks + n_ctx_chunks - 1 - s, 0))
    ctx_out = pl.BlockSpec((1, CHUNK, GDN_WIDTH), lambda b, s: (b, s, 0))
    state_shape = (n_batch, 2 * HEADS, HEAD_W, HEAD_W)
    return pl.pallas_call(
        _gdn_ctx_kernel,
        grid=(n_batch, n_ctx_chunks),
        in_specs=[f_spec(GDN_WIDTH)] * 3 + [f_spec(LANES)] + [b_spec(GDN_WIDTH)] * 3 + [b_spec(LANES)]
        + [pl.BlockSpec(masks.shape, lambda b, s: (0, 0, 0))],
        out_specs=[ctx_out, ctx_out, pl.BlockSpec((1,) + state_shape[1:], lambda b, s: (b, 0, 0, 0))],
        out_shape=[jax.ShapeDtypeStruct((n_batch, n_tok - n_lat, GDN_WIDTH), BF16)] * 2
        + [jax.ShapeDtypeStruct(state_shape, F32)],
        scratch_shapes=[pltpu.VMEM(state_shape[1:], F32)],
        compiler_params=pltpu.CompilerParams(
            dimension_semantics=("arbitrary", "arbitrary"), vmem_limit_bytes=VMEM_LIMIT),
        name="gdn_context",
    )(q, k, v, gb, q, k, v, gb, masks)[2]


ATT_TQ = 512
ATT_TK = 768


BOUND_LIMIT = 50.0


def _attn_gdn_kernel(n_kv, n_q_tiles, q_ref, k_ref, vt_ref, z_ref, lam_ref, normw_ref,
                     qf_ref, kf_ref, vf_ref, gbf_ref, qb_ref, kb_ref, vb_ref, gbb_ref, mask_ref, s0_ref,
                     o_ref, of_ref, ob_ref, kx_ref, kmax_ref, s_ref):
    tq = q_ref.shape[2]
    step = pl.program_id(1)

    @pl.when(step == 0)
    def _():
        s_ref[...] = s0_ref[0]

    def gdn_step():
        _gdn_chunk_step(qf_ref, kf_ref, vf_ref, gbf_ref, qb_ref, kb_ref, vb_ref, gbb_ref, mask_ref,
                        of_ref, ob_ref, s_ref)

    @pl.when(step % n_q_tiles == 0)
    def _():
        lane_k = lax.broadcasted_iota(jnp.int32, (ATT_TK, HEAD_W), 1)
        unit = jnp.where(lane_k == 0, 1.0, 0.0).astype(BF16)

        def widen(j, carry):
            n1, n2 = carry
            rows = pl.ds(pl.multiple_of(j * ATT_TK, ATT_TK), ATT_TK)
            kb = k_ref[0, rows, :]
            kx_ref[rows, 0:HEAD_W] = kb
            kx_ref[rows, HEAD_W:2 * HEAD_W] = unit
            sq = kb.astype(F32) * kb.astype(F32)
            a1 = jnp.sum(jnp.where(lane_k < DIFF_DQK, sq, 0.0), axis=1, keepdims=True)
            a2 = jnp.sum(jnp.where(lane_k < DIFF_DQK, 0.0, sq), axis=1, keepdims=True)
            return (jnp.maximum(n1, jnp.max(a1, axis=0, keepdims=True)),
                    jnp.maximum(n2, jnp.max(a2, axis=0, keepdims=True)))

        zero11 = jnp.zeros((1, 1), F32)
        n1, n2 = lax.fori_loop(0, n_kv, widen, (zero11, zero11))
        kmax_ref[0:1, :] = jnp.broadcast_to(jnp.sqrt(n1), (1, LANES))
        kmax_ref[1:2, :] = jnp.broadcast_to(jnp.sqrt(n2), (1, LANES))

    qt = q_ref[0]
    feat = lax.broadcasted_iota(jnp.int32, (HEAD_W, tq), 0)
    first = feat < DIFF_DQK
    zero = jnp.zeros_like(qt)
    qs_t = jnp.concatenate([jnp.where(first, qt, zero), jnp.where(first, zero, qt)], axis=1)
    sq = qt.astype(F32) * qt.astype(F32)
    c1 = jnp.sqrt(jnp.sum(jnp.where(first, sq, 0.0), axis=0, keepdims=True)) * kmax_ref[0:1, 0:1]
    c2 = jnp.sqrt(jnp.sum(jnp.where(first, 0.0, sq), axis=0, keepdims=True)) * kmax_ref[1:2, 0:1]
    c = jnp.concatenate([c1, c2], axis=1)
    bound = jnp.max(c)

    def shifted_by_bound():
        gdn_step()
        feat2 = lax.broadcasted_iota(jnp.int32, (HEAD_W, 2 * tq), 0)
        aug = jnp.where(feat2 == 0, -c, 0.0).astype(BF16)
        qx_t = jnp.concatenate([qs_t, aug], axis=0)
        l = jnp.zeros((1, 2 * tq), F32)
        acc = jnp.zeros((HEAD_W, 2 * tq), F32)
        for j in range(n_kv):
            rows = slice(j * ATT_TK, (j + 1) * ATT_TK)
            p = jnp.exp2(jnp.dot(kx_ref[rows, :], qx_t, preferred_element_type=F32))
            l = l + jnp.sum(p, axis=0, keepdims=True)
            acc = acc + jnp.dot(vt_ref[0, :, rows], p.astype(BF16), preferred_element_type=F32)
        return acc / l

    def running_max():
        gdn_step()
        m = jnp.full((1, 2 * tq), -jnp.inf, F32)
        l = jnp.zeros((1, 2 * tq), F32)
        acc = jnp.zeros((HEAD_W, 2 * tq), F32)
        for j in range(n_kv):
            rows = slice(j * ATT_TK, (j + 1) * ATT_TK)
            s = jnp.dot(k_ref[0, rows, :], qs_t, preferred_element_type=F32)
            m_new = jnp.maximum(m, jnp.max(s, axis=0, keepdims=True))
            alpha = jnp.exp2(m - m_new)
            p = jnp.exp2(s - m_new)
            l = alpha * l + jnp.sum(p, axis=0, keepdims=True)
            acc = alpha * acc + jnp.dot(vt_ref[0, :, rows], p.astype(BF16), preferred_element_type=F32)
            m = m_new
        return acc / l

    o_t = lax.cond(bound < BOUND_LIMIT, shifted_by_bound, running_max)
    lam_p = lam_ref[...]
    lam = (jnp.exp(jnp.sum(lam_p[0:1, :] * lam_p[1:2, :], axis=-1, keepdims=True))
           - jnp.exp(jnp.sum(lam_p[2:3, :] * lam_p[3:4, :], axis=-1, keepdims=True)) + LAMBDA_INIT)
    o_t = o_t[:, :tq] - lam * o_t[:, tq:]
    y_t = o_t * lax.rsqrt(jnp.mean(o_t * o_t, axis=0, keepdims=True) + EPS)
    y = y_t.T * normw_ref[0:1, :]
    o_ref[0] = (y * (1.0 - LAMBDA_INIT) * _silu(z_ref[0].astype(F32))).astype(o_ref.dtype)


def _attention_and_gdn_scan(q_t, kn, v_t, z_b, lam_par, norm_w, q_a, k_a, v_a, gb, n_lat):
    assert SCAN_CHUNKS == 1
    n_batch, n_tok, _ = kn.shape
    n_q_tiles = n_lat // ATT_TQ
    n_steps = HEADS * n_q_tiles
    assert n_steps == n_lat // CHUNK
    masks = jnp.asarray(_scan_masks())
    state0 = _gdn_context_state(q_a, k_a, v_a, gb, masks, n_lat)

    head = lambda s: s // n_q_tiles
    tile = lambda s: s % n_q_tiles
    qt_spec = pl.BlockSpec((1, HEAD_W, ATT_TQ), lambda b, s: (b, head(s), tile(s)))
    tok_spec = pl.BlockSpec((1, ATT_TQ, HEAD_W), lambda b, s: (b, tile(s), head(s)))
    k_spec = pl.BlockSpec((1, n_tok, HEAD_W), lambda b, s: (b, 0, head(s)))
    vt_spec = pl.BlockSpec((1, HEAD_W, n_tok), lambda b, s: (b, head(s), 0))
    par_spec = pl.BlockSpec((8, LANES), lambda b, s: (0, 0))
    f_spec = lambda w: pl.BlockSpec((1, CHUNK, w), lambda b, s: (b, s, 0))
    b_spec = lambda w: pl.BlockSpec((1, CHUNK, w), lambda b, s: (b, n_steps - 1 - s, 0))
    state_shape = state0.shape
    y_b, o_f, o_b = pl.pallas_call(
        functools.partial(_attn_gdn_kernel, n_tok // ATT_TK, n_q_tiles),
        grid=(n_batch, n_steps),
        in_specs=[qt_spec, k_spec, vt_spec, tok_spec, par_spec, par_spec]
        + [f_spec(GDN_WIDTH)] * 3 + [f_spec(LANES)] + [b_spec(GDN_WIDTH)] * 3 + [b_spec(LANES)]
        + [pl.BlockSpec(masks.shape, lambda b, s: (0, 0, 0)),
           pl.BlockSpec((1,) + state_shape[1:], lambda b, s: (b, 0, 0, 0))],
        out_specs=[tok_spec, f_spec(GDN_WIDTH), b_spec(GDN_WIDTH)],
        out_shape=[jax.ShapeDtypeStruct((n_batch, n_lat, HEADS * HEAD_W), BF16)]
        + [jax.ShapeDtypeStruct((n_batch, n_lat, GDN_WIDTH), BF16)] * 2,
        scratch_shapes=[pltpu.VMEM((n_tok, 2 * HEAD_W), BF16), pltpu.VMEM((8, LANES), F32),
                        pltpu.VMEM(state_shape[1:], F32)],
        compiler_params=pltpu.CompilerParams(
            dimension_semantics=("arbitrary", "arbitrary"), vmem_limit_bytes=VMEM_LIMIT),
        name="attention_gdn_scan",
    )(q_t, kn, v_t, z_b, lam_par, norm_w, q_a, k_a, v_a, gb, q_a, k_a, v_a, gb, masks, state0)
    return y_b, o_f, o_b


def _mixer_out_kernel(x_ref, of_ref, ob_ref, za_ref, yb_ref, gates_ref, mod_ref, normw_ref,
                      woa_ref, wob_ref, wout_ref, o_ref, ya_ref):
    b = pl.program_id(0)
    for h in range(HEADS):
        cols = slice(h * HEAD_W, (h + 1) * HEAD_W)
        o_a = of_ref[0, :, cols].astype(F32) + ob_ref[0, :, cols].astype(F32)
        y = o_a * lax.rsqrt(jnp.mean(o_a * o_a, axis=-1, keepdims=True) + EPS) * normw_ref[0:1, :]
        ya_ref[:, cols] = (y * _silu(za_ref[0, :, cols].astype(F32))).astype(BF16)
    gates = _sigmoid(gates_ref[0].astype(F32))
    merged = (gates[:, :D_MODEL] * jnp.dot(ya_ref[...], woa_ref[...], preferred_element_type=F32)
              + gates[:, D_MODEL:] * jnp.dot(yb_ref[0], wob_ref[...], preferred_element_type=F32))
    out = jnp.dot(merged.astype(BF16), wout_ref[...], preferred_element_type=F32)
    gate = mod_ref[pl.ds(b, 1), 2 * D_MODEL:3 * D_MODEL]
    o_ref[0] = x_ref[0] + gate * out


def _mixer_output(x, o_f, o_b, z_a, y_b, gates, mod, gdn_norm_w, w_oa, w_ob, w_out):
    n_batch, n_lat, _ = x.shape
    tok_spec = lambda w: pl.BlockSpec((1, ROW_TILE, w), lambda b, i: (b, i, 0))
    w_spec = pl.BlockSpec((D_MODEL, D_MODEL), lambda b, i: (0, 0))
    return pl.pallas_call(
        _mixer_out_kernel,
        grid=(n_batch, n_lat // ROW_TILE),
        in_specs=[tok_spec(D_MODEL), tok_spec(GDN_WIDTH), tok_spec(GDN_WIDTH), tok_spec(GDN_WIDTH),
                  tok_spec(D_MODEL), tok_spec(2 * D_MODEL),
                  pl.BlockSpec((8, 3 * D_MODEL), lambda b, i: (0, 0)),
                  pl.BlockSpec((8, LANES), lambda b, i: (0, 0)),
                  w_spec, w_spec, w_spec],
        out_specs=tok_spec(D_MODEL),
        out_shape=jax.ShapeDtypeStruct(x.shape, F32),
        scratch_shapes=[pltpu.VMEM((ROW_TILE, GDN_WIDTH), BF16)],
        compiler_params=pltpu.CompilerParams(
            dimension_semantics=("arbitrary", "arbitrary"), vmem_limit_bytes=VMEM_LIMIT),
        name="mixer_output",
    )(x, o_f, o_b, z_a, y_b, gates, mod, gdn_norm_w, w_oa, w_ob, w_out)


def _pad_rows(a, rows=8):
    return jnp.pad(a, ((0, rows - a.shape[0]), (0, 0)))


def _pad_lanes(a, lanes=LANES):
    return jnp.pad(a, ((0, 0), (0, lanes - a.shape[1])))


def _rope_tables(n_lat, n_ctx):
    rows = n_lat // GRID_W
    row = np.broadcast_to(np.arange(rows)[:, None], (rows, GRID_W)).reshape(-1).astype(np.float32)
    col = np.broadcast_to(np.arange(GRID_W)[None, :], (rows, GRID_W)).reshape(-1).astype(np.float32)
    n_freq = DIFF_DQK // 4
    inv_freq = (np.float32(ROPE_THETA) ** (-np.arange(n_freq, dtype=np.float32) / np.float32(n_freq))).astype(np.float32)
    ang_r = row[:, None] * inv_freq
    ang_c = col[:, None] * inv_freq
    ang = np.concatenate([ang_r, ang_r, ang_c, ang_c] * 2, axis=-1).astype(np.float32)
    cos = np.concatenate([np.cos(ang), np.ones((n_ctx, LANES), np.float32)], axis=0)
    sin = np.concatenate([np.sin(ang), np.zeros((n_ctx, LANES), np.float32)], axis=0)
    sign = np.where(np.arange(LANES) % (DIFF_DQK // 2) < DIFF_DQK // 4, -1.0, 1.0).astype(np.float32)
    return (jnp.asarray(cos, F32), jnp.asarray(sin, F32),
            jnp.asarray(np.ascontiguousarray(cos.T), F32), jnp.asarray(np.ascontiguousarray((sin * sign).T), F32))


def kernel(x, c, ctx, c_ctx, w_ada, b_ada, w_in, conv_w, a_log, dt_bias, gdn_norm_w, q_norm_w, k_norm_w,
           lambda_q1, lambda_k1, lambda_q2, lambda_k2, diff_norm_w, w_oa, w_ob, w_out):
    assert w_ada.shape[0] == 1, "single-layer block"
    n_batch, n_lat, _ = x.shape
    n_ctx = ctx.shape[1]
    assert n_ctx == ROW_TILE and n_lat % ROW_TILE == 0 and n_batch < 8

    bounds = np.cumsum((0,) + IN_SIZES)
    assert bounds[2] == 4 * GDN_WIDTH and bounds[4] - bounds[2] <= LANES
    w_all = w_in[0].astype(BF16)
    piece = lambda j0, j1: w_all[:, bounds[j0]:bounds[j1]]
    w_qv_t = jnp.concatenate([piece(4, 5).T, piece(6, 7).T], axis=0)
    w_k = piece(5, 6)
    w_zg = piece(7, 9)
    cc = _pad_rows(jnp.concatenate([c, c_ctx[None, :]], axis=0))
    conv_w8 = _pad_rows(conv_w[0])
    gpar = _pad_rows(_pad_lanes(jnp.stack([a_log[0].reshape(-1), dt_bias[0].reshape(-1)])))
    gains = _pad_rows(jnp.stack([jnp.tile(q_norm_w[0], 2), jnp.tile(k_norm_w[0], 2)]))
    qgain_t = jnp.broadcast_to((jnp.tile(q_norm_w[0], 2) * (DIFF_DQK ** -0.5 * LOG2E))[:, None],
                               (HEAD_W, ROW_TILE))
    lam_par = _pad_rows(_pad_lanes(jnp.stack([lambda_q1[0], lambda_k1[0], lambda_q2[0], lambda_k2[0]])))
    cos_tab, sin_tab, cos_t, sin_t = _rope_tables(n_lat, n_ctx)

    mod = _ada_modulation(cc, w_ada[0], b_ada)
    q_a, k_a, v_a, gb, z_a, q_t, kn, v_t, z_b, gates = _input_projection(
        x, ctx, mod, w_all, w_qv_t, w_k, w_zg, cos_tab, sin_tab, cos_t, sin_t, gains, qgain_t, conv_w8, gpar)
    y_b, o_f, o_b = _attention_and_gdn_scan(q_t, kn, v_t, z_b, lam_par, _pad_rows(diff_norm_w),
                                            q_a, k_a, v_a, gb, n_lat)
    return _mixer_output(x, o_f, o_b, z_a, y_b, gates, mod, _pad_rows(gdn_norm_w),
                         w_oa[0].astype(BF16), w_ob[0].astype(BF16), w_out[0].astype(BF16))
```

```python
import functools
import math

import jax
import jax.numpy as jnp
import numpy as np
from jax import lax
from jax.experimental import pallas as pl
from jax.experimental.pallas import tpu as pltpu

F32 = jnp.float32
BF16 = jnp.bfloat16

D_MODEL = 1024
GRID_W = 64
EPS = 1e-6
HEADS = 8
HEAD_W = 128
GDN_WIDTH = HEADS * HEAD_W
CONV_K = 5
CHUNK = 64
DIFF_DQK = 64
ROPE_THETA = 10000.0
LAMBDA_INIT = 0.8 - 0.6 * math.exp(-0.3 * 0)
IN_SIZES = (3 * GDN_WIDTH, GDN_WIDTH, 2 * HEADS, 2 * HEADS, 1024, 1024, 1024, 1024, 2 * D_MODEL)

ROW_TILE = 256
HALO = 16
LANES = 128
VMEM_LIMIT = 56 * 1024 * 1024


def _bdot(a, b):
    return jnp.dot(a.astype(BF16), b.astype(BF16), preferred_element_type=F32)


def _bdot_nt(a, b):
    return lax.dot_general(a.astype(BF16), b.astype(BF16), (((1,), (1,)), ((), ())),
                           preferred_element_type=F32)


def _bdot_tn(a, b):
    return lax.dot_general(a.astype(BF16), b.astype(BF16), (((0,), (0,)), ((), ())),
                           preferred_element_type=F32)


def _silu(x):
    return x * (1.0 / (1.0 + jnp.exp(-x)))


def _sigmoid(x):
    return 1.0 / (1.0 + jnp.exp(-x))


def _split3(x):
    hi = x.astype(BF16)
    r1 = x - hi.astype(F32)
    mid = r1.astype(BF16)
    lo = (r1 - mid.astype(F32)).astype(BF16)
    return hi, mid, lo


def _ada_kernel(c_ref, w_ref, b_ref, o_ref):
    a = _split3(_silu(c_ref[...]))
    w = _split3(w_ref[...])
    acc = b_ref[...]
    for i, j in ((2, 0), (1, 1), (0, 2), (1, 0), (0, 1), (0, 0)):
        acc = acc + jnp.dot(a[i], w[j], preferred_element_type=F32)
    o_ref[...] = acc


def _ada_modulation(cc, w_ada, b_ada):
    n_col = w_ada.shape[1] // D_MODEL
    return pl.pallas_call(
        _ada_kernel,
        grid=(n_col,),
        in_specs=[pl.BlockSpec((8, D_MODEL), lambda j: (0, 0)),
                  pl.BlockSpec((D_MODEL, D_MODEL), lambda j: (0, j)),
                  pl.BlockSpec((1, D_MODEL), lambda j: (0, j))],
        out_specs=pl.BlockSpec((8, D_MODEL), lambda j: (0, j)),
        out_shape=jax.ShapeDtypeStruct((8, w_ada.shape[1]), F32),
        name="ada_modulation",
    )(cc, w_ada, b_ada)


LOG2E = 1.4426950408889634


def _qk_norm_rope(x, gain, cos, sin):
    lane = lax.broadcasted_iota(jnp.int32, x.shape, 1)
    first_comp = lane < DIFF_DQK
    first_half = (lane % (DIFF_DQK // 2)) < (DIFF_DQK // 4)
    xx = x * x
    s1 = jnp.sum(jnp.where(first_comp, xx, 0.0), axis=-1, keepdims=True)
    s2 = jnp.sum(jnp.where(first_comp, 0.0, xx), axis=-1, keepdims=True)
    ms = jnp.where(first_comp, s1, s2) * (1.0 / DIFF_DQK)
    t = x * lax.rsqrt(ms + EPS) * gain
    rot = jnp.where(first_half, -pltpu.roll(t, LANES - DIFF_DQK // 4, 1), pltpu.roll(t, DIFF_DQK // 4, 1))
    return t * cos + rot * sin


def _inproj_kernel(n_lat_tiles, n_batch, x_ref, ctx_ref, mod_ref, wl_ref, wqvt_ref, wk_ref, wzg_ref,
                   cos_ref, sin_ref, cost_ref, sint_ref, gain_ref, qgain_ref, convw_ref, gpar_ref,
                   qa_ref, ka_ref, va_ref, gb_ref, za_ref, qt_ref, kn_ref, vt_ref, zb_ref, gates_ref, xp_ref):
    b = pl.program_id(0)
    i = pl.program_id(1)

    @pl.when(i == 0)
    def _():
        xp_ref[...] = jnp.zeros_like(xp_ref)

    is_ctx = i >= n_lat_tiles
    xt = jnp.where(is_ctx, ctx_ref[0], x_ref[0])
    mod = mod_ref[pl.ds(jnp.where(is_ctx, n_batch, b), 1), :]
    shift = mod[:, 0:D_MODEL]
    scale = mod[:, D_MODEL:2 * D_MODEL]
    ms = jnp.mean(xt * xt, axis=-1, keepdims=True)
    h = (xt * lax.rsqrt(ms + EPS) * (1.0 + scale) + shift).astype(BF16)
    proj = lambda w_ref, c0, cw: jnp.dot(h, w_ref[:, c0:c0 + cw], preferred_element_type=F32)
    head_cols = lambda hd: slice(hd * HEAD_W, (hd + 1) * HEAD_W)

    cur_ends_seq = (i == n_lat_tiles) | (i == n_lat_tiles + 1)
    next_starts_seq = i == n_lat_tiles
    pad = CONV_K // 2
    for which, o_ref in enumerate((qa_ref, ka_ref, va_ref)):
        res = proj(wl_ref, which * GDN_WIDTH, GDN_WIDTH)
        for hd in range(HEADS):
            cols = slice(which * GDN_WIDTH + hd * HEAD_W, which * GDN_WIDTH + (hd + 1) * HEAD_W)
            xp_ref[HALO + ROW_TILE:, cols] = jnp.where(cur_ends_seq, 0.0, res[0:HALO, head_cols(hd)])
            slab = xp_ref[:, cols].reshape(-1, 8, HEAD_W)
            sub = lax.broadcasted_iota(jnp.int32, slab.shape, 1)
            y = None
            for j in range(CONV_K):
                s = pad - j
                if s == 0:
                    shifted = slab
                else:
                    rot = pltpu.roll(slab, s % 8, 1)
                    if s > 0:
                        shifted = jnp.where(sub >= s, rot, jnp.concatenate([rot[-1:], rot[:-1]], axis=0))
                    else:
                        shifted = jnp.where(sub < 8 + s, rot, jnp.concatenate([rot[1:], rot[:1]], axis=0))
                body = shifted[HALO // 8:(HALO + ROW_TILE) // 8].reshape(ROW_TILE, HEAD_W)
                term = body * convw_ref[j:j + 1, cols]
                y = term if y is None else y + term
            y = _silu(y)
            if which < 2:
                y = y * lax.rsqrt(jnp.sum(y * y, axis=-1, keepdims=True) + EPS)
            if which == 0:
                y = y * (HEAD_W ** -0.5)
            o_ref[0, :, head_cols(hd)] = y.astype(o_ref.dtype)
            xp_ref[0:HALO, cols] = jnp.where(next_starts_seq, 0.0, xp_ref[ROW_TILE:ROW_TILE + HALO, cols])
            xp_ref[HALO:HALO + ROW_TILE, cols] = res[:, head_cols(hd)]
    za_ref[0] = proj(wl_ref, 3 * GDN_WIDTH, GDN_WIDTH).astype(za_ref.dtype)

    ab = proj(wl_ref, 4 * GDN_WIDTH, LANES)
    lane = lax.broadcasted_iota(jnp.int32, ab.shape, 1)
    xs = ab + gpar_ref[1:2, :]
    softplus = jnp.maximum(xs, 0.0) + jnp.log1p(jnp.exp(-jnp.abs(xs)))
    g_all = -jnp.exp(gpar_ref[0:1, :]) * softplus
    gb_ref[0] = jnp.where(lane < 2 * HEADS, g_all, jnp.where(lane < 4 * HEADS, _sigmoid(ab), 0.0))

    nt = (((1,), (1,)), ((), ()))
    q_t = lax.dot_general(wqvt_ref[0:D_MODEL, :], h, nt, preferred_element_type=F32)
    cos_t = cost_ref[...]
    sin_t = sint_ref[...]
    half_rows = DIFF_DQK // 4
    for hd in range(HEADS):
        t = q_t[head_cols(hd), :]
        tt = t * t
        r = jnp.concatenate(
            [jnp.broadcast_to(lax.rsqrt(jnp.mean(tt[c * DIFF_DQK:(c + 1) * DIFF_DQK], axis=0, keepdims=True) + EPS),
                              (DIFF_DQK, ROW_TILE)) for c in range(2)], axis=0)
        tn = t * r * qgain_ref[...]
        blocks = [tn[b * half_rows:(b + 1) * half_rows] for b in range(HEAD_W // half_rows)]
        partner = jnp.concatenate([blocks[b ^ 1] for b in range(len(blocks))], axis=0)
        qt_ref[0, head_cols(hd), :] = (tn * cos_t + partner * sin_t).astype(qt_ref.dtype)
    vt_ref[0] = lax.dot_general(wqvt_ref[D_MODEL:2 * D_MODEL, :], h, nt,
                                preferred_element_type=F32).astype(vt_ref.dtype)
    cos = cos_ref[...]
    sin = sin_ref[...]
    res = proj(wk_ref, 0, D_MODEL)
    for hd in range(HEADS):
        kn_ref[0, :, head_cols(hd)] = _qk_norm_rope(
            res[:, head_cols(hd)], gain_ref[1:2, :], cos, sin).astype(kn_ref.dtype)
    zb_ref[0] = proj(wzg_ref, 0, D_MODEL).astype(zb_ref.dtype)
    for c0 in range(0, 2 * D_MODEL, D_MODEL):
        gates_ref[0, :, c0:c0 + D_MODEL] = proj(wzg_ref, D_MODEL + c0, D_MODEL).astype(gates_ref.dtype)


def _input_projection(x, ctx, mod, w_all, w_qv_t, w_k, w_zg, cos_tab, sin_tab, cos_t, sin_t, gains, qgain_t,
                      conv_w8, gpar):
    n_batch, n_lat, _ = x.shape
    n_lat_tiles = n_lat // ROW_TILE
    n_tiles = n_lat_tiles + 1
    n_tok = n_lat + ctx.shape[1]
    last = n_tiles - 1
    now = lambda b, i: (b, jnp.minimum(i, last), 0)
    now_t = lambda b, i: (b, 0, jnp.minimum(i, last))
    trailing = lambda b, i: (b, jnp.maximum(i - 1, 0), 0)
    tok = lambda w, dt: jax.ShapeDtypeStruct((n_batch, n_tok, w), dt)
    feat = jax.ShapeDtypeStruct((n_batch, D_MODEL, n_tok), BF16)
    out_shape = [tok(GDN_WIDTH, BF16)] * 3 + [tok(LANES, F32), tok(GDN_WIDTH, BF16), feat,
                                              tok(D_MODEL, BF16), feat, tok(D_MODEL, BF16),
                                              tok(2 * D_MODEL, BF16)]
    tok_spec = lambda w: pl.BlockSpec((1, ROW_TILE, w), now)
    feat_spec = pl.BlockSpec((1, D_MODEL, ROW_TILE), now_t)
    out_specs = ([pl.BlockSpec((1, ROW_TILE, GDN_WIDTH), trailing)] * 3
                 + [tok_spec(LANES), tok_spec(GDN_WIDTH), feat_spec, tok_spec(D_MODEL), feat_spec,
                    tok_spec(D_MODEL), tok_spec(2 * D_MODEL)])
    resident = lambda a: pl.BlockSpec(a.shape, lambda b, i: (0, 0), pipeline_mode=pl.Buffered(1))
    tab_spec = pl.BlockSpec((ROW_TILE, LANES), lambda b, i: (jnp.minimum(i, last), 0))
    tab_t_spec = pl.BlockSpec((LANES, ROW_TILE), lambda b, i: (0, jnp.minimum(i, last)))
    small = lambda a: pl.BlockSpec(a.shape, lambda b, i: (0, 0))
    return pl.pallas_call(
        functools.partial(_inproj_kernel, n_lat_tiles, n_batch),
        grid=(n_batch, n_tiles + 1),
        in_specs=[
            pl.BlockSpec((1, ROW_TILE, D_MODEL), lambda b, i: (b, jnp.minimum(i, n_lat_tiles - 1), 0)),
            pl.BlockSpec((1, ROW_TILE, D_MODEL), lambda b, i: (b, 0, 0)),
            small(mod),
            pl.BlockSpec((D_MODEL, 4 * GDN_WIDTH + LANES), lambda b, i: (0, 0), pipeline_mode=pl.Buffered(1)),
            resident(w_qv_t), resident(w_k), resident(w_zg),
            tab_spec, tab_spec, tab_t_spec, tab_t_spec, small(gains), small(qgain_t), small(conv_w8), small(gpar),
        ],
        out_specs=out_specs,
        out_shape=out_shape,
        scratch_shapes=[pltpu.VMEM((ROW_TILE + 2 * HALO, 3 * GDN_WIDTH), F32)],
        compiler_params=pltpu.CompilerParams(
            dimension_semantics=("arbitrary", "arbitrary"), vmem_limit_bytes=VMEM_LIMIT),
        name="input_projection",
    )(x, ctx, mod, w_all, w_qv_t, w_k, w_zg, cos_tab, sin_tab, cos_t, sin_t, gains, qgain_t, conv_w8, gpar)


GROUP = 4
GSIZE = GROUP * CHUNK
N_LEVELS = 5
M_EYE, M_PAIR, M_LEVEL0 = 0, 1, 2
M_INCL = (M_LEVEL0 + N_LEVELS, M_LEVEL0 + N_LEVELS + 2)
M_STRICT = (M_LEVEL0 + N_LEVELS + 1, M_LEVEL0 + N_LEVELS + 3)


def _scan_masks():
    r = np.arange(GSIZE)[:, None]
    c = np.arange(GSIZE)[None, :]
    blk = (r // CHUNK) == (c // CHUNK)
    planes = [r == c, (r >> 1) == (c >> 1)]
    for shift in range(1, N_LEVELS + 1):
        planes.append(((r >> (shift + 1)) == (c >> (shift + 1))) & ((r >> shift) != (c >> shift)))
    planes += [blk & (r >= c), blk & (r > c), blk & (r <= c), blk & (r < c)]
    return np.stack(planes).astype(np.float32)


def _gdn_scan_kernel(qf_ref, kf_ref, vf_ref, gbf_ref, qb_ref, kb_ref, vb_ref, gbb_ref, mask_ref,
                     of_ref, ob_ref, s_ref, uw_ref, qd_ref, kdec_ref, qk_ref, glast_ref):
    n_groups = 2 * HEADS // GROUP

    @pl.when(pl.program_id(1) == 0)
    def _():
        s_ref[...] = jnp.zeros_like(s_ref)
        uw_ref[...] = jnp.zeros_like(uw_ref)
        qd_ref[...] = jnp.zeros_like(qd_ref)
        kdec_ref[...] = jnp.zeros_like(kdec_ref)
        qk_ref[...] = jnp.zeros_like(qk_ref)
        glast_ref[...] = jnp.zeros_like(glast_ref)

    rows_of = lambda c: slice(c * CHUNK, (c + 1) * CHUNK)
    stack_rows = lambda parts: jnp.concatenate(parts, axis=0)
    group_heads = lambda gi: [(gi % (HEADS // GROUP)) * GROUP + c for c in range(GROUP)]
    group_lanes = lambda gi: [(gi // (HEADS // GROUP)) * HEADS + h for h in group_heads(gi)]

    state = [s_ref[lane] for lane in range(2 * HEADS)]
    ws_qs = [[_bdot(stack_rows([uw_ref[gi, rows_of(c), HEAD_W:], qd_ref[gi, rows_of(c), :]]), state[lane])
              for c, lane in enumerate(group_lanes(gi))] for gi in range(n_groups)]
    v_new = [[uw_ref[gi, rows_of(c), :HEAD_W] - ws_qs[gi][c][:CHUNK] for c in range(GROUP)]
             for gi in range(n_groups)]
    for gi in range(n_groups):
        for c, lane in enumerate(group_lanes(gi)):
            s_ref[lane] = (state[lane] * glast_ref[lane, 0:1, :]
                           + _bdot_tn(kdec_ref[gi, rows_of(c), :], v_new[gi][c]))
    for gi in range(n_groups):
        o_ref = of_ref if gi < n_groups // 2 else ob_ref
        o = (stack_rows([ws_qs[gi][c][CHUNK:] for c in range(GROUP)])
             + _bdot(qk_ref[gi], stack_rows(v_new[gi])))
        for c, h in enumerate(group_heads(gi)):
            o_ref[0, :, h * HEAD_W:(h + 1) * HEAD_W] = o[rows_of(c)].astype(o_ref.dtype)

    groups = []
    for d, (q_ref, k_ref, v_ref, gb_ref) in enumerate(
            ((qf_ref, kf_ref, vf_ref, gbf_ref), (qb_ref, kb_ref, vb_ref, gbb_ref))):
        last = CHUNK - 1 if d == 0 else 0
        tri = mask_ref[M_INCL[d], 0:CHUNK, 0:CHUNK].astype(BF16)
        gb = gb_ref[0]
        gc = None
        for part in _split3(gb):
            term = jnp.dot(tri, part, preferred_element_type=F32)
            gc = term if gc is None else gc + term
        gc_t = gc.T
        for half in range(HEADS // GROUP):
            gi = d * (HEADS // GROUP) + half
            lanes = group_lanes(gi)
            head_cols = [slice(h * HEAD_W, (h + 1) * HEAD_W) for h in group_heads(gi)]
            groups.append(dict(
                d=d, gi=gi, lanes=lanes,
                q=stack_rows([q_ref[0, :, cs].astype(F32) for cs in head_cols]),
                k=stack_rows([k_ref[0, :, cs].astype(F32) for cs in head_cols]),
                v=stack_rows([v_ref[0, :, cs].astype(F32) for cs in head_cols]),
                g_col=stack_rows([gc[:, l:l + 1] for l in lanes]),
                g_row=jnp.concatenate([gc_t[l:l + 1, :] for l in lanes], axis=1),
                g_last=stack_rows([jnp.broadcast_to(gc[last:last + 1, l:l + 1], (CHUNK, 1)) for l in lanes]),
                g_last_tiles=[jnp.broadcast_to(gc[last:last + 1, l:l + 1], (8, LANES)) for l in lanes],
                beta=stack_rows([gb[:, 2 * HEADS + l:2 * HEADS + l + 1] for l in lanes])))

    for g in groups:
        incl = mask_ref[M_INCL[g["d"]]]
        decay = jnp.exp((g["g_col"] - g["g_row"]) * incl) * incl
        kq = _bdot_nt(stack_rows([g["k"] * g["beta"], g["q"]]), g["k"])
        g["a"] = kq[:GSIZE] * decay * mask_ref[M_STRICT[g["d"]]]
        g["qk"] = kq[GSIZE:] * decay
        g["t"] = mask_ref[M_EYE] - g["a"] * mask_ref[M_PAIR]
    for level in range(N_LEVELS):
        for g in groups:
            g["x"] = _bdot(g["a"] * mask_ref[M_LEVEL0 + level], g["t"])
        for g in groups:
            g["t"] = g["t"] - _bdot(g["t"], g["x"])
    for g in groups:
        gi = g["gi"]
        e_g = jnp.exp(g["g_col"])
        uw_ref[gi] = _bdot(g["t"], jnp.concatenate([g["v"] * g["beta"], g["k"] * (g["beta"] * e_g)], axis=1))
        qd_ref[gi] = g["q"] * e_g
        kdec_ref[gi] = g["k"] * jnp.exp(g["g_last"] - g["g_col"])
        qk_ref[gi] = g["qk"]
        for c, lane in enumerate(g["lanes"]):
            glast_ref[lane] = jnp.exp(g["g_last_tiles"][c])


def _gdn_scan(q, k, v, gb, n_lat):
    n_batch, n_tok, _ = q.shape
    n_lat_chunks = n_lat // CHUNK
    n_ctx_chunks = (n_tok - n_lat) // CHUNK
    n_pos = n_tok // CHUNK

    def fwd_chunk(p):
        return jnp.where(p < n_ctx_chunks, n_lat_chunks + p, p - n_ctx_chunks)

    def bwd_chunk(p):
        return jnp.where(p < n_ctx_chunks, n_lat_chunks + n_ctx_chunks - 1 - p,
                         n_lat_chunks - 1 - (p - n_ctx_chunks))

    prepared = lambda s: jnp.minimum(s, n_pos - 1)
    advanced = lambda s: jnp.maximum(s - 1, 0)
    f_in = lambda w: pl.BlockSpec((1, CHUNK, w), lambda b, s: (b, fwd_chunk(prepared(s)), 0))
    b_in = lambda w: pl.BlockSpec((1, CHUNK, w), lambda b, s: (b, bwd_chunk(prepared(s)), 0))
    f_out = pl.BlockSpec((1, CHUNK, GDN_WIDTH), lambda b, s: (b, fwd_chunk(advanced(s)), 0))
    b_out = pl.BlockSpec((1, CHUNK, GDN_WIDTH), lambda b, s: (b, bwd_chunk(advanced(s)), 0))
    masks = jnp.asarray(_scan_masks())
    n_groups = 2 * HEADS // GROUP
    return pl.pallas_call(
        _gdn_scan_kernel,
        grid=(n_batch, n_pos + 1),
        in_specs=[f_in(GDN_WIDTH)] * 3 + [f_in(LANES)] + [b_in(GDN_WIDTH)] * 3 + [b_in(LANES)]
        + [pl.BlockSpec(masks.shape, lambda b, s: (0, 0, 0))],
        out_specs=[f_out, b_out],
        out_shape=[jax.ShapeDtypeStruct((n_batch, n_tok, GDN_WIDTH), BF16)] * 2,
        scratch_shapes=[pltpu.VMEM((2 * HEADS, HEAD_W, HEAD_W), F32),
                        pltpu.VMEM((n_groups, GSIZE, 2 * HEAD_W), F32),
                        pltpu.VMEM((n_groups, GSIZE, HEAD_W), F32),
                        pltpu.VMEM((n_groups, GSIZE, HEAD_W), F32),
                        pltpu.VMEM((n_groups, GSIZE, GSIZE), F32),
                        pltpu.VMEM((2 * HEADS, 8, LANES), F32)],
        compiler_params=pltpu.CompilerParams(
            dimension_semantics=("arbitrary", "arbitrary"), vmem_limit_bytes=VMEM_LIMIT),
        name="gdn_scan",
    )(q, k, v, gb, q, k, v, gb, masks)


ATT_TQ = 1024
ATT_TK = 768


BOUND_LIMIT = 50.0


def _diff_attn_kernel(n_kv, q_ref, k_ref, vt_ref, z_ref, lam_ref, normw_ref, o_ref, kx_ref, kmax_ref):
    tq = q_ref.shape[2]

    @pl.when(pl.program_id(2) == 0)
    def _():
        lane_k = lax.broadcasted_iota(jnp.int32, (ATT_TK, HEAD_W), 1)
        unit = jnp.where(lane_k == 0, 1.0, 0.0).astype(BF16)

        def widen(j, carry):
            n1, n2 = carry
            rows = pl.ds(pl.multiple_of(j * ATT_TK, ATT_TK), ATT_TK)
            kb = k_ref[0, rows, :]
            kx_ref[rows, 0:HEAD_W] = kb
            kx_ref[rows, HEAD_W:2 * HEAD_W] = unit
            sq = kb.astype(F32) * kb.astype(F32)
            a1 = jnp.sum(jnp.where(lane_k < DIFF_DQK, sq, 0.0), axis=1, keepdims=True)
            a2 = jnp.sum(jnp.where(lane_k < DIFF_DQK, 0.0, sq), axis=1, keepdims=True)
            return (jnp.maximum(n1, jnp.max(a1, axis=0, keepdims=True)),
                    jnp.maximum(n2, jnp.max(a2, axis=0, keepdims=True)))

        zero11 = jnp.zeros((1, 1), F32)
        n1, n2 = lax.fori_loop(0, n_kv, widen, (zero11, zero11))
        kmax_ref[0:1, :] = jnp.broadcast_to(jnp.sqrt(n1), (1, LANES))
        kmax_ref[1:2, :] = jnp.broadcast_to(jnp.sqrt(n2), (1, LANES))

    qt = q_ref[0]
    feat = lax.broadcasted_iota(jnp.int32, (HEAD_W, tq), 0)
    first = feat < DIFF_DQK
    zero = jnp.zeros_like(qt)
    qs_t = jnp.concatenate([jnp.where(first, qt, zero), jnp.where(first, zero, qt)], axis=1)
    sq = qt.astype(F32) * qt.astype(F32)
    c1 = jnp.sqrt(jnp.sum(jnp.where(first, sq, 0.0), axis=0, keepdims=True)) * kmax_ref[0:1, 0:1]
    c2 = jnp.sqrt(jnp.sum(jnp.where(first, 0.0, sq), axis=0, keepdims=True)) * kmax_ref[1:2, 0:1]
    c = jnp.concatenate([c1, c2], axis=1)
    bound = jnp.max(c)

    def shifted_by_bound():
        feat2 = lax.broadcasted_iota(jnp.int32, (HEAD_W, 2 * tq), 0)
        aug = jnp.where(feat2 == 0, -c, 0.0).astype(BF16)
        qx_t = jnp.concatenate([qs_t, aug], axis=0)
        l = jnp.zeros((1, 2 * tq), F32)
        acc = jnp.zeros((HEAD_W, 2 * tq), F32)
        for j in range(n_kv):
            rows = slice(j * ATT_TK, (j + 1) * ATT_TK)
            p = jnp.exp2(jnp.dot(kx_ref[rows, :], qx_t, preferred_element_type=F32))
            l = l + jnp.sum(p, axis=0, keepdims=True)
            acc = acc + jnp.dot(vt_ref[0, :, rows], p.astype(BF16), preferred_element_type=F32)
        return acc * (1.0 / l)

    def running_max():
        m = jnp.full((1, 2 * tq), -jnp.inf, F32)
        l = jnp.zeros((1, 2 * tq), F32)
        acc = jnp.zeros((HEAD_W, 2 * tq), F32)
        for j in range(n_kv):
            rows = slice(j * ATT_TK, (j + 1) * ATT_TK)
            s = jnp.dot(k_ref[0, rows, :], qs_t, preferred_element_type=F32)
            m_new = jnp.maximum(m, jnp.max(s, axis=0, keepdims=True))
            alpha = jnp.exp2(m - m_new)
            p = jnp.exp2(s - m_new)
            l = alpha * l + jnp.sum(p, axis=0, keepdims=True)
            acc = alpha * acc + jnp.dot(vt_ref[0, :, rows], p.astype(BF16), preferred_element_type=F32)
            m = m_new
        return acc * (1.0 / l)

    o_t = lax.cond(bound < BOUND_LIMIT, shifted_by_bound, running_max)
    lam_p = lam_ref[...]
    lam = (jnp.exp(jnp.sum(lam_p[0:1, :] * lam_p[1:2, :], axis=-1, keepdims=True))
           - jnp.exp(jnp.sum(lam_p[2:3, :] * lam_p[3:4, :], axis=-1, keepdims=True)) + LAMBDA_INIT)
    o_t = o_t[:, :tq] - lam * o_t[:, tq:]
    y_t = o_t * lax.rsqrt(jnp.mean(o_t * o_t, axis=0, keepdims=True) + EPS)
    y = y_t.T * normw_ref[0:1, :]
    o_ref[0] = (y * (1.0 - LAMBDA_INIT) * _silu(z_ref[0].astype(F32))).astype(o_ref.dtype)


def _diff_attention(q_t, kn, v_t, z_b, lam_par, norm_w, n_lat):
    n_batch, n_tok, _ = kn.shape
    qt_spec = pl.BlockSpec((1, HEAD_W, ATT_TQ), lambda b, h, i: (b, h, i))
    tok_spec = pl.BlockSpec((1, ATT_TQ, HEAD_W), lambda b, h, i: (b, i, h))
    kv_spec = pl.BlockSpec((1, n_tok, HEAD_W), lambda b, h, i: (b, 0, h))
    par_spec = pl.BlockSpec((8, LANES), lambda b, h, i: (0, 0))
    return pl.pallas_call(
        functools.partial(_diff_attn_kernel, n_tok // ATT_TK),
        grid=(n_batch, HEADS, n_lat // ATT_TQ),
        in_specs=[qt_spec, kv_spec, pl.BlockSpec((1, HEAD_W, n_tok), lambda b, h, i: (b, h, 0)), tok_spec,
                  par_spec, par_spec],
        out_specs=tok_spec,
        out_shape=jax.ShapeDtypeStruct((n_batch, n_lat, HEADS * HEAD_W), BF16),
        scratch_shapes=[pltpu.VMEM((n_tok, 2 * HEAD_W), BF16), pltpu.VMEM((8, LANES), F32)],
        compiler_params=pltpu.CompilerParams(
            dimension_semantics=("arbitrary", "arbitrary", "arbitrary"), vmem_limit_bytes=62 * 1024 * 1024),
        name="diff_attention",
    )(q_t, kn, v_t, z_b, lam_par, norm_w)


MIX_HEADS_PER_DOT = 2


def _mixer_out_kernel(x_ref, of_ref, ob_ref, za_ref, yb_ref, gates_ref, mod_ref, normw_ref,
                      woa_ref, wob_ref, wout_ref, o_ref):
    b = pl.program_id(0)
    proj_a = jnp.dot(yb_ref[0], wob_ref[...], preferred_element_type=F32) * _sigmoid(
        gates_ref[0, :, D_MODEL:].astype(F32))
    acc = None
    for h0 in range(0, HEADS, MIX_HEADS_PER_DOT):
        parts = []
        for h in range(h0, h0 + MIX_HEADS_PER_DOT):
            cols = slice(h * HEAD_W, (h + 1) * HEAD_W)
            o_a = of_ref[0, :, cols].astype(F32) + ob_ref[0, :, cols].astype(F32)
            y = o_a * lax.rsqrt(jnp.mean(o_a * o_a, axis=-1, keepdims=True) + EPS) * normw_ref[0:1, :]
            parts.append((y * _silu(za_ref[0, :, cols].astype(F32))).astype(BF16))
        term = jnp.dot(jnp.concatenate(parts, axis=1), woa_ref[h0 * HEAD_W:(h0 + MIX_HEADS_PER_DOT) * HEAD_W, :],
                       preferred_element_type=F32)
        acc = term if acc is None else acc + term
    merged = _sigmoid(gates_ref[0, :, :D_MODEL].astype(F32)) * acc + proj_a
    out = jnp.dot(merged.astype(BF16), wout_ref[...], preferred_element_type=F32)
    gate = mod_ref[pl.ds(b, 1), 2 * D_MODEL:3 * D_MODEL]
    o_ref[0] = x_ref[0] + gate * out


def _mixer_output(x, o_f, o_b, z_a, y_b, gates, mod, gdn_norm_w, w_oa, w_ob, w_out):
    n_batch, n_lat, _ = x.shape
    tok_spec = lambda w: pl.BlockSpec((1, ROW_TILE, w), lambda b, i: (b, i, 0))
    w_spec = pl.BlockSpec((D_MODEL, D_MODEL), lambda b, i: (0, 0))
    return pl.pallas_call(
        _mixer_out_kernel,
        grid=(n_batch, n_lat // ROW_TILE),
        in_specs=[tok_spec(D_MODEL), tok_spec(GDN_WIDTH), tok_spec(GDN_WIDTH), tok_spec(GDN_WIDTH),
                  tok_spec(D_MODEL), tok_spec(2 * D_MODEL),
                  pl.BlockSpec((8, 3 * D_MODEL), lambda b, i: (0, 0)),
                  pl.BlockSpec((8, LANES), lambda b, i: (0, 0)),
                  w_spec, w_spec, w_spec],
        out_specs=tok_spec(D_MODEL),
        out_shape=jax.ShapeDtypeStruct(x.shape, F32),
        compiler_params=pltpu.CompilerParams(
            dimension_semantics=("arbitrary", "arbitrary"), vmem_limit_bytes=VMEM_LIMIT),
        name="mixer_output",
    )(x, o_f, o_b, z_a, y_b, gates, mod, gdn_norm_w, w_oa, w_ob, w_out)


def _pad_rows(a, rows=8):
    return jnp.pad(a, ((0, rows - a.shape[0]), (0, 0)))


def _pad_lanes(a, lanes=LANES):
    return jnp.pad(a, ((0, 0), (0, lanes - a.shape[1])))


def _rope_tables(n_lat, n_ctx):
    rows = n_lat // GRID_W
    row = np.broadcast_to(np.arange(rows)[:, None], (rows, GRID_W)).reshape(-1).astype(np.float32)
    col = np.broadcast_to(np.arange(GRID_W)[None, :], (rows, GRID_W)).reshape(-1).astype(np.float32)
    n_freq = DIFF_DQK // 4
    inv_freq = (np.float32(ROPE_THETA) ** (-np.arange(n_freq, dtype=np.float32) / np.float32(n_freq))).astype(np.float32)
    ang_r = row[:, None] * inv_freq
    ang_c = col[:, None] * inv_freq
    ang = np.concatenate([ang_r, ang_r, ang_c, ang_c] * 2, axis=-1).astype(np.float32)
    cos = np.concatenate([np.cos(ang), np.ones((n_ctx, LANES), np.float32)], axis=0)
    sin = np.concatenate([np.sin(ang), np.zeros((n_ctx, LANES), np.float32)], axis=0)
    sign = np.where(np.arange(LANES) % (DIFF_DQK // 2) < DIFF_DQK // 4, -1.0, 1.0).astype(np.float32)
    return (jnp.asarray(cos, F32), jnp.asarray(sin, F32),
            jnp.asarray(np.ascontiguousarray(cos.T), F32), jnp.asarray(np.ascontiguousarray((sin * sign).T), F32))


def kernel(x, c, ctx, c_ctx, w_ada, b_ada, w_in, conv_w, a_log, dt_bias, gdn_norm_w, q_norm_w, k_norm_w,
           lambda_q1, lambda_k1, lambda_q2, lambda_k2, diff_norm_w, w_oa, w_ob, w_out):
    assert w_ada.shape[0] == 1, "single-layer block"
    n_batch, n_lat, _ = x.shape
    n_ctx = ctx.shape[1]
    assert n_ctx == ROW_TILE and n_lat % ROW_TILE == 0 and n_batch < 8

    bounds = np.cumsum((0,) + IN_SIZES)
    assert bounds[2] == 4 * GDN_WIDTH and bounds[4] - bounds[2] <= LANES
    w_all = w_in[0].astype(BF16)
    piece = lambda j0, j1: w_all[:, bounds[j0]:bounds[j1]]
    w_qv_t = jnp.concatenate([piece(4, 5).T, piece(6, 7).T], axis=0)
    w_k = piece(5, 6)
    w_zg = piece(7, 9)
    cc = _pad_rows(jnp.concatenate([c, c_ctx[None, :]], axis=0))
    conv_w8 = _pad_rows(conv_w[0])
    gpar = _pad_rows(_pad_lanes(jnp.stack([a_log[0].reshape(-1), dt_bias[0].reshape(-1)])))
    gains = _pad_rows(jnp.stack([jnp.tile(q_norm_w[0], 2), jnp.tile(k_norm_w[0], 2)]))
    qgain_t = jnp.broadcast_to((jnp.tile(q_norm_w[0], 2) * (DIFF_DQK ** -0.5 * LOG2E))[:, None],
                               (HEAD_W, ROW_TILE))
    lam_par = _pad_rows(_pad_lanes(jnp.stack([lambda_q1[0], lambda_k1[0], lambda_q2[0], lambda_k2[0]])))
    cos_tab, sin_tab, cos_t, sin_t = _rope_tables(n_lat, n_ctx)

    mod = _ada_modulation(cc, w_ada[0], b_ada)
    q_a, k_a, v_a, gb, z_a, q_t, kn, v_t, z_b, gates = _input_projection(
        x, ctx, mod, w_all, w_qv_t, w_k, w_zg, cos_tab, sin_tab, cos_t, sin_t, gains, qgain_t, conv_w8, gpar)
    o_f, o_b = _gdn_scan(q_a, k_a, v_a, gb, n_lat)
    y_b = _diff_attention(q_t, kn, v_t, z_b, lam_par, _pad_rows(diff_norm_w), n_lat)
    return _mixer_output(x, o_f, o_b, z_a, y_b, gates, mod, _pad_rows(gdn_norm_w),
                         w_oa[0].astype(BF16), w_ob[0].astype(BF16), w_out[0].astype(BF16))
```

```python
import functools
import math

import jax
import jax.numpy as jnp
import numpy as np
from jax import lax
from jax.experimental import pallas as pl
from jax.experimental.pallas import tpu as pltpu

F32 = jnp.float32
BF16 = jnp.bfloat16

D_MODEL = 1024
GRID_W = 64
EPS = 1e-6
HEADS = 8
HEAD_W = 128
GDN_WIDTH = HEADS * HEAD_W
CONV_K = 5
CHUNK = 64
DIFF_DQK = 64
ROPE_THETA = 10000.0
LAMBDA_INIT = 0.8 - 0.6 * math.exp(-0.3 * 0)
IN_SIZES = (3 * GDN_WIDTH, GDN_WIDTH, 2 * HEADS, 2 * HEADS, 1024, 1024, 1024, 1024, 2 * D_MODEL)

ROW_TILE = 256
HALO = 16
LANES = 128
VMEM_LIMIT = 56 * 1024 * 1024


def _bdot(a, b):
    return jnp.dot(a.astype(BF16), b.astype(BF16), preferred_element_type=F32)


def _bdot_nt(a, b):
    return lax.dot_general(a.astype(BF16), b.astype(BF16), (((1,), (1,)), ((), ())),
                           preferred_element_type=F32)


def _bdot_tn(a, b):
    return lax.dot_general(a.astype(BF16), b.astype(BF16), (((0,), (0,)), ((), ())),
                           preferred_element_type=F32)


def _silu(x):
    return x * (1.0 / (1.0 + jnp.exp(-x)))


def _sigmoid(x):
    return 1.0 / (1.0 + jnp.exp(-x))


def _split3(x):
    hi = x.astype(BF16)
    r1 = x - hi.astype(F32)
    mid = r1.astype(BF16)
    lo = (r1 - mid.astype(F32)).astype(BF16)
    return hi, mid, lo


def _ada_kernel(c_ref, w_ref, b_ref, o_ref):
    a = _split3(_silu(c_ref[...]))
    w = _split3(w_ref[...])
    acc = b_ref[...]
    for i, j in ((2, 0), (1, 1), (0, 2), (1, 0), (0, 1), (0, 0)):
        acc = acc + jnp.dot(a[i], w[j], preferred_element_type=F32)
    o_ref[...] = acc


def _ada_modulation(cc, w_ada, b_ada):
    n_col = w_ada.shape[1] // D_MODEL
    return pl.pallas_call(
        _ada_kernel,
        grid=(n_col,),
        in_specs=[pl.BlockSpec((8, D_MODEL), lambda j: (0, 0)),
                  pl.BlockSpec((D_MODEL, D_MODEL), lambda j: (0, j)),
                  pl.BlockSpec((1, D_MODEL), lambda j: (0, j))],
        out_specs=pl.BlockSpec((8, D_MODEL), lambda j: (0, j)),
        out_shape=jax.ShapeDtypeStruct((8, w_ada.shape[1]), F32),
        name="ada_modulation",
    )(cc, w_ada, b_ada)


LOG2E = 1.4426950408889634


def _qk_norm_rope(x, gain, cos, sin):
    lane = lax.broadcasted_iota(jnp.int32, x.shape, 1)
    first_comp = lane < DIFF_DQK
    first_half = (lane % (DIFF_DQK // 2)) < (DIFF_DQK // 4)
    xx = x * x
    s1 = jnp.sum(jnp.where(first_comp, xx, 0.0), axis=-1, keepdims=True)
    s2 = jnp.sum(jnp.where(first_comp, 0.0, xx), axis=-1, keepdims=True)
    ms = jnp.where(first_comp, s1, s2) * (1.0 / DIFF_DQK)
    t = x * lax.rsqrt(ms + EPS) * gain
    rot = jnp.where(first_half, -pltpu.roll(t, LANES - DIFF_DQK // 4, 1), pltpu.roll(t, DIFF_DQK // 4, 1))
    return t * cos + rot * sin


def _inproj_kernel(n_lat_tiles, n_batch, x_ref, ctx_ref, mod_ref, wl_ref, wqvt_ref, wk_ref, wzg_ref,
                   cos_ref, sin_ref, cost_ref, sint_ref, gain_ref, qgain_ref, convw_ref, gpar_ref,
                   qa_ref, ka_ref, va_ref, gb_ref, za_ref, qt_ref, kn_ref, vt_ref, zb_ref, gates_ref, xp_ref):
    b = pl.program_id(0)
    i = pl.program_id(1)

    @pl.when(i == 0)
    def _():
        xp_ref[...] = jnp.zeros_like(xp_ref)

    is_ctx = i >= n_lat_tiles
    xt = jnp.where(is_ctx, ctx_ref[0], x_ref[0])
    mod = mod_ref[pl.ds(jnp.where(is_ctx, n_batch, b), 1), :]
    shift = mod[:, 0:D_MODEL]
    scale = mod[:, D_MODEL:2 * D_MODEL]
    ms = jnp.mean(xt * xt, axis=-1, keepdims=True)
    h = (xt * lax.rsqrt(ms + EPS) * (1.0 + scale) + shift).astype(BF16)
    proj = lambda w_ref, c0, cw: jnp.dot(h, w_ref[:, c0:c0 + cw], preferred_element_type=F32)
    head_cols = lambda hd: slice(hd * HEAD_W, (hd + 1) * HEAD_W)

    cur_ends_seq = (i == n_lat_tiles) | (i == n_lat_tiles + 1)
    next_starts_seq = i == n_lat_tiles
    pad = CONV_K // 2
    for which, o_ref in enumerate((qa_ref, ka_ref, va_ref)):
        res = proj(wl_ref, which * GDN_WIDTH, GDN_WIDTH)
        for hd in range(HEADS):
            cols = slice(which * GDN_WIDTH + hd * HEAD_W, which * GDN_WIDTH + (hd + 1) * HEAD_W)
            xp_ref[HALO + ROW_TILE:, cols] = jnp.where(cur_ends_seq, 0.0, res[0:HALO, head_cols(hd)])
            slab = xp_ref[:, cols].reshape(-1, 8, HEAD_W)
            sub = lax.broadcasted_iota(jnp.int32, slab.shape, 1)
            y = None
            for j in range(CONV_K):
                s = pad - j
                if s == 0:
                    shifted = slab
                else:
                    rot = pltpu.roll(slab, s % 8, 1)
                    if s > 0:
                        shifted = jnp.where(sub >= s, rot, jnp.concatenate([rot[-1:], rot[:-1]], axis=0))
                    else:
                        shifted = jnp.where(sub < 8 + s, rot, jnp.concatenate([rot[1:], rot[:1]], axis=0))
                body = shifted[HALO // 8:(HALO + ROW_TILE) // 8].reshape(ROW_TILE, HEAD_W)
                term = body * convw_ref[j:j + 1, cols]
                y = term if y is None else y + term
            y = _silu(y)
            if which < 2:
                y = y * lax.rsqrt(jnp.sum(y * y, axis=-1, keepdims=True) + EPS)
            if which == 0:
                y = y * (HEAD_W ** -0.5)
            o_ref[0, :, head_cols(hd)] = y.astype(o_ref.dtype)
            xp_ref[0:HALO, cols] = jnp.where(next_starts_seq, 0.0, xp_ref[ROW_TILE:ROW_TILE + HALO, cols])
            xp_ref[HALO:HALO + ROW_TILE, cols] = res[:, head_cols(hd)]
    za_ref[0] = proj(wl_ref, 3 * GDN_WIDTH, GDN_WIDTH).astype(za_ref.dtype)

    ab = proj(wl_ref, 4 * GDN_WIDTH, LANES)
    lane = lax.broadcasted_iota(jnp.int32, ab.shape, 1)
    xs = ab + gpar_ref[1:2, :]
    softplus = jnp.maximum(xs, 0.0) + jnp.log1p(jnp.exp(-jnp.abs(xs)))
    g_all = -jnp.exp(gpar_ref[0:1, :]) * softplus
    gb_ref[0] = jnp.where(lane < 2 * HEADS, g_all, jnp.where(lane < 4 * HEADS, _sigmoid(ab), 0.0))

    nt = (((1,), (1,)), ((), ()))
    q_t = lax.dot_general(wqvt_ref[0:D_MODEL, :], h, nt, preferred_element_type=F32)
    cos_t = cost_ref[...]
    sin_t = sint_ref[...]
    half_rows = DIFF_DQK // 4
    for hd in range(HEADS):
        t = q_t[head_cols(hd), :]
        tt = t * t
        r = jnp.concatenate(
            [jnp.broadcast_to(lax.rsqrt(jnp.mean(tt[c * DIFF_DQK:(c + 1) * DIFF_DQK], axis=0, keepdims=True) + EPS),
                              (DIFF_DQK, ROW_TILE)) for c in range(2)], axis=0)
        tn = t * r * qgain_ref[...]
        blocks = [tn[b * half_rows:(b + 1) * half_rows] for b in range(HEAD_W // half_rows)]
        partner = jnp.concatenate([blocks[b ^ 1] for b in range(len(blocks))], axis=0)
        qt_ref[0, head_cols(hd), :] = (tn * cos_t + partner * sin_t).astype(qt_ref.dtype)
    vt_ref[0] = lax.dot_general(wqvt_ref[D_MODEL:2 * D_MODEL, :], h, nt,
                                preferred_element_type=F32).astype(vt_ref.dtype)
    cos = cos_ref[...]
    sin = sin_ref[...]
    res = proj(wk_ref, 0, D_MODEL)
    for hd in range(HEADS):
        kn_ref[0, :, head_cols(hd)] = _qk_norm_rope(
            res[:, head_cols(hd)], gain_ref[1:2, :], cos, sin).astype(kn_ref.dtype)
    zb_ref[0] = proj(wzg_ref, 0, D_MODEL).astype(zb_ref.dtype)
    for c0 in range(0, 2 * D_MODEL, D_MODEL):
        gates_ref[0, :, c0:c0 + D_MODEL] = proj(wzg_ref, D_MODEL + c0, D_MODEL).astype(gates_ref.dtype)


def _input_projection(x, ctx, mod, w_all, w_qv_t, w_k, w_zg, cos_tab, sin_tab, cos_t, sin_t, gains, qgain_t,
                      conv_w8, gpar):
    n_batch, n_lat, _ = x.shape
    n_lat_tiles = n_lat // ROW_TILE
    n_tiles = n_lat_tiles + 1
    n_tok = n_lat + ctx.shape[1]
    last = n_tiles - 1
    now = lambda b, i: (b, jnp.minimum(i, last), 0)
    now_t = lambda b, i: (b, 0, jnp.minimum(i, last))
    trailing = lambda b, i: (b, jnp.maximum(i - 1, 0), 0)
    tok = lambda w, dt: jax.ShapeDtypeStruct((n_batch, n_tok, w), dt)
    feat = jax.ShapeDtypeStruct((n_batch, D_MODEL, n_tok), BF16)
    out_shape = [tok(GDN_WIDTH, BF16)] * 3 + [tok(LANES, F32), tok(GDN_WIDTH, BF16), feat,
                                              tok(D_MODEL, BF16), feat, tok(D_MODEL, BF16),
                                              tok(2 * D_MODEL, BF16)]
    tok_spec = lambda w: pl.BlockSpec((1, ROW_TILE, w), now)
    feat_spec = pl.BlockSpec((1, D_MODEL, ROW_TILE), now_t)
    out_specs = ([pl.BlockSpec((1, ROW_TILE, GDN_WIDTH), trailing)] * 3
                 + [tok_spec(LANES), tok_spec(GDN_WIDTH), feat_spec, tok_spec(D_MODEL), feat_spec,
                    tok_spec(D_MODEL), tok_spec(2 * D_MODEL)])
    resident = lambda a: pl.BlockSpec(a.shape, lambda b, i: (0, 0), pipeline_mode=pl.Buffered(1))
    tab_spec = pl.BlockSpec((ROW_TILE, LANES), lambda b, i: (jnp.minimum(i, last), 0))
    tab_t_spec = pl.BlockSpec((LANES, ROW_TILE), lambda b, i: (0, jnp.minimum(i, last)))
    small = lambda a: pl.BlockSpec(a.shape, lambda b, i: (0, 0))
    return pl.pallas_call(
        functools.partial(_inproj_kernel, n_lat_tiles, n_batch),
        grid=(n_batch, n_tiles + 1),
        in_specs=[
            pl.BlockSpec((1, ROW_TILE, D_MODEL), lambda b, i: (b, jnp.minimum(i, n_lat_tiles - 1), 0)),
            pl.BlockSpec((1, ROW_TILE, D_MODEL), lambda b, i: (b, 0, 0)),
            small(mod),
            pl.BlockSpec((D_MODEL, 4 * GDN_WIDTH + LANES), lambda b, i: (0, 0), pipeline_mode=pl.Buffered(1)),
            resident(w_qv_t), resident(w_k), resident(w_zg),
            tab_spec, tab_spec, tab_t_spec, tab_t_spec, small(gains), small(qgain_t), small(conv_w8), small(gpar),
        ],
        out_specs=out_specs,
        out_shape=out_shape,
        scratch_shapes=[pltpu.VMEM((ROW_TILE + 2 * HALO, 3 * GDN_WIDTH), F32)],
        compiler_params=pltpu.CompilerParams(
            dimension_semantics=("arbitrary", "arbitrary"), vmem_limit_bytes=VMEM_LIMIT),
        name="input_projection",
    )(x, ctx, mod, w_all, w_qv_t, w_k, w_zg, cos_tab, sin_tab, cos_t, sin_t, gains, qgain_t, conv_w8, gpar)


GROUP = 4
GSIZE = GROUP * CHUNK
N_LEVELS = 5
M_EYE, M_PAIR, M_LEVEL0 = 0, 1, 2
M_INCL = (M_LEVEL0 + N_LEVELS, M_LEVEL0 + N_LEVELS + 2)
M_STRICT = (M_LEVEL0 + N_LEVELS + 1, M_LEVEL0 + N_LEVELS + 3)


def _scan_masks():
    r = np.arange(GSIZE)[:, None]
    c = np.arange(GSIZE)[None, :]
    blk = (r // CHUNK) == (c // CHUNK)
    planes = [r == c, (r >> 1) == (c >> 1)]
    for shift in range(1, N_LEVELS + 1):
        planes.append(((r >> (shift + 1)) == (c >> (shift + 1))) & ((r >> shift) != (c >> shift)))
    planes += [blk & (r >= c), blk & (r > c), blk & (r <= c), blk & (r < c)]
    return np.stack(planes).astype(np.float32)


def _gdn_scan_kernel(qf_ref, kf_ref, vf_ref, gbf_ref, qb_ref, kb_ref, vb_ref, gbb_ref, mask_ref,
                     of_ref, ob_ref, s_ref):
    @pl.when(pl.program_id(1) == 0)
    def _():
        s_ref[...] = jnp.zeros_like(s_ref)

    rows_of = lambda c: slice(c * CHUNK, (c + 1) * CHUNK)
    stack_rows = lambda parts: jnp.concatenate(parts, axis=0)

    groups = []
    for d, (q_ref, k_ref, v_ref, gb_ref, o_ref) in enumerate(
            ((qf_ref, kf_ref, vf_ref, gbf_ref, of_ref), (qb_ref, kb_ref, vb_ref, gbb_ref, ob_ref))):
        last = CHUNK - 1 if d == 0 else 0
        tri = mask_ref[M_INCL[d], 0:CHUNK, 0:CHUNK].astype(BF16)
        gb = gb_ref[0]
        gc = None
        for part in _split3(gb):
            term = jnp.dot(tri, part, preferred_element_type=F32)
            gc = term if gc is None else gc + term
        gc_t = gc.T
        for half in range(HEADS // GROUP):
            heads = [half * GROUP + c for c in range(GROUP)]
            lanes = [d * HEADS + h for h in heads]
            head_cols = [slice(h * HEAD_W, (h + 1) * HEAD_W) for h in heads]
            groups.append(dict(
                d=d, lanes=lanes, head_cols=head_cols, o_ref=o_ref,
                q=stack_rows([q_ref[0, :, cs].astype(F32) for cs in head_cols]),
                k=stack_rows([k_ref[0, :, cs].astype(F32) for cs in head_cols]),
                v=stack_rows([v_ref[0, :, cs].astype(F32) for cs in head_cols]),
                g_col=stack_rows([gc[:, l:l + 1] for l in lanes]),
                g_row=jnp.concatenate([gc_t[l:l + 1, :] for l in lanes], axis=1),
                g_last=stack_rows([jnp.broadcast_to(gc[last:last + 1, l:l + 1], (CHUNK, 1)) for l in lanes]),
                beta=stack_rows([gb[:, 2 * HEADS + l:2 * HEADS + l + 1] for l in lanes])))

    for g in groups:
        incl = mask_ref[M_INCL[g["d"]]]
        decay = jnp.exp((g["g_col"] - g["g_row"]) * incl) * incl
        kq = _bdot_nt(stack_rows([g["k"] * g["beta"], g["q"]]), g["k"])
        g["a"] = kq[:GSIZE] * decay * mask_ref[M_STRICT[g["d"]]]
        g["qk"] = kq[GSIZE:] * decay
        g["t"] = mask_ref[M_EYE] - g["a"] * mask_ref[M_PAIR]
    for level in range(N_LEVELS):
        for g in groups:
            g["x"] = _bdot(g["a"] * mask_ref[M_LEVEL0 + level], g["t"])
        for g in groups:
            g["t"] = g["t"] - _bdot(g["t"], g["x"])
    for g in groups:
        e_g = jnp.exp(g["g_col"])
        uw = _bdot(g["t"], jnp.concatenate([g["v"] * g["beta"], g["k"] * (g["beta"] * e_g)], axis=1))
        g["u"] = uw[:, :HEAD_W]
        g["w"] = uw[:, HEAD_W:]
        g["qd"] = g["q"] * e_g
        g["k_dec"] = g["k"] * jnp.exp(g["g_last"] - g["g_col"])

    state = [s_ref[lane] for lane in range(2 * HEADS)]
    for g in groups:
        g["ws_qs"] = [_bdot(stack_rows([g["w"][rows_of(c)], g["qd"][rows_of(c)]]), state[lane])
                      for c, lane in enumerate(g["lanes"])]
    for g in groups:
        g["v_new"] = [g["u"][rows_of(c)] - g["ws_qs"][c][:CHUNK] for c in range(GROUP)]
    for g in groups:
        for c, lane in enumerate(g["lanes"]):
            s_ref[lane] = (state[lane] * jnp.exp(g["g_last"][c * CHUNK:c * CHUNK + 1, :])
                           + _bdot_tn(g["k_dec"][rows_of(c)], g["v_new"][c]))
    for g in groups:
        o = (stack_rows([g["ws_qs"][c][CHUNK:] for c in range(GROUP)])
             + _bdot(g["qk"], stack_rows(g["v_new"])))
        for c, cs in enumerate(g["head_cols"]):
            g["o_ref"][0, :, cs] = o[rows_of(c)].astype(g["o_ref"].dtype)


def _gdn_scan(q, k, v, gb, n_lat):
    n_batch, n_tok, _ = q.shape
    n_lat_chunks = n_lat // CHUNK
    n_ctx_chunks = (n_tok - n_lat) // CHUNK
    n_steps = n_tok // CHUNK

    def fwd_chunk(s):
        return jnp.where(s < n_ctx_chunks, n_lat_chunks + s, s - n_ctx_chunks)

    def bwd_chunk(s):
        return jnp.where(s < n_ctx_chunks, n_lat_chunks + n_ctx_chunks - 1 - s,
                         n_lat_chunks - 1 - (s - n_ctx_chunks))

    f_spec = lambda w: pl.BlockSpec((1, CHUNK, w), lambda b, s: (b, fwd_chunk(s), 0))
    b_spec = lambda w: pl.BlockSpec((1, CHUNK, w), lambda b, s: (b, bwd_chunk(s), 0))
    masks = jnp.asarray(_scan_masks())
    return pl.pallas_call(
        _gdn_scan_kernel,
        grid=(n_batch, n_steps),
        in_specs=[f_spec(GDN_WIDTH)] * 3 + [f_spec(LANES)] + [b_spec(GDN_WIDTH)] * 3 + [b_spec(LANES)]
        + [pl.BlockSpec(masks.shape, lambda b, s: (0, 0, 0))],
        out_specs=[f_spec(GDN_WIDTH), b_spec(GDN_WIDTH)],
        out_shape=[jax.ShapeDtypeStruct((n_batch, n_tok, GDN_WIDTH), BF16)] * 2,
        scratch_shapes=[pltpu.VMEM((2 * HEADS, HEAD_W, HEAD_W), F32)],
        compiler_params=pltpu.CompilerParams(
            dimension_semantics=("arbitrary", "arbitrary"), vmem_limit_bytes=VMEM_LIMIT),
        name="gdn_scan",
    )(q, k, v, gb, q, k, v, gb, masks)


ATT_TQ = 1024
ATT_TK = 768
ATT_VMEM_LIMIT = 62 * 1024 * 1024


BOUND_LIMIT = 50.0


def _diff_attn_kernel(n_kv, q_ref, k_ref, vt_ref, z_ref, lam_ref, normw_ref, o_ref, kx_ref, kmax_ref):
    tq = q_ref.shape[2]

    @pl.when(pl.program_id(2) == 0)
    def _():
        lane_k = lax.broadcasted_iota(jnp.int32, (ATT_TK, HEAD_W), 1)
        unit = jnp.where(lane_k == 0, 1.0, 0.0).astype(BF16)

        def widen(j, carry):
            n1, n2 = carry
            rows = pl.ds(pl.multiple_of(j * ATT_TK, ATT_TK), ATT_TK)
            kb = k_ref[0, rows, :]
            kx_ref[rows, 0:HEAD_W] = kb
            kx_ref[rows, HEAD_W:2 * HEAD_W] = unit
            sq = kb.astype(F32) * kb.astype(F32)
            a1 = jnp.sum(jnp.where(lane_k < DIFF_DQK, sq, 0.0), axis=1, keepdims=True)
            a2 = jnp.sum(jnp.where(lane_k < DIFF_DQK, 0.0, sq), axis=1, keepdims=True)
            return (jnp.maximum(n1, jnp.max(a1, axis=0, keepdims=True)),
                    jnp.maximum(n2, jnp.max(a2, axis=0, keepdims=True)))

        zero11 = jnp.zeros((1, 1), F32)
        n1, n2 = lax.fori_loop(0, n_kv, widen, (zero11, zero11))
        kmax_ref[0:1, :] = jnp.broadcast_to(jnp.sqrt(n1), (1, LANES))
        kmax_ref[1:2, :] = jnp.broadcast_to(jnp.sqrt(n2), (1, LANES))

    qt = q_ref[0]
    feat = lax.broadcasted_iota(jnp.int32, (HEAD_W, tq), 0)
    first = feat < DIFF_DQK
    zero = jnp.zeros_like(qt)
    qs_t = jnp.concatenate([jnp.where(first, qt, zero), jnp.where(first, zero, qt)], axis=1)
    sq = qt.astype(F32) * qt.astype(F32)
    c1 = jnp.sqrt(jnp.sum(jnp.where(first, sq, 0.0), axis=0, keepdims=True)) * kmax_ref[0:1, 0:1]
    c2 = jnp.sqrt(jnp.sum(jnp.where(first, 0.0, sq), axis=0, keepdims=True)) * kmax_ref[1:2, 0:1]
    c = jnp.concatenate([c1, c2], axis=1)
    bound = jnp.max(c)

    def shifted_by_bound():
        feat2 = lax.broadcasted_iota(jnp.int32, (HEAD_W, 2 * tq), 0)
        aug = jnp.where(feat2 == 0, -c, 0.0).astype(BF16)
        qx_t = jnp.concatenate([qs_t, aug], axis=0)
        l = jnp.zeros((1, 2 * tq), F32)
        acc = jnp.zeros((HEAD_W, 2 * tq), F32)
        for j in range(n_kv):
            rows = slice(j * ATT_TK, (j + 1) * ATT_TK)
            p = jnp.exp2(jnp.dot(kx_ref[rows, :], qx_t, preferred_element_type=F32))
            l = l + jnp.sum(p, axis=0, keepdims=True)
            acc = acc + jnp.dot(vt_ref[0, :, rows], p.astype(BF16), preferred_element_type=F32)
        return acc / l

    def running_max():
        m = jnp.full((1, 2 * tq), -jnp.inf, F32)
        l = jnp.zeros((1, 2 * tq), F32)
        acc = jnp.zeros((HEAD_W, 2 * tq), F32)
        for j in range(n_kv):
            rows = slice(j * ATT_TK, (j + 1) * ATT_TK)
            s = jnp.dot(k_ref[0, rows, :], qs_t, preferred_element_type=F32)
            m_new = jnp.maximum(m, jnp.max(s, axis=0, keepdims=True))
            alpha = jnp.exp2(m - m_new)
            p = jnp.exp2(s - m_new)
            l = alpha * l + jnp.sum(p, axis=0, keepdims=True)
            acc = alpha * acc + jnp.dot(vt_ref[0, :, rows], p.astype(BF16), preferred_element_type=F32)
            m = m_new
        return acc / l

    o_t = lax.cond(bound < BOUND_LIMIT, shifted_by_bound, running_max)
    lam_p = lam_ref[...]
    lam = (jnp.exp(jnp.sum(lam_p[0:1, :] * lam_p[1:2, :], axis=-1, keepdims=True))
           - jnp.exp(jnp.sum(lam_p[2:3, :] * lam_p[3:4, :], axis=-1, keepdims=True)) + LAMBDA_INIT)
    o_t = o_t[:, :tq] - lam * o_t[:, tq:]
    y_t = o_t * lax.rsqrt(jnp.mean(o_t * o_t, axis=0, keepdims=True) + EPS)
    y = y_t.T * normw_ref[0:1, :]
    o_ref[0] = (y * (1.0 - LAMBDA_INIT) * _silu(z_ref[0].astype(F32))).astype(o_ref.dtype)


def _diff_attention(q_t, kn, v_t, z_b, lam_par, norm_w, n_lat):
    n_batch, n_tok, _ = kn.shape
    assert n_lat % ATT_TQ == 0 and n_tok % ATT_TK == 0
    qt_spec = pl.BlockSpec((1, HEAD_W, ATT_TQ), lambda b, h, i: (b, h, i))
    tok_spec = pl.BlockSpec((1, ATT_TQ, HEAD_W), lambda b, h, i: (b, i, h))
    kv_spec = pl.BlockSpec((1, n_tok, HEAD_W), lambda b, h, i: (b, 0, h))
    par_spec = pl.BlockSpec((8, LANES), lambda b, h, i: (0, 0))
    return pl.pallas_call(
        functools.partial(_diff_attn_kernel, n_tok // ATT_TK),
        grid=(n_batch, HEADS, n_lat // ATT_TQ),
        in_specs=[qt_spec, kv_spec, pl.BlockSpec((1, HEAD_W, n_tok), lambda b, h, i: (b, h, 0)), tok_spec,
                  par_spec, par_spec],
        out_specs=tok_spec,
        out_shape=jax.ShapeDtypeStruct((n_batch, n_lat, HEADS * HEAD_W), BF16),
        scratch_shapes=[pltpu.VMEM((n_tok, 2 * HEAD_W), BF16), pltpu.VMEM((8, LANES), F32)],
        compiler_params=pltpu.CompilerParams(
            dimension_semantics=("arbitrary", "arbitrary", "arbitrary"), vmem_limit_bytes=ATT_VMEM_LIMIT),
        name="diff_attention",
    )(q_t, kn, v_t, z_b, lam_par, norm_w)


def _mixer_out_kernel(x_ref, of_ref, ob_ref, za_ref, yb_ref, gates_ref, mod_ref, normw_ref,
                      woa_ref, wob_ref, wout_ref, o_ref, ya_ref):
    b = pl.program_id(0)
    for h in range(HEADS):
        cols = slice(h * HEAD_W, (h + 1) * HEAD_W)
        o_a = of_ref[0, :, cols].astype(F32) + ob_ref[0, :, cols].astype(F32)
        y = o_a * lax.rsqrt(jnp.mean(o_a * o_a, axis=-1, keepdims=True) + EPS) * normw_ref[0:1, :]
        ya_ref[:, cols] = (y * _silu(za_ref[0, :, cols].astype(F32))).astype(BF16)
    gates = _sigmoid(gates_ref[0].astype(F32))
    merged = (gates[:, :D_MODEL] * jnp.dot(ya_ref[...], woa_ref[...], preferred_element_type=F32)
              + gates[:, D_MODEL:] * jnp.dot(yb_ref[0], wob_ref[...], preferred_element_type=F32))
    out = jnp.dot(merged.astype(BF16), wout_ref[...], preferred_element_type=F32)
    gate = mod_ref[pl.ds(b, 1), 2 * D_MODEL:3 * D_MODEL]
    o_ref[0] = x_ref[0] + gate * out


def _mixer_output(x, o_f, o_b, z_a, y_b, gates, mod, gdn_norm_w, w_oa, w_ob, w_out):
    n_batch, n_lat, _ = x.shape
    tok_spec = lambda w: pl.BlockSpec((1, ROW_TILE, w), lambda b, i: (b, i, 0))
    w_spec = pl.BlockSpec((D_MODEL, D_MODEL), lambda b, i: (0, 0))
    return pl.pallas_call(
        _mixer_out_kernel,
        grid=(n_batch, n_lat // ROW_TILE),
        in_specs=[tok_spec(D_MODEL), tok_spec(GDN_WIDTH), tok_spec(GDN_WIDTH), tok_spec(GDN_WIDTH),
                  tok_spec(D_MODEL), tok_spec(2 * D_MODEL),
                  pl.BlockSpec((8, 3 * D_MODEL), lambda b, i: (0, 0)),
                  pl.BlockSpec((8, LANES), lambda b, i: (0, 0)),
                  w_spec, w_spec, w_spec],
        out_specs=tok_spec(D_MODEL),
        out_shape=jax.ShapeDtypeStruct(x.shape, F32),
        scratch_shapes=[pltpu.VMEM((ROW_TILE, GDN_WIDTH), BF16)],
        compiler_params=pltpu.CompilerParams(
            dimension_semantics=("arbitrary", "arbitrary"), vmem_limit_bytes=VMEM_LIMIT),
        name="mixer_output",
    )(x, o_f, o_b, z_a, y_b, gates, mod, gdn_norm_w, w_oa, w_ob, w_out)


def _pad_rows(a, rows=8):
    return jnp.pad(a, ((0, rows - a.shape[0]), (0, 0)))


def _pad_lanes(a, lanes=LANES):
    return jnp.pad(a, ((0, 0), (0, lanes - a.shape[1])))


def _rope_tables(n_lat, n_ctx):
    rows = n_lat // GRID_W
    row = np.broadcast_to(np.arange(rows)[:, None], (rows, GRID_W)).reshape(-1).astype(np.float32)
    col = np.broadcast_to(np.arange(GRID_W)[None, :], (rows, GRID_W)).reshape(-1).astype(np.float32)
    n_freq = DIFF_DQK // 4
    inv_freq = (np.float32(ROPE_THETA) ** (-np.arange(n_freq, dtype=np.float32) / np.float32(n_freq))).astype(np.float32)
    ang_r = row[:, None] * inv_freq
    ang_c = col[:, None] * inv_freq
    ang = np.concatenate([ang_r, ang_r, ang_c, ang_c] * 2, axis=-1).astype(np.float32)
    cos = np.concatenate([np.cos(ang), np.ones((n_ctx, LANES), np.float32)], axis=0)
    sin = np.concatenate([np.sin(ang), np.zeros((n_ctx, LANES), np.float32)], axis=0)
    sign = np.where(np.arange(LANES) % (DIFF_DQK // 2) < DIFF_DQK // 4, -1.0, 1.0).astype(np.float32)
    return (jnp.asarray(cos, F32), jnp.asarray(sin, F32),
            jnp.asarray(np.ascontiguousarray(cos.T), F32), jnp.asarray(np.ascontiguousarray((sin * sign).T), F32))


def kernel(x, c, ctx, c_ctx, w_ada, b_ada, w_in, conv_w, a_log, dt_bias, gdn_norm_w, q_norm_w, k_norm_w,
           lambda_q1, lambda_k1, lambda_q2, lambda_k2, diff_norm_w, w_oa, w_ob, w_out):
    assert w_ada.shape[0] == 1, "single-layer block"
    n_batch, n_lat, _ = x.shape
    n_ctx = ctx.shape[1]
    assert n_ctx == ROW_TILE and n_lat % ROW_TILE == 0 and n_batch < 8

    bounds = np.cumsum((0,) + IN_SIZES)
    assert bounds[2] == 4 * GDN_WIDTH and bounds[4] - bounds[2] <= LANES
    w_all = w_in[0].astype(BF16)
    piece = lambda j0, j1: w_all[:, bounds[j0]:bounds[j1]]
    w_qv_t = jnp.concatenate([piece(4, 5).T, piece(6, 7).T], axis=0)
    w_k = piece(5, 6)
    w_zg = piece(7, 9)
    cc = _pad_rows(jnp.concatenate([c, c_ctx[None, :]], axis=0))
    conv_w8 = _pad_rows(conv_w[0])
    gpar = _pad_rows(_pad_lanes(jnp.stack([a_log[0].reshape(-1), dt_bias[0].reshape(-1)])))
    gains = _pad_rows(jnp.stack([jnp.tile(q_norm_w[0], 2), jnp.tile(k_norm_w[0], 2)]))
    qgain_t = jnp.broadcast_to((jnp.tile(q_norm_w[0], 2) * (DIFF_DQK ** -0.5 * LOG2E))[:, None],
                               (HEAD_W, ROW_TILE))
    lam_par = _pad_rows(_pad_lanes(jnp.stack([lambda_q1[0], lambda_k1[0], lambda_q2[0], lambda_k2[0]])))
    cos_tab, sin_tab, cos_t, sin_t = _rope_tables(n_lat, n_ctx)

    mod = _ada_modulation(cc, w_ada[0], b_ada)
    q_a, k_a, v_a, gb, z_a, q_t, kn, v_t, z_b, gates = _input_projection(
        x, ctx, mod, w_all, w_qv_t, w_k, w_zg, cos_tab, sin_tab, cos_t, sin_t, gains, qgain_t, conv_w8, gpar)
    o_f, o_b = _gdn_scan(q_a, k_a, v_a, gb, n_lat)
    y_b = _diff_attention(q_t, kn, v_t, z_b, lam_par, _pad_rows(diff_norm_w), n_lat)
    return _mixer_output(x, o_f, o_b, z_a, y_b, gates, mod, _pad_rows(gdn_norm_w),
                         w_oa[0].astype(BF16), w_ob[0].astype(BF16), w_out[0].astype(BF16))
```

```python
import functools
import math

import jax
import jax.numpy as jnp
import numpy as np
from jax import lax
from jax.experimental import pallas as pl
from jax.experimental.pallas import tpu as pltpu

F32 = jnp.float32
BF16 = jnp.bfloat16

D_MODEL = 1024
GRID_W = 64
EPS = 1e-6
HEADS = 8
HEAD_W = 128
GDN_WIDTH = HEADS * HEAD_W
CONV_K = 5
CHUNK = 64
DIFF_DQK = 64
ROPE_THETA = 10000.0
LAMBDA_INIT = 0.8 - 0.6 * math.exp(-0.3 * 0)
IN_SIZES = (3 * GDN_WIDTH, GDN_WIDTH, 2 * HEADS, 2 * HEADS, 1024, 1024, 1024, 1024, 2 * D_MODEL)

ROW_TILE = 256
HALO = 16
LANES = 128
VMEM_LIMIT = 56 * 1024 * 1024


def _bdot(a, b):
    return jnp.dot(a.astype(BF16), b.astype(BF16), preferred_element_type=F32)


def _bdot_nt(a, b):
    return lax.dot_general(a.astype(BF16), b.astype(BF16), (((1,), (1,)), ((), ())),
                           preferred_element_type=F32)


def _bdot_tn(a, b):
    return lax.dot_general(a.astype(BF16), b.astype(BF16), (((0,), (0,)), ((), ())),
                           preferred_element_type=F32)


def _silu(x):
    return x * (1.0 / (1.0 + jnp.exp(-x)))


def _sigmoid(x):
    return 1.0 / (1.0 + jnp.exp(-x))


def _split3(x):
    hi = x.astype(BF16)
    r1 = x - hi.astype(F32)
    mid = r1.astype(BF16)
    lo = (r1 - mid.astype(F32)).astype(BF16)
    return hi, mid, lo


def _ada_kernel(c_ref, w_ref, b_ref, o_ref):
    a = _split3(_silu(c_ref[...]))
    w = _split3(w_ref[...])
    acc = b_ref[...]
    for i, j in ((2, 0), (1, 1), (0, 2), (1, 0), (0, 1), (0, 0)):
        acc = acc + jnp.dot(a[i], w[j], preferred_element_type=F32)
    o_ref[...] = acc


def _ada_modulation(cc, w_ada, b_ada):
    n_col = w_ada.shape[1] // D_MODEL
    return pl.pallas_call(
        _ada_kernel,
        grid=(n_col,),
        in_specs=[pl.BlockSpec((8, D_MODEL), lambda j: (0, 0)),
                  pl.BlockSpec((D_MODEL, D_MODEL), lambda j: (0, j)),
                  pl.BlockSpec((1, D_MODEL), lambda j: (0, j))],
        out_specs=pl.BlockSpec((8, D_MODEL), lambda j: (0, j)),
        out_shape=jax.ShapeDtypeStruct((8, w_ada.shape[1]), F32),
        name="ada_modulation",
    )(cc, w_ada, b_ada)


LOG2E = 1.4426950408889634


def _qk_norm_rope(x, gain, cos, sin):
    lane = lax.broadcasted_iota(jnp.int32, x.shape, 1)
    first_comp = lane < DIFF_DQK
    first_half = (lane % (DIFF_DQK // 2)) < (DIFF_DQK // 4)
    xx = x * x
    s1 = jnp.sum(jnp.where(first_comp, xx, 0.0), axis=-1, keepdims=True)
    s2 = jnp.sum(jnp.where(first_comp, 0.0, xx), axis=-1, keepdims=True)
    ms = jnp.where(first_comp, s1, s2) * (1.0 / DIFF_DQK)
    t = x * lax.rsqrt(ms + EPS) * gain
    rot = jnp.where(first_half, -pltpu.roll(t, LANES - DIFF_DQK // 4, 1), pltpu.roll(t, DIFF_DQK // 4, 1))
    return t * cos + rot * sin


def _inproj_kernel(n_lat_tiles, n_batch, x_ref, ctx_ref, mod_ref, wl_ref, wqvt_ref, wk_ref, wzg_ref,
                   cos_ref, sin_ref, cost_ref, sint_ref, gain_ref, qgain_ref, convw_ref, gpar_ref,
                   qa_ref, ka_ref, va_ref, gb_ref, za_ref, qt_ref, kn_ref, vt_ref, zb_ref, gates_ref, xp_ref):
    b = pl.program_id(0)
    i = pl.program_id(1)

    @pl.when(i == 0)
    def _():
        xp_ref[...] = jnp.zeros_like(xp_ref)

    is_ctx = i >= n_lat_tiles
    xt = jnp.where(is_ctx, ctx_ref[0], x_ref[0])
    mod = mod_ref[pl.ds(jnp.where(is_ctx, n_batch, b), 1), :]
    shift = mod[:, 0:D_MODEL]
    scale = mod[:, D_MODEL:2 * D_MODEL]
    ms = jnp.mean(xt * xt, axis=-1, keepdims=True)
    h = (xt * lax.rsqrt(ms + EPS) * (1.0 + scale) + shift).astype(BF16)
    proj = lambda w_ref, c0, cw: jnp.dot(h, w_ref[:, c0:c0 + cw], preferred_element_type=F32)
    head_cols = lambda hd: slice(hd * HEAD_W, (hd + 1) * HEAD_W)

    cur_ends_seq = (i == n_lat_tiles) | (i == n_lat_tiles + 1)
    next_starts_seq = i == n_lat_tiles
    pad = CONV_K // 2
    for which, o_ref in enumerate((qa_ref, ka_ref, va_ref)):
        res = proj(wl_ref, which * GDN_WIDTH, GDN_WIDTH)
        for hd in range(HEADS):
            cols = slice(which * GDN_WIDTH + hd * HEAD_W, which * GDN_WIDTH + (hd + 1) * HEAD_W)
            xp_ref[HALO + ROW_TILE:, cols] = jnp.where(cur_ends_seq, 0.0, res[0:HALO, head_cols(hd)])
            slab = xp_ref[:, cols].reshape(-1, 8, HEAD_W)
            sub = lax.broadcasted_iota(jnp.int32, slab.shape, 1)
            y = None
            for j in range(CONV_K):
                s = pad - j
                if s == 0:
                    shifted = slab
                else:
                    rot = pltpu.roll(slab, s % 8, 1)
                    if s > 0:
                        shifted = jnp.where(sub >= s, rot, jnp.concatenate([rot[-1:], rot[:-1]], axis=0))
                    else:
                        shifted = jnp.where(sub < 8 + s, rot, jnp.concatenate([rot[1:], rot[:1]], axis=0))
                body = shifted[HALO // 8:(HALO + ROW_TILE) // 8].reshape(ROW_TILE, HEAD_W)
                term = body * convw_ref[j:j + 1, cols]
                y = term if y is None else y + term
            y = _silu(y)
            if which < 2:
                y = y * lax.rsqrt(jnp.sum(y * y, axis=-1, keepdims=True) + EPS)
            if which == 0:
                y = y * (HEAD_W ** -0.5)
            o_ref[0, :, head_cols(hd)] = y.astype(o_ref.dtype)
            xp_ref[0:HALO, cols] = jnp.where(next_starts_seq, 0.0, xp_ref[ROW_TILE:ROW_TILE + HALO, cols])
            xp_ref[HALO:HALO + ROW_TILE, cols] = res[:, head_cols(hd)]
    za_ref[0] = proj(wl_ref, 3 * GDN_WIDTH, GDN_WIDTH).astype(za_ref.dtype)

    ab = proj(wl_ref, 4 * GDN_WIDTH, LANES)
    lane = lax.broadcasted_iota(jnp.int32, ab.shape, 1)
    xs = ab + gpar_ref[1:2, :]
    softplus = jnp.maximum(xs, 0.0) + jnp.log1p(jnp.exp(-jnp.abs(xs)))
    g_all = -jnp.exp(gpar_ref[0:1, :]) * softplus
    gb_ref[0] = jnp.where(lane < 2 * HEADS, g_all, jnp.where(lane < 4 * HEADS, _sigmoid(ab), 0.0))

    nt = (((1,), (1,)), ((), ()))
    q_t = lax.dot_general(wqvt_ref[0:D_MODEL, :], h, nt, preferred_element_type=F32)
    cos_t = cost_ref[...]
    sin_t = sint_ref[...]
    half_rows = DIFF_DQK // 4
    for hd in range(HEADS):
        t = q_t[head_cols(hd), :]
        tt = t * t
        r = jnp.concatenate(
            [jnp.broadcast_to(lax.rsqrt(jnp.mean(tt[c * DIFF_DQK:(c + 1) * DIFF_DQK], axis=0, keepdims=True) + EPS),
                              (DIFF_DQK, ROW_TILE)) for c in range(2)], axis=0)
        tn = t * r * qgain_ref[...]
        blocks = [tn[b * half_rows:(b + 1) * half_rows] for b in range(HEAD_W // half_rows)]
        partner = jnp.concatenate([blocks[b ^ 1] for b in range(len(blocks))], axis=0)
        qt_ref[0, head_cols(hd), :] = (tn * cos_t + partner * sin_t).astype(qt_ref.dtype)
    vt_ref[0] = lax.dot_general(wqvt_ref[D_MODEL:2 * D_MODEL, :], h, nt,
                                preferred_element_type=F32).astype(vt_ref.dtype)
    cos = cos_ref[...]
    sin = sin_ref[...]
    res = proj(wk_ref, 0, D_MODEL)
    for hd in range(HEADS):
        kn_ref[0, :, head_cols(hd)] = _qk_norm_rope(
            res[:, head_cols(hd)], gain_ref[1:2, :], cos, sin).astype(kn_ref.dtype)
    zb_ref[0] = proj(wzg_ref, 0, D_MODEL).astype(zb_ref.dtype)
    for c0 in range(0, 2 * D_MODEL, D_MODEL):
        gates_ref[0, :, c0:c0 + D_MODEL] = proj(wzg_ref, D_MODEL + c0, D_MODEL).astype(gates_ref.dtype)


def _input_projection(x, ctx, mod, w_all, w_qv_t, w_k, w_zg, cos_tab, sin_tab, cos_t, sin_t, gains, qgain_t,
                      conv_w8, gpar):
    n_batch, n_lat, _ = x.shape
    n_lat_tiles = n_lat // ROW_TILE
    n_tiles = n_lat_tiles + 1
    n_tok = n_lat + ctx.shape[1]
    last = n_tiles - 1
    now = lambda b, i: (b, jnp.minimum(i, last), 0)
    now_t = lambda b, i: (b, 0, jnp.minimum(i, last))
    trailing = lambda b, i: (b, jnp.maximum(i - 1, 0), 0)
    tok = lambda w, dt: jax.ShapeDtypeStruct((n_batch, n_tok, w), dt)
    feat = jax.ShapeDtypeStruct((n_batch, D_MODEL, n_tok), BF16)
    out_shape = [tok(GDN_WIDTH, BF16)] * 3 + [tok(LANES, F32), tok(GDN_WIDTH, BF16), feat,
                                              tok(D_MODEL, BF16), feat, tok(D_MODEL, BF16),
                                              tok(2 * D_MODEL, BF16)]
    tok_spec = lambda w: pl.BlockSpec((1, ROW_TILE, w), now)
    feat_spec = pl.BlockSpec((1, D_MODEL, ROW_TILE), now_t)
    out_specs = ([pl.BlockSpec((1, ROW_TILE, GDN_WIDTH), trailing)] * 3
                 + [tok_spec(LANES), tok_spec(GDN_WIDTH), feat_spec, tok_spec(D_MODEL), feat_spec,
                    tok_spec(D_MODEL), tok_spec(2 * D_MODEL)])
    resident = lambda a: pl.BlockSpec(a.shape, lambda b, i: (0, 0), pipeline_mode=pl.Buffered(1))
    tab_spec = pl.BlockSpec((ROW_TILE, LANES), lambda b, i: (jnp.minimum(i, last), 0))
    tab_t_spec = pl.BlockSpec((LANES, ROW_TILE), lambda b, i: (0, jnp.minimum(i, last)))
    small = lambda a: pl.BlockSpec(a.shape, lambda b, i: (0, 0))
    return pl.pallas_call(
        functools.partial(_inproj_kernel, n_lat_tiles, n_batch),
        grid=(n_batch, n_tiles + 1),
        in_specs=[
            pl.BlockSpec((1, ROW_TILE, D_MODEL), lambda b, i: (b, jnp.minimum(i, n_lat_tiles - 1), 0)),
            pl.BlockSpec((1, ROW_TILE, D_MODEL), lambda b, i: (b, 0, 0)),
            small(mod),
            pl.BlockSpec((D_MODEL, 4 * GDN_WIDTH + LANES), lambda b, i: (0, 0), pipeline_mode=pl.Buffered(1)),
            resident(w_qv_t), resident(w_k), resident(w_zg),
            tab_spec, tab_spec, tab_t_spec, tab_t_spec, small(gains), small(qgain_t), small(conv_w8), small(gpar),
        ],
        out_specs=out_specs,
        out_shape=out_shape,
        scratch_shapes=[pltpu.VMEM((ROW_TILE + 2 * HALO, 3 * GDN_WIDTH), F32)],
        compiler_params=pltpu.CompilerParams(
            dimension_semantics=("arbitrary", "arbitrary"), vmem_limit_bytes=VMEM_LIMIT),
        name="input_projection",
    )(x, ctx, mod, w_all, w_qv_t, w_k, w_zg, cos_tab, sin_tab, cos_t, sin_t, gains, qgain_t, conv_w8, gpar)


GROUP = 1
GSIZE = GROUP * CHUNK
N_LEVELS = 5
M_EYE, M_PAIR, M_LEVEL0 = 0, 1, 2
M_INCL = (M_LEVEL0 + N_LEVELS, M_LEVEL0 + N_LEVELS + 2)
M_STRICT = (M_LEVEL0 + N_LEVELS + 1, M_LEVEL0 + N_LEVELS + 3)


def _scan_masks():
    r = np.arange(GSIZE)[:, None]
    c = np.arange(GSIZE)[None, :]
    blk = (r // CHUNK) == (c // CHUNK)
    planes = [r == c, (r >> 1) == (c >> 1)]
    for shift in range(1, N_LEVELS + 1):
        planes.append(((r >> (shift + 1)) == (c >> (shift + 1))) & ((r >> shift) != (c >> shift)))
    planes += [blk & (r >= c), blk & (r > c), blk & (r <= c), blk & (r < c)]
    return np.stack(planes).astype(np.float32)


def _gdn_scan_kernel(qf_ref, kf_ref, vf_ref, gbf_ref, qb_ref, kb_ref, vb_ref, gbb_ref, mask_ref,
                     of_ref, ob_ref, s_ref):
    @pl.when(pl.program_id(1) == 0)
    def _():
        s_ref[...] = jnp.zeros_like(s_ref)

    rows_of = lambda c: slice(c * CHUNK, (c + 1) * CHUNK)
    stack_rows = lambda parts: jnp.concatenate(parts, axis=0)

    groups = []
    for d, (q_ref, k_ref, v_ref, gb_ref, o_ref) in enumerate(
            ((qf_ref, kf_ref, vf_ref, gbf_ref, of_ref), (qb_ref, kb_ref, vb_ref, gbb_ref, ob_ref))):
        last = CHUNK - 1 if d == 0 else 0
        tri = mask_ref[M_INCL[d], 0:CHUNK, 0:CHUNK].astype(BF16)
        gb = gb_ref[0]
        gc = None
        for part in _split3(gb):
            term = jnp.dot(tri, part, preferred_element_type=F32)
            gc = term if gc is None else gc + term
        gc_t = gc.T
        for half in range(HEADS // GROUP):
            heads = [half * GROUP + c for c in range(GROUP)]
            lanes = [d * HEADS + h for h in heads]
            head_cols = [slice(h * HEAD_W, (h + 1) * HEAD_W) for h in heads]
            groups.append(dict(
                d=d, lanes=lanes, head_cols=head_cols, o_ref=o_ref,
                q=stack_rows([q_ref[0, :, cs].astype(F32) for cs in head_cols]),
                k=stack_rows([k_ref[0, :, cs].astype(F32) for cs in head_cols]),
                v=stack_rows([v_ref[0, :, cs].astype(F32) for cs in head_cols]),
                g_col=stack_rows([gc[:, l:l + 1] for l in lanes]),
                g_row=jnp.concatenate([gc_t[l:l + 1, :] for l in lanes], axis=1),
                g_last=stack_rows([jnp.broadcast_to(gc[last:last + 1, l:l + 1], (CHUNK, 1)) for l in lanes]),
                beta=stack_rows([gb[:, 2 * HEADS + l:2 * HEADS + l + 1] for l in lanes])))

    for g in groups:
        incl = mask_ref[M_INCL[g["d"]]]
        decay = jnp.exp((g["g_col"] - g["g_row"]) * incl) * incl
        kq = _bdot_nt(stack_rows([g["k"] * g["beta"], g["q"]]), g["k"])
        g["a"] = kq[:GSIZE] * decay * mask_ref[M_STRICT[g["d"]]]
        g["qk"] = kq[GSIZE:] * decay
        g["t"] = mask_ref[M_EYE] - g["a"] * mask_ref[M_PAIR]
    for level in range(N_LEVELS):
        for g in groups:
            g["x"] = _bdot(g["a"] * mask_ref[M_LEVEL0 + level], g["t"])
        for g in groups:
            g["t"] = g["t"] - _bdot(g["t"], g["x"])
    for g in groups:
        e_g = jnp.exp(g["g_col"])
        uw = _bdot(g["t"], jnp.concatenate([g["v"] * g["beta"], g["k"] * (g["beta"] * e_g)], axis=1))
        g["u"] = uw[:, :HEAD_W]
        g["w"] = uw[:, HEAD_W:]
        g["qd"] = g["q"] * e_g
        g["k_dec"] = g["k"] * jnp.exp(g["g_last"] - g["g_col"])

    state = [s_ref[lane] for lane in range(2 * HEADS)]
    for g in groups:
        g["ws_qs"] = [_bdot(stack_rows([g["w"][rows_of(c)], g["qd"][rows_of(c)]]), state[lane])
                      for c, lane in enumerate(g["lanes"])]
    for g in groups:
        g["v_new"] = [g["u"][rows_of(c)] - g["ws_qs"][c][:CHUNK] for c in range(GROUP)]
    for g in groups:
        for c, lane in enumerate(g["lanes"]):
            s_ref[lane] = (state[lane] * jnp.exp(g["g_last"][c * CHUNK:c * CHUNK + 1, :])
                           + _bdot_tn(g["k_dec"][rows_of(c)], g["v_new"][c]))
    for g in groups:
        o = (stack_rows([g["ws_qs"][c][CHUNK:] for c in range(GROUP)])
             + _bdot(g["qk"], stack_rows(g["v_new"])))
        for c, cs in enumerate(g["head_cols"]):
            g["o_ref"][0, :, cs] = o[rows_of(c)].astype(g["o_ref"].dtype)


def _gdn_scan(q, k, v, gb, n_lat):
    n_batch, n_tok, _ = q.shape
    n_lat_chunks = n_lat // CHUNK
    n_ctx_chunks = (n_tok - n_lat) // CHUNK
    n_steps = n_tok // CHUNK

    def fwd_chunk(s):
        return jnp.where(s < n_ctx_chunks, n_lat_chunks + s, s - n_ctx_chunks)

    def bwd_chunk(s):
        return jnp.where(s < n_ctx_chunks, n_lat_chunks + n_ctx_chunks - 1 - s,
                         n_lat_chunks - 1 - (s - n_ctx_chunks))

    f_spec = lambda w: pl.BlockSpec((1, CHUNK, w), lambda b, s: (b, fwd_chunk(s), 0))
    b_spec = lambda w: pl.BlockSpec((1, CHUNK, w), lambda b, s: (b, bwd_chunk(s), 0))
    masks = jnp.asarray(_scan_masks())
    return pl.pallas_call(
        _gdn_scan_kernel,
        grid=(n_batch, n_steps),
        in_specs=[f_spec(GDN_WIDTH)] * 3 + [f_spec(LANES)] + [b_spec(GDN_WIDTH)] * 3 + [b_spec(LANES)]
        + [pl.BlockSpec(masks.shape, lambda b, s: (0, 0, 0))],
        out_specs=[f_spec(GDN_WIDTH), b_spec(GDN_WIDTH)],
        out_shape=[jax.ShapeDtypeStruct((n_batch, n_tok, GDN_WIDTH), BF16)] * 2,
        scratch_shapes=[pltpu.VMEM((2 * HEADS, HEAD_W, HEAD_W), F32)],
        compiler_params=pltpu.CompilerParams(
            dimension_semantics=("arbitrary", "arbitrary"), vmem_limit_bytes=VMEM_LIMIT),
        name="gdn_scan",
    )(q, k, v, gb, q, k, v, gb, masks)


ATT_TQ = 1024
ATT_TK = 768
ATT_VMEM_LIMIT = 62 * 1024 * 1024


BOUND_LIMIT = 50.0


def _diff_attn_kernel(n_kv, q_ref, k_ref, vt_ref, z_ref, lam_ref, normw_ref, o_ref, kx_ref, kmax_ref):
    tq = q_ref.shape[2]

    @pl.when(pl.program_id(2) == 0)
    def _():
        lane_k = lax.broadcasted_iota(jnp.int32, (ATT_TK, HEAD_W), 1)
        unit = jnp.where(lane_k == 0, 1.0, 0.0).astype(BF16)

        def widen(j, carry):
            n1, n2 = carry
            rows = pl.ds(pl.multiple_of(j * ATT_TK, ATT_TK), ATT_TK)
            kb = k_ref[0, rows, :]
            kx_ref[rows, 0:HEAD_W] = kb
            kx_ref[rows, HEAD_W:2 * HEAD_W] = unit
            sq = kb.astype(F32) * kb.astype(F32)
            a1 = jnp.sum(jnp.where(lane_k < DIFF_DQK, sq, 0.0), axis=1, keepdims=True)
            a2 = jnp.sum(jnp.where(lane_k < DIFF_DQK, 0.0, sq), axis=1, keepdims=True)
            return (jnp.maximum(n1, jnp.max(a1, axis=0, keepdims=True)),
                    jnp.maximum(n2, jnp.max(a2, axis=0, keepdims=True)))

        zero11 = jnp.zeros((1, 1), F32)
        n1, n2 = lax.fori_loop(0, n_kv, widen, (zero11, zero11))
        kmax_ref[0:1, :] = jnp.broadcast_to(jnp.sqrt(n1), (1, LANES))
        kmax_ref[1:2, :] = jnp.broadcast_to(jnp.sqrt(n2), (1, LANES))

    qt = q_ref[0]
    feat = lax.broadcasted_iota(jnp.int32, (HEAD_W, tq), 0)
    first = feat < DIFF_DQK
    zero = jnp.zeros_like(qt)
    qs_t = jnp.concatenate([jnp.where(first, qt, zero), jnp.where(first, zero, qt)], axis=1)
    sq = qt.astype(F32) * qt.astype(F32)
    c1 = jnp.sqrt(jnp.sum(jnp.where(first, sq, 0.0), axis=0, keepdims=True)) * kmax_ref[0:1, 0:1]
    c2 = jnp.sqrt(jnp.sum(jnp.where(first, 0.0, sq), axis=0, keepdims=True)) * kmax_ref[1:2, 0:1]
    c = jnp.concatenate([c1, c2], axis=1)
    bound = jnp.max(c)

    def shifted_by_bound():
        feat2 = lax.broadcasted_iota(jnp.int32, (HEAD_W, 2 * tq), 0)
        aug = jnp.where(feat2 == 0, -c, 0.0).astype(BF16)
        qx_t = jnp.concatenate([qs_t, aug], axis=0)
        l = jnp.zeros((1, 2 * tq), F32)
        acc = jnp.zeros((HEAD_W, 2 * tq), F32)
        for j in range(n_kv):
            rows = slice(j * ATT_TK, (j + 1) * ATT_TK)
            p = jnp.exp2(jnp.dot(kx_ref[rows, :], qx_t, preferred_element_type=F32))
            l = l + jnp.sum(p, axis=0, keepdims=True)
            acc = acc + jnp.dot(vt_ref[0, :, rows], p.astype(BF16), preferred_element_type=F32)
        return acc / l

    def running_max():
        m = jnp.full((1, 2 * tq), -jnp.inf, F32)
        l = jnp.zeros((1, 2 * tq), F32)
        acc = jnp.zeros((HEAD_W, 2 * tq), F32)
        for j in range(n_kv):
            rows = slice(j * ATT_TK, (j + 1) * ATT_TK)
            s = jnp.dot(k_ref[0, rows, :], qs_t, preferred_element_type=F32)
            m_new = jnp.maximum(m, jnp.max(s, axis=0, keepdims=True))
            alpha = jnp.exp2(m - m_new)
            p = jnp.exp2(s - m_new)
            l = alpha * l + jnp.sum(p, axis=0, keepdims=True)
            acc = alpha * acc + jnp.dot(vt_ref[0, :, rows], p.astype(BF16), preferred_element_type=F32)
            m = m_new
        return acc / l

    o_t = lax.cond(bound < BOUND_LIMIT, shifted_by_bound, running_max)
    lam_p = lam_ref[...]
    lam = (jnp.exp(jnp.sum(lam_p[0:1, :] * lam_p[1:2, :], axis=-1, keepdims=True))
           - jnp.exp(jnp.sum(lam_p[2:3, :] * lam_p[3:4, :], axis=-1, keepdims=True)) + LAMBDA_INIT)
    o_t = o_t[:, :tq] - lam * o_t[:, tq:]
    y_t = o_t * lax.rsqrt(jnp.mean(o_t * o_t, axis=0, keepdims=True) + EPS)
    y = y_t.T * normw_ref[0:1, :]
    o_ref[0] = (y * (1.0 - LAMBDA_INIT) * _silu(z_ref[0].astype(F32))).astype(o_ref.dtype)


def _diff_attention(q_t, kn, v_t, z_b, lam_par, norm_w, n_lat):
    n_batch, n_tok, _ = kn.shape
    assert n_lat % ATT_TQ == 0 and n_tok % ATT_TK == 0
    qt_spec = pl.BlockSpec((1, HEAD_W, ATT_TQ), lambda b, h, i: (b, h, i))
    tok_spec = pl.BlockSpec((1, ATT_TQ, HEAD_W), lambda b, h, i: (b, i, h))
    kv_spec = pl.BlockSpec((1, n_tok, HEAD_W), lambda b, h, i: (b, 0, h))
    par_spec = pl.BlockSpec((8, LANES), lambda b, h, i: (0, 0))
    return pl.pallas_call(
        functools.partial(_diff_attn_kernel, n_tok // ATT_TK),
        grid=(n_batch, HEADS, n_lat // ATT_TQ),
        in_specs=[qt_spec, kv_spec, pl.BlockSpec((1, HEAD_W, n_tok), lambda b, h, i: (b, h, 0)), tok_spec,
                  par_spec, par_spec],
        out_specs=tok_spec,
        out_shape=jax.ShapeDtypeStruct((n_batch, n_lat, HEADS * HEAD_W), BF16),
        scratch_shapes=[pltpu.VMEM((n_tok, 2 * HEAD_W), BF16), pltpu.VMEM((8, LANES), F32)],
        compiler_params=pltpu.CompilerParams(
            dimension_semantics=("arbitrary", "arbitrary", "arbitrary"), vmem_limit_bytes=ATT_VMEM_LIMIT),
        name="diff_attention",
    )(q_t, kn, v_t, z_b, lam_par, norm_w)


def _mixer_out_kernel(x_ref, of_ref, ob_ref, za_ref, yb_ref, gates_ref, mod_ref, normw_ref,
                      woa_ref, wob_ref, wout_ref, o_ref, ya_ref):
    b = pl.program_id(0)
    for h in range(HEADS):
        cols = slice(h * HEAD_W, (h + 1) * HEAD_W)
        o_a = of_ref[0, :, cols].astype(F32) + ob_ref[0, :, cols].astype(F32)
        y = o_a * lax.rsqrt(jnp.mean(o_a * o_a, axis=-1, keepdims=True) + EPS) * normw_ref[0:1, :]
        ya_ref[:, cols] = (y * _silu(za_ref[0, :, cols].astype(F32))).astype(BF16)
    gates = _sigmoid(gates_ref[0].astype(F32))
    merged = (gates[:, :D_MODEL] * jnp.dot(ya_ref[...], woa_ref[...], preferred_element_type=F32)
              + gates[:, D_MODEL:] * jnp.dot(yb_ref[0], wob_ref[...], preferred_element_type=F32))
    out = jnp.dot(merged.astype(BF16), wout_ref[...], preferred_element_type=F32)
    gate = mod_ref[pl.ds(b, 1), 2 * D_MODEL:3 * D_MODEL]
    o_ref[0] = x_ref[0] + gate * out


def _mixer_output(x, o_f, o_b, z_a, y_b, gates, mod, gdn_norm_w, w_oa, w_ob, w_out):
    n_batch, n_lat, _ = x.shape
    tok_spec = lambda w: pl.BlockSpec((1, ROW_TILE, w), lambda b, i: (b, i, 0))
    w_spec = pl.BlockSpec((D_MODEL, D_MODEL), lambda b, i: (0, 0))
    return pl.pallas_call(
        _mixer_out_kernel,
        grid=(n_batch, n_lat // ROW_TILE),
        in_specs=[tok_spec(D_MODEL), tok_spec(GDN_WIDTH), tok_spec(GDN_WIDTH), tok_spec(GDN_WIDTH),
                  tok_spec(D_MODEL), tok_spec(2 * D_MODEL),
                  pl.BlockSpec((8, 3 * D_MODEL), lambda b, i: (0, 0)),
                  pl.BlockSpec((8, LANES), lambda b, i: (0, 0)),
                  w_spec, w_spec, w_spec],
        out_specs=tok_spec(D_MODEL),
        out_shape=jax.ShapeDtypeStruct(x.shape, F32),
        scratch_shapes=[pltpu.VMEM((ROW_TILE, GDN_WIDTH), BF16)],
        compiler_params=pltpu.CompilerParams(
            dimension_semantics=("arbitrary", "arbitrary"), vmem_limit_bytes=VMEM_LIMIT),
        name="mixer_output",
    )(x, o_f, o_b, z_a, y_b, gates, mod, gdn_norm_w, w_oa, w_ob, w_out)


def _pad_rows(a, rows=8):
    return jnp.pad(a, ((0, rows - a.shape[0]), (0, 0)))


def _pad_lanes(a, lanes=LANES):
    return jnp.pad(a, ((0, 0), (0, lanes - a.shape[1])))


def _rope_tables(n_lat, n_ctx):
    rows = n_lat // GRID_W
    row = np.broadcast_to(np.arange(rows)[:, None], (rows, GRID_W)).reshape(-1).astype(np.float32)
    col = np.broadcast_to(np.arange(GRID_W)[None, :], (rows, GRID_W)).reshape(-1).astype(np.float32)
    n_freq = DIFF_DQK // 4
    inv_freq = (np.float32(ROPE_THETA) ** (-np.arange(n_freq, dtype=np.float32) / np.float32(n_freq))).astype(np.float32)
    ang_r = row[:, None] * inv_freq
    ang_c = col[:, None] * inv_freq
    ang = np.concatenate([ang_r, ang_r, ang_c, ang_c] * 2, axis=-1).astype(np.float32)
    cos = np.concatenate([np.cos(ang), np.ones((n_ctx, LANES), np.float32)], axis=0)
    sin = np.concatenate([np.sin(ang), np.zeros((n_ctx, LANES), np.float32)], axis=0)
    sign = np.where(np.arange(LANES) % (DIFF_DQK // 2) < DIFF_DQK // 4, -1.0, 1.0).astype(np.float32)
    return (jnp.asarray(cos, F32), jnp.asarray(sin, F32),
            jnp.asarray(np.ascontiguousarray(cos.T), F32), jnp.asarray(np.ascontiguousarray((sin * sign).T), F32))


def kernel(x, c, ctx, c_ctx, w_ada, b_ada, w_in, conv_w, a_log, dt_bias, gdn_norm_w, q_norm_w, k_norm_w,
           lambda_q1, lambda_k1, lambda_q2, lambda_k2, diff_norm_w, w_oa, w_ob, w_out):
    assert w_ada.shape[0] == 1, "single-layer block"
    n_batch, n_lat, _ = x.shape
    n_ctx = ctx.shape[1]
    assert n_ctx == ROW_TILE and n_lat % ROW_TILE == 0 and n_batch < 8

    bounds = np.cumsum((0,) + IN_SIZES)
    assert bounds[2] == 4 * GDN_WIDTH and bounds[4] - bounds[2] <= LANES
    w_all = w_in[0].astype(BF16)
    piece = lambda j0, j1: w_all[:, bounds[j0]:bounds[j1]]
    w_qv_t = jnp.concatenate([piece(4, 5).T, piece(6, 7).T], axis=0)
    w_k = piece(5, 6)
    w_zg = piece(7, 9)
    cc = _pad_rows(jnp.concatenate([c, c_ctx[None, :]], axis=0))
    conv_w8 = _pad_rows(conv_w[0])
    gpar = _pad_rows(_pad_lanes(jnp.stack([a_log[0].reshape(-1), dt_bias[0].reshape(-1)])))
    gains = _pad_rows(jnp.stack([jnp.tile(q_norm_w[0], 2), jnp.tile(k_norm_w[0], 2)]))
    qgain_t = jnp.broadcast_to((jnp.tile(q_norm_w[0], 2) * (DIFF_DQK ** -0.5 * LOG2E))[:, None],
                               (HEAD_W, ROW_TILE))
    lam_par = _pad_rows(_pad_lanes(jnp.stack([lambda_q1[0], lambda_k1[0], lambda_q2[0], lambda_k2[0]])))
    cos_tab, sin_tab, cos_t, sin_t = _rope_tables(n_lat, n_ctx)

    mod = _ada_modulation(cc, w_ada[0], b_ada)
    q_a, k_a, v_a, gb, z_a, q_t, kn, v_t, z_b, gates = _input_projection(
        x, ctx, mod, w_all, w_qv_t, w_k, w_zg, cos_tab, sin_tab, cos_t, sin_t, gains, qgain_t, conv_w8, gpar)
    o_f, o_b = _gdn_scan(q_a, k_a, v_a, gb, n_lat)
    y_b = _diff_attention(q_t, kn, v_t, z_b, lam_par, _pad_rows(diff_norm_w), n_lat)
    return _mixer_output(x, o_f, o_b, z_a, y_b, gates, mod, _pad_rows(gdn_norm_w),
                         w_oa[0].astype(BF16), w_ob[0].astype(BF16), w_out[0].astype(BF16))
```

```python
import functools
import math

import jax
import jax.numpy as jnp
import numpy as np
from jax import lax
from jax.experimental import pallas as pl
from jax.experimental.pallas import tpu as pltpu

F32 = jnp.float32
BF16 = jnp.bfloat16

D_MODEL = 1024
GRID_W = 64
EPS = 1e-6
HEADS = 8
HEAD_W = 128
GDN_WIDTH = HEADS * HEAD_W
CONV_K = 5
CHUNK = 64
DIFF_DQK = 64
ROPE_THETA = 10000.0
LAMBDA_INIT = 0.8 - 0.6 * math.exp(-0.3 * 0)
IN_SIZES = (3 * GDN_WIDTH, GDN_WIDTH, 2 * HEADS, 2 * HEADS, 1024, 1024, 1024, 1024, 2 * D_MODEL)

ROW_TILE = 256
HALO = 16
LANES = 128
VMEM_LIMIT = 56 * 1024 * 1024


def _bdot(a, b):
    return jnp.dot(a.astype(BF16), b.astype(BF16), preferred_element_type=F32)


def _bdot_nt(a, b):
    return lax.dot_general(a.astype(BF16), b.astype(BF16), (((1,), (1,)), ((), ())),
                           preferred_element_type=F32)


def _bdot_tn(a, b):
    return lax.dot_general(a.astype(BF16), b.astype(BF16), (((0,), (0,)), ((), ())),
                           preferred_element_type=F32)


def _silu(x):
    return x * (1.0 / (1.0 + jnp.exp(-x)))


def _sigmoid(x):
    return 1.0 / (1.0 + jnp.exp(-x))


def _split3(x):
    hi = x.astype(BF16)
    r1 = x - hi.astype(F32)
    mid = r1.astype(BF16)
    lo = (r1 - mid.astype(F32)).astype(BF16)
    return hi, mid, lo


def _ada_kernel(c_ref, w_ref, b_ref, o_ref):
    a = _split3(_silu(c_ref[...]))
    w = _split3(w_ref[...])
    acc = b_ref[...]
    for i, j in ((2, 0), (1, 1), (0, 2), (1, 0), (0, 1), (0, 0)):
        acc = acc + jnp.dot(a[i], w[j], preferred_element_type=F32)
    o_ref[...] = acc


def _ada_modulation(cc, w_ada, b_ada):
    n_col = w_ada.shape[1] // D_MODEL
    return pl.pallas_call(
        _ada_kernel,
        grid=(n_col,),
        in_specs=[pl.BlockSpec((8, D_MODEL), lambda j: (0, 0)),
                  pl.BlockSpec((D_MODEL, D_MODEL), lambda j: (0, j)),
                  pl.BlockSpec((1, D_MODEL), lambda j: (0, j))],
        out_specs=pl.BlockSpec((8, D_MODEL), lambda j: (0, j)),
        out_shape=jax.ShapeDtypeStruct((8, w_ada.shape[1]), F32),
        name="ada_modulation",
    )(cc, w_ada, b_ada)


LOG2E = 1.4426950408889634


def _qk_norm_rope(x, gain, cos, sin):
    lane = lax.broadcasted_iota(jnp.int32, x.shape, 1)
    first_comp = lane < DIFF_DQK
    first_half = (lane % (DIFF_DQK // 2)) < (DIFF_DQK // 4)
    xx = x * x
    s1 = jnp.sum(jnp.where(first_comp, xx, 0.0), axis=-1, keepdims=True)
    s2 = jnp.sum(jnp.where(first_comp, 0.0, xx), axis=-1, keepdims=True)
    ms = jnp.where(first_comp, s1, s2) * (1.0 / DIFF_DQK)
    t = x * lax.rsqrt(ms + EPS) * gain
    rot = jnp.where(first_half, -pltpu.roll(t, LANES - DIFF_DQK // 4, 1), pltpu.roll(t, DIFF_DQK // 4, 1))
    return t * cos + rot * sin


def _inproj_kernel(n_lat_tiles, n_batch, x_ref, ctx_ref, mod_ref, wl_ref, wqvt_ref, wk_ref, wzg_ref,
                   cos_ref, sin_ref, cost_ref, sint_ref, gain_ref, qgain_ref, convw_ref, gpar_ref,
                   qa_ref, ka_ref, va_ref, gb_ref, za_ref, qt_ref, kn_ref, vt_ref, zb_ref, gates_ref, xp_ref):
    b = pl.program_id(0)
    i = pl.program_id(1)

    @pl.when(i == 0)
    def _():
        xp_ref[...] = jnp.zeros_like(xp_ref)

    is_ctx = i >= n_lat_tiles
    xt = jnp.where(is_ctx, ctx_ref[0], x_ref[0])
    mod = mod_ref[pl.ds(jnp.where(is_ctx, n_batch, b), 1), :]
    shift = mod[:, 0:D_MODEL]
    scale = mod[:, D_MODEL:2 * D_MODEL]
    ms = jnp.mean(xt * xt, axis=-1, keepdims=True)
    h = (xt * lax.rsqrt(ms + EPS) * (1.0 + scale) + shift).astype(BF16)
    proj = lambda w_ref, c0, cw: jnp.dot(h, w_ref[:, c0:c0 + cw], preferred_element_type=F32)
    head_cols = lambda hd: slice(hd * HEAD_W, (hd + 1) * HEAD_W)

    cur_ends_seq = (i == n_lat_tiles) | (i == n_lat_tiles + 1)
    next_starts_seq = i == n_lat_tiles
    pad = CONV_K // 2
    for which, o_ref in enumerate((qa_ref, ka_ref, va_ref)):
        res = proj(wl_ref, which * GDN_WIDTH, GDN_WIDTH)
        for hd in range(HEADS):
            cols = slice(which * GDN_WIDTH + hd * HEAD_W, which * GDN_WIDTH + (hd + 1) * HEAD_W)
            xp_ref[HALO + ROW_TILE:, cols] = jnp.where(cur_ends_seq, 0.0, res[0:HALO, head_cols(hd)])
            slab = xp_ref[:, cols].reshape(-1, 8, HEAD_W)
            sub = lax.broadcasted_iota(jnp.int32, slab.shape, 1)
            y = None
            for j in range(CONV_K):
                s = pad - j
                if s == 0:
                    shifted = slab
                else:
                    rot = pltpu.roll(slab, s % 8, 1)
                    if s > 0:
                        shifted = jnp.where(sub >= s, rot, jnp.concatenate([rot[-1:], rot[:-1]], axis=0))
                    else:
                        shifted = jnp.where(sub < 8 + s, rot, jnp.concatenate([rot[1:], rot[:1]], axis=0))
                body = shifted[HALO // 8:(HALO + ROW_TILE) // 8].reshape(ROW_TILE, HEAD_W)
                term = body * convw_ref[j:j + 1, cols]
                y = term if y is None else y + term
            y = _silu(y)
            if which < 2:
                y = y * lax.rsqrt(jnp.sum(y * y, axis=-1, keepdims=True) + EPS)
            if which == 0:
                y = y * (HEAD_W ** -0.5)
            o_ref[0, :, head_cols(hd)] = y.astype(o_ref.dtype)
            xp_ref[0:HALO, cols] = jnp.where(next_starts_seq, 0.0, xp_ref[ROW_TILE:ROW_TILE + HALO, cols])
            xp_ref[HALO:HALO + ROW_TILE, cols] = res[:, head_cols(hd)]
    za_ref[0] = proj(wl_ref, 3 * GDN_WIDTH, GDN_WIDTH).astype(za_ref.dtype)

    ab = proj(wl_ref, 4 * GDN_WIDTH, LANES)
    lane = lax.broadcasted_iota(jnp.int32, ab.shape, 1)
    xs = ab + gpar_ref[1:2, :]
    softplus = jnp.maximum(xs, 0.0) + jnp.log1p(jnp.exp(-jnp.abs(xs)))
    g_all = -jnp.exp(gpar_ref[0:1, :]) * softplus
    gb_ref[0] = jnp.where(lane < 2 * HEADS, g_all, jnp.where(lane < 4 * HEADS, _sigmoid(ab), 0.0))

    nt = (((1,), (1,)), ((), ()))
    q_t = lax.dot_general(wqvt_ref[0:D_MODEL, :], h, nt, preferred_element_type=F32)
    cos_t = cost_ref[...]
    sin_t = sint_ref[...]
    half_rows = DIFF_DQK // 4
    for hd in range(HEADS):
        t = q_t[head_cols(hd), :]
        tt = t * t
        r = jnp.concatenate(
            [jnp.broadcast_to(lax.rsqrt(jnp.mean(tt[c * DIFF_DQK:(c + 1) * DIFF_DQK], axis=0, keepdims=True) + EPS),
                              (DIFF_DQK, ROW_TILE)) for c in range(2)], axis=0)
        tn = t * r * qgain_ref[...]
        blocks = [tn[b * half_rows:(b + 1) * half_rows] for b in range(HEAD_W // half_rows)]
        partner = jnp.concatenate([blocks[b ^ 1] for b in range(len(blocks))], axis=0)
        qt_ref[0, head_cols(hd), :] = (tn * cos_t + partner * sin_t).astype(qt_ref.dtype)
    vt_ref[0] = lax.dot_general(wqvt_ref[D_MODEL:2 * D_MODEL, :], h, nt,
                                preferred_element_type=F32).astype(vt_ref.dtype)
    cos = cos_ref[...]
    sin = sin_ref[...]
    res = proj(wk_ref, 0, D_MODEL)
    for hd in range(HEADS):
        kn_ref[0, :, head_cols(hd)] = _qk_norm_rope(
            res[:, head_cols(hd)], gain_ref[1:2, :], cos, sin).astype(kn_ref.dtype)
    zb_ref[0] = proj(wzg_ref, 0, D_MODEL).astype(zb_ref.dtype)
    for c0 in range(0, 2 * D_MODEL, D_MODEL):
        gates_ref[0, :, c0:c0 + D_MODEL] = proj(wzg_ref, D_MODEL + c0, D_MODEL).astype(gates_ref.dtype)


def _input_projection(x, ctx, mod, w_all, w_qv_t, w_k, w_zg, cos_tab, sin_tab, cos_t, sin_t, gains, qgain_t,
                      conv_w8, gpar):
    n_batch, n_lat, _ = x.shape
    n_lat_tiles = n_lat // ROW_TILE
    n_tiles = n_lat_tiles + 1
    n_tok = n_lat + ctx.shape[1]
    last = n_tiles - 1
    now = lambda b, i: (b, jnp.minimum(i, last), 0)
    now_t = lambda b, i: (b, 0, jnp.minimum(i, last))
    trailing = lambda b, i: (b, jnp.maximum(i - 1, 0), 0)
    tok = lambda w, dt: jax.ShapeDtypeStruct((n_batch, n_tok, w), dt)
    feat = jax.ShapeDtypeStruct((n_batch, D_MODEL, n_tok), BF16)
    out_shape = [tok(GDN_WIDTH, BF16)] * 3 + [tok(LANES, F32), tok(GDN_WIDTH, BF16), feat,
                                              tok(D_MODEL, BF16), feat, tok(D_MODEL, BF16),
                                              tok(2 * D_MODEL, BF16)]
    tok_spec = lambda w: pl.BlockSpec((1, ROW_TILE, w), now)
    feat_spec = pl.BlockSpec((1, D_MODEL, ROW_TILE), now_t)
    out_specs = ([pl.BlockSpec((1, ROW_TILE, GDN_WIDTH), trailing)] * 3
                 + [tok_spec(LANES), tok_spec(GDN_WIDTH), feat_spec, tok_spec(D_MODEL), feat_spec,
                    tok_spec(D_MODEL), tok_spec(2 * D_MODEL)])
    resident = lambda a: pl.BlockSpec(a.shape, lambda b, i: (0, 0), pipeline_mode=pl.Buffered(1))
    tab_spec = pl.BlockSpec((ROW_TILE, LANES), lambda b, i: (jnp.minimum(i, last), 0))
    tab_t_spec = pl.BlockSpec((LANES, ROW_TILE), lambda b, i: (0, jnp.minimum(i, last)))
    small = lambda a: pl.BlockSpec(a.shape, lambda b, i: (0, 0))
    return pl.pallas_call(
        functools.partial(_inproj_kernel, n_lat_tiles, n_batch),
        grid=(n_batch, n_tiles + 1),
        in_specs=[
            pl.BlockSpec((1, ROW_TILE, D_MODEL), lambda b, i: (b, jnp.minimum(i, n_lat_tiles - 1), 0)),
            pl.BlockSpec((1, ROW_TILE, D_MODEL), lambda b, i: (b, 0, 0)),
            small(mod),
            pl.BlockSpec((D_MODEL, 4 * GDN_WIDTH + LANES), lambda b, i: (0, 0), pipeline_mode=pl.Buffered(1)),
            resident(w_qv_t), resident(w_k), resident(w_zg),
            tab_spec, tab_spec, tab_t_spec, tab_t_spec, small(gains), small(qgain_t), small(conv_w8), small(gpar),
        ],
        out_specs=out_specs,
        out_shape=out_shape,
        scratch_shapes=[pltpu.VMEM((ROW_TILE + 2 * HALO, 3 * GDN_WIDTH), F32)],
        compiler_params=pltpu.CompilerParams(
            dimension_semantics=("arbitrary", "arbitrary"), vmem_limit_bytes=VMEM_LIMIT),
        name="input_projection",
    )(x, ctx, mod, w_all, w_qv_t, w_k, w_zg, cos_tab, sin_tab, cos_t, sin_t, gains, qgain_t, conv_w8, gpar)


GROUP = 1
GSIZE = GROUP * CHUNK
N_LEVELS = 5
M_EYE, M_PAIR, M_LEVEL0 = 0, 1, 2
M_INCL = (M_LEVEL0 + N_LEVELS, M_LEVEL0 + N_LEVELS + 2)
M_STRICT = (M_LEVEL0 + N_LEVELS + 1, M_LEVEL0 + N_LEVELS + 3)
SCAN_CHUNKS = 2
SCAN_ROWS = SCAN_CHUNKS * CHUNK


def _scan_masks():
    r = np.arange(GSIZE)[:, None]
    c = np.arange(GSIZE)[None, :]
    blk = (r // CHUNK) == (c // CHUNK)
    planes = [r == c, (r >> 1) == (c >> 1)]
    for shift in range(1, N_LEVELS + 1):
        planes.append(((r >> (shift + 1)) == (c >> (shift + 1))) & ((r >> shift) != (c >> shift)))
    planes += [blk & (r >= c), blk & (r > c), blk & (r <= c), blk & (r < c)]
    return np.stack(planes).astype(np.float32)


def _gdn_scan_kernel(qf_ref, kf_ref, vf_ref, gbf_ref, qb_ref, kb_ref, vb_ref, gbb_ref, mask_ref,
                     of_ref, ob_ref, s_ref):
    @pl.when(pl.program_id(1) == 0)
    def _():
        s_ref[...] = jnp.zeros_like(s_ref)

    rows_of = lambda c: slice(c * CHUNK, (c + 1) * CHUNK)
    stack_rows = lambda parts: jnp.concatenate(parts, axis=0)

    groups = []
    for d, (q_ref, k_ref, v_ref, gb_ref, o_ref) in enumerate(
            ((qf_ref, kf_ref, vf_ref, gbf_ref, of_ref), (qb_ref, kb_ref, vb_ref, gbb_ref, ob_ref))):
        last = CHUNK - 1 if d == 0 else 0
        tri = mask_ref[M_INCL[d], 0:CHUNK, 0:CHUNK].astype(BF16)
        for pos in range(SCAN_CHUNKS):
            local = pos if d == 0 else SCAN_CHUNKS - 1 - pos
            rows = slice(local * CHUNK, (local + 1) * CHUNK)
            gb = gb_ref[0, rows, :]
            gc = None
            for part in _split3(gb):
                term = jnp.dot(tri, part, preferred_element_type=F32)
                gc = term if gc is None else gc + term
            gc_t = gc.T
            for half in range(HEADS // GROUP):
                heads = [half * GROUP + c for c in range(GROUP)]
                lanes = [d * HEADS + h for h in heads]
                head_cols = [slice(h * HEAD_W, (h + 1) * HEAD_W) for h in heads]
                groups.append(dict(
                    d=d, pos=pos, rows=rows, lanes=lanes, head_cols=head_cols, o_ref=o_ref,
                    q=stack_rows([q_ref[0, rows, cs].astype(F32) for cs in head_cols]),
                    k=stack_rows([k_ref[0, rows, cs].astype(F32) for cs in head_cols]),
                    v=stack_rows([v_ref[0, rows, cs].astype(F32) for cs in head_cols]),
                    g_col=stack_rows([gc[:, l:l + 1] for l in lanes]),
                    g_row=jnp.concatenate([gc_t[l:l + 1, :] for l in lanes], axis=1),
                    g_last=stack_rows([jnp.broadcast_to(gc[last:last + 1, l:l + 1], (CHUNK, 1))
                                       for l in lanes]),
                    beta=stack_rows([gb[:, 2 * HEADS + l:2 * HEADS + l + 1] for l in lanes])))

    for g in groups:
        incl = mask_ref[M_INCL[g["d"]]]
        decay = jnp.exp((g["g_col"] - g["g_row"]) * incl) * incl
        kq = _bdot_nt(stack_rows([g["k"] * g["beta"], g["q"]]), g["k"])
        g["a"] = kq[:GSIZE] * decay * mask_ref[M_STRICT[g["d"]]]
        g["qk"] = kq[GSIZE:] * decay
        g["t"] = mask_ref[M_EYE] - g["a"] * mask_ref[M_PAIR]
    for level in range(N_LEVELS):
        for g in groups:
            g["x"] = _bdot(g["a"] * mask_ref[M_LEVEL0 + level], g["t"])
        for g in groups:
            g["t"] = g["t"] - _bdot(g["t"], g["x"])
    for g in groups:
        e_g = jnp.exp(g["g_col"])
        uw = _bdot(g["t"], jnp.concatenate([g["v"] * g["beta"], g["k"] * (g["beta"] * e_g)], axis=1))
        g["u"] = uw[:, :HEAD_W]
        g["w"] = uw[:, HEAD_W:]
        g["qd"] = g["q"] * e_g
        g["k_dec"] = g["k"] * jnp.exp(g["g_last"] - g["g_col"])

    state = [s_ref[lane] for lane in range(2 * HEADS)]
    for pos in range(SCAN_CHUNKS):
        now = [g for g in groups if g["pos"] == pos]
        for g in now:
            g["ws_qs"] = [_bdot(stack_rows([g["w"][rows_of(c)], g["qd"][rows_of(c)]]), state[lane])
                          for c, lane in enumerate(g["lanes"])]
        for g in now:
            g["v_new"] = [g["u"][rows_of(c)] - g["ws_qs"][c][:CHUNK] for c in range(GROUP)]
        for g in now:
            for c, lane in enumerate(g["lanes"]):
                state[lane] = (state[lane] * jnp.exp(g["g_last"][c * CHUNK:c * CHUNK + 1, :])
                               + _bdot_tn(g["k_dec"][rows_of(c)], g["v_new"][c]))
        for g in now:
            o = (stack_rows([g["ws_qs"][c][CHUNK:] for c in range(GROUP)])
                 + _bdot(g["qk"], stack_rows(g["v_new"])))
            for c, cs in enumerate(g["head_cols"]):
                g["o_ref"][0, g["rows"], cs] = o[rows_of(c)].astype(g["o_ref"].dtype)
    for lane in range(2 * HEADS):
        s_ref[lane] = state[lane]


def _gdn_scan(q, k, v, gb, n_lat):
    n_batch, n_tok, _ = q.shape
    assert n_lat % SCAN_ROWS == 0 and (n_tok - n_lat) % SCAN_ROWS == 0
    n_lat_blocks = n_lat // SCAN_ROWS
    n_ctx_blocks = (n_tok - n_lat) // SCAN_ROWS
    n_steps = n_tok // SCAN_ROWS

    def fwd_block(s):
        return jnp.where(s < n_ctx_blocks, n_lat_blocks + s, s - n_ctx_blocks)

    def bwd_block(s):
        return jnp.where(s < n_ctx_blocks, n_lat_blocks + n_ctx_blocks - 1 - s,
                         n_lat_blocks - 1 - (s - n_ctx_blocks))

    f_spec = lambda w: pl.BlockSpec((1, SCAN_ROWS, w), lambda b, s: (b, fwd_block(s), 0))
    b_spec = lambda w: pl.BlockSpec((1, SCAN_ROWS, w), lambda b, s: (b, bwd_block(s), 0))
    masks = jnp.asarray(_scan_masks())
    return pl.pallas_call(
        _gdn_scan_kernel,
        grid=(n_batch, n_steps),
        in_specs=[f_spec(GDN_WIDTH)] * 3 + [f_spec(LANES)] + [b_spec(GDN_WIDTH)] * 3 + [b_spec(LANES)]
        + [pl.BlockSpec(masks.shape, lambda b, s: (0, 0, 0))],
        out_specs=[f_spec(GDN_WIDTH), b_spec(GDN_WIDTH)],
        out_shape=[jax.ShapeDtypeStruct((n_batch, n_tok, GDN_WIDTH), BF16)] * 2,
        scratch_shapes=[pltpu.VMEM((2 * HEADS, HEAD_W, HEAD_W), F32)],
        compiler_params=pltpu.CompilerParams(
            dimension_semantics=("arbitrary", "arbitrary"), vmem_limit_bytes=VMEM_LIMIT),
        name="gdn_scan",
    )(q, k, v, gb, q, k, v, gb, masks)


ATT_TQ = 1024
ATT_TK = 768
ATT_VMEM_LIMIT = 62 * 1024 * 1024


BOUND_LIMIT = 50.0


def _diff_attn_kernel(n_kv, q_ref, k_ref, vt_ref, z_ref, lam_ref, normw_ref, o_ref, kx_ref, kmax_ref):
    tq = q_ref.shape[2]

    @pl.when(pl.program_id(2) == 0)
    def _():
        lane_k = lax.broadcasted_iota(jnp.int32, (ATT_TK, HEAD_W), 1)
        unit = jnp.where(lane_k == 0, 1.0, 0.0).astype(BF16)

        def widen(j, carry):
            n1, n2 = carry
            rows = pl.ds(pl.multiple_of(j * ATT_TK, ATT_TK), ATT_TK)
            kb = k_ref[0, rows, :]
            kx_ref[rows, 0:HEAD_W] = kb
            kx_ref[rows, HEAD_W:2 * HEAD_W] = unit
            sq = kb.astype(F32) * kb.astype(F32)
            a1 = jnp.sum(jnp.where(lane_k < DIFF_DQK, sq, 0.0), axis=1, keepdims=True)
            a2 = jnp.sum(jnp.where(lane_k < DIFF_DQK, 0.0, sq), axis=1, keepdims=True)
            return (jnp.maximum(n1, jnp.max(a1, axis=0, keepdims=True)),
                    jnp.maximum(n2, jnp.max(a2, axis=0, keepdims=True)))

        zero11 = jnp.zeros((1, 1), F32)
        n1, n2 = lax.fori_loop(0, n_kv, widen, (zero11, zero11))
        kmax_ref[0:1, :] = jnp.broadcast_to(jnp.sqrt(n1), (1, LANES))
        kmax_ref[1:2, :] = jnp.broadcast_to(jnp.sqrt(n2), (1, LANES))

    qt = q_ref[0]
    feat = lax.broadcasted_iota(jnp.int32, (HEAD_W, tq), 0)
    first = feat < DIFF_DQK
    zero = jnp.zeros_like(qt)
    qs_t = jnp.concatenate([jnp.where(first, qt, zero), jnp.where(first, zero, qt)], axis=1)
    sq = qt.astype(F32) * qt.astype(F32)
    c1 = jnp.sqrt(jnp.sum(jnp.where(first, sq, 0.0), axis=0, keepdims=True)) * kmax_ref[0:1, 0:1]
    c2 = jnp.sqrt(jnp.sum(jnp.where(first, 0.0, sq), axis=0, keepdims=True)) * kmax_ref[1:2, 0:1]
    c = jnp.concatenate([c1, c2], axis=1)
    bound = jnp.max(c)

    def shifted_by_bound():
        feat2 = lax.broadcasted_iota(jnp.int32, (HEAD_W, 2 * tq), 0)
        aug = jnp.where(feat2 == 0, -c, 0.0).astype(BF16)
        qx_t = jnp.concatenate([qs_t, aug], axis=0)
        l = jnp.zeros((1, 2 * tq), F32)
        acc = jnp.zeros((HEAD_W, 2 * tq), F32)
        for j in range(n_kv):
            rows = slice(j * ATT_TK, (j + 1) * ATT_TK)
            p = jnp.exp2(jnp.dot(kx_ref[rows, :], qx_t, preferred_element_type=F32))
            l = l + jnp.sum(p, axis=0, keepdims=True)
            acc = acc + jnp.dot(vt_ref[0, :, rows], p.astype(BF16), preferred_element_type=F32)
        return acc / l

    def running_max():
        m = jnp.full((1, 2 * tq), -jnp.inf, F32)
        l = jnp.zeros((1, 2 * tq), F32)
        acc = jnp.zeros((HEAD_W, 2 * tq), F32)
        for j in range(n_kv):
            rows = slice(j * ATT_TK, (j + 1) * ATT_TK)
            s = jnp.dot(k_ref[0, rows, :], qs_t, preferred_element_type=F32)
            m_new = jnp.maximum(m, jnp.max(s, axis=0, keepdims=True))
            alpha = jnp.exp2(m - m_new)
            p = jnp.exp2(s - m_new)
            l = alpha * l + jnp.sum(p, axis=0, keepdims=True)
            acc = alpha * acc + jnp.dot(vt_ref[0, :, rows], p.astype(BF16), preferred_element_type=F32)
            m = m_new
        return acc / l

    o_t = lax.cond(bound < BOUND_LIMIT, shifted_by_bound, running_max)
    lam_p = lam_ref[...]
    lam = (jnp.exp(jnp.sum(lam_p[0:1, :] * lam_p[1:2, :], axis=-1, keepdims=True))
           - jnp.exp(jnp.sum(lam_p[2:3, :] * lam_p[3:4, :], axis=-1, keepdims=True)) + LAMBDA_INIT)
    o_t = o_t[:, :tq] - lam * o_t[:, tq:]
    y_t = o_t * lax.rsqrt(jnp.mean(o_t * o_t, axis=0, keepdims=True) + EPS)
    y = y_t.T * normw_ref[0:1, :]
    o_ref[0] = (y * (1.0 - LAMBDA_INIT) * _silu(z_ref[0].astype(F32))).astype(o_ref.dtype)


def _diff_attention(q_t, kn, v_t, z_b, lam_par, norm_w, n_lat):
    n_batch, n_tok, _ = kn.shape
    assert n_lat % ATT_TQ == 0 and n_tok % ATT_TK == 0
    qt_spec = pl.BlockSpec((1, HEAD_W, ATT_TQ), lambda b, h, i: (b, h, i))
    tok_spec = pl.BlockSpec((1, ATT_TQ, HEAD_W), lambda b, h, i: (b, i, h))
    kv_spec = pl.BlockSpec((1, n_tok, HEAD_W), lambda b, h, i: (b, 0, h))
    par_spec = pl.BlockSpec((8, LANES), lambda b, h, i: (0, 0))
    return pl.pallas_call(
        functools.partial(_diff_attn_kernel, n_tok // ATT_TK),
        grid=(n_batch, HEADS, n_lat // ATT_TQ),
        in_specs=[qt_spec, kv_spec, pl.BlockSpec((1, HEAD_W, n_tok), lambda b, h, i: (b, h, 0)), tok_spec,
                  par_spec, par_spec],
        out_specs=tok_spec,
        out_shape=jax.ShapeDtypeStruct((n_batch, n_lat, HEADS * HEAD_W), BF16),
        scratch_shapes=[pltpu.VMEM((n_tok, 2 * HEAD_W), BF16), pltpu.VMEM((8, LANES), F32)],
        compiler_params=pltpu.CompilerParams(
            dimension_semantics=("arbitrary", "arbitrary", "arbitrary"), vmem_limit_bytes=ATT_VMEM_LIMIT),
        name="diff_attention",
    )(q_t, kn, v_t, z_b, lam_par, norm_w)


def _mixer_out_kernel(x_ref, of_ref, ob_ref, za_ref, yb_ref, gates_ref, mod_ref, normw_ref,
                      woa_ref, wob_ref, wout_ref, o_ref, ya_ref):
    b = pl.program_id(0)
    for h in range(HEADS):
        cols = slice(h * HEAD_W, (h + 1) * HEAD_W)
        o_a = of_ref[0, :, cols].astype(F32) + ob_ref[0, :, cols].astype(F32)
        y = o_a * lax.rsqrt(jnp.mean(o_a * o_a, axis=-1, keepdims=True) + EPS) * normw_ref[0:1, :]
        ya_ref[:, cols] = (y * _silu(za_ref[0, :, cols].astype(F32))).astype(BF16)
    gates = _sigmoid(gates_ref[0].astype(F32))
    merged = (gates[:, :D_MODEL] * jnp.dot(ya_ref[...], woa_ref[...], preferred_element_type=F32)
              + gates[:, D_MODEL:] * jnp.dot(yb_ref[0], wob_ref[...], preferred_element_type=F32))
    out = jnp.dot(merged.astype(BF16), wout_ref[...], preferred_element_type=F32)
    gate = mod_ref[pl.ds(b, 1), 2 * D_MODEL:3 * D_MODEL]
    o_ref[0] = x_ref[0] + gate * out


def _mixer_output(x, o_f, o_b, z_a, y_b, gates, mod, gdn_norm_w, w_oa, w_ob, w_out):
    n_batch, n_lat, _ = x.shape
    tok_spec = lambda w: pl.BlockSpec((1, ROW_TILE, w), lambda b, i: (b, i, 0))
    w_spec = pl.BlockSpec((D_MODEL, D_MODEL), lambda b, i: (0, 0))
    return pl.pallas_call(
        _mixer_out_kernel,
        grid=(n_batch, n_lat // ROW_TILE),
        in_specs=[tok_spec(D_MODEL), tok_spec(GDN_WIDTH), tok_spec(GDN_WIDTH), tok_spec(GDN_WIDTH),
                  tok_spec(D_MODEL), tok_spec(2 * D_MODEL),
                  pl.BlockSpec((8, 3 * D_MODEL), lambda b, i: (0, 0)),
                  pl.BlockSpec((8, LANES), lambda b, i: (0, 0)),
                  w_spec, w_spec, w_spec],
        out_specs=tok_spec(D_MODEL),
        out_shape=jax.ShapeDtypeStruct(x.shape, F32),
        scratch_shapes=[pltpu.VMEM((ROW_TILE, GDN_WIDTH), BF16)],
        compiler_params=pltpu.CompilerParams(
            dimension_semantics=("arbitrary", "arbitrary"), vmem_limit_bytes=VMEM_LIMIT),
        name="mixer_output",
    )(x, o_f, o_b, z_a, y_b, gates, mod, gdn_norm_w, w_oa, w_ob, w_out)


def _pad_rows(a, rows=8):
    return jnp.pad(a, ((0, rows - a.shape[0]), (0, 0)))


def _pad_lanes(a, lanes=LANES):
    return jnp.pad(a, ((0, 0), (0, lanes - a.shape[1])))


def _rope_tables(n_lat, n_ctx):
    rows = n_lat // GRID_W
    row = np.broadcast_to(np.arange(rows)[:, None], (rows, GRID_W)).reshape(-1).astype(np.float32)
    col = np.broadcast_to(np.arange(GRID_W)[None, :], (rows, GRID_W)).reshape(-1).astype(np.float32)
    n_freq = DIFF_DQK // 4
    inv_freq = (np.float32(ROPE_THETA) ** (-np.arange(n_freq, dtype=np.float32) / np.float32(n_freq))).astype(np.float32)
    ang_r = row[:, None] * inv_freq
    ang_c = col[:, None] * inv_freq
    ang = np.concatenate([ang_r, ang_r, ang_c, ang_c] * 2, axis=-1).astype(np.float32)
    cos = np.concatenate([np.cos(ang), np.ones((n_ctx, LANES), np.float32)], axis=0)
    sin = np.concatenate([np.sin(ang), np.zeros((n_ctx, LANES), np.float32)], axis=0)
    sign = np.where(np.arange(LANES) % (DIFF_DQK // 2) < DIFF_DQK // 4, -1.0, 1.0).astype(np.float32)
    return (jnp.asarray(cos, F32), jnp.asarray(sin, F32),
            jnp.asarray(np.ascontiguousarray(cos.T), F32), jnp.asarray(np.ascontiguousarray((sin * sign).T), F32))


def kernel(x, c, ctx, c_ctx, w_ada, b_ada, w_in, conv_w, a_log, dt_bias, gdn_norm_w, q_norm_w, k_norm_w,
           lambda_q1, lambda_k1, lambda_q2, lambda_k2, diff_norm_w, w_oa, w_ob, w_out):
    assert w_ada.shape[0] == 1, "single-layer block"
    n_batch, n_lat, _ = x.shape
    n_ctx = ctx.shape[1]
    assert n_ctx == ROW_TILE and n_lat % ROW_TILE == 0 and n_batch < 8

    bounds = np.cumsum((0,) + IN_SIZES)
    assert bounds[2] == 4 * GDN_WIDTH and bounds[4] - bounds[2] <= LANES
    w_all = w_in[0].astype(BF16)
    piece = lambda j0, j1: w_all[:, bounds[j0]:bounds[j1]]
    w_qv_t = jnp.concatenate([piece(4, 5).T, piece(6, 7).T], axis=0)
    w_k = piece(5, 6)
    w_zg = piece(7, 9)
    cc = _pad_rows(jnp.concatenate([c, c_ctx[None, :]], axis=0))
    conv_w8 = _pad_rows(conv_w[0])
    gpar = _pad_rows(_pad_lanes(jnp.stack([a_log[0].reshape(-1), dt_bias[0].reshape(-1)])))
    gains = _pad_rows(jnp.stack([jnp.tile(q_norm_w[0], 2), jnp.tile(k_norm_w[0], 2)]))
    qgain_t = jnp.broadcast_to((jnp.tile(q_norm_w[0], 2) * (DIFF_DQK ** -0.5 * LOG2E))[:, None],
                               (HEAD_W, ROW_TILE))
    lam_par = _pad_rows(_pad_lanes(jnp.stack([lambda_q1[0], lambda_k1[0], lambda_q2[0], lambda_k2[0]])))
    cos_tab, sin_tab, cos_t, sin_t = _rope_tables(n_lat, n_ctx)

    mod = _ada_modulation(cc, w_ada[0], b_ada)
    q_a, k_a, v_a, gb, z_a, q_t, kn, v_t, z_b, gates = _input_projection(
        x, ctx, mod, w_all, w_qv_t, w_k, w_zg, cos_tab, sin_tab, cos_t, sin_t, gains, qgain_t, conv_w8, gpar)
    o_f, o_b = _gdn_scan(q_a, k_a, v_a, gb, n_lat)
    y_b = _diff_attention(q_t, kn, v_t, z_b, lam_par, _pad_rows(diff_norm_w), n_lat)
    return _mixer_output(x, o_f, o_b, z_a, y_b, gates, mod, _pad_rows(gdn_norm_w),
                         w_oa[0].astype(BF16), w_ob[0].astype(BF16), w_out[0].astype(BF16))
```

```python
import functools
import math

import jax
import jax.numpy as jnp
import numpy as np
from jax import lax
from jax.experimental import pallas as pl
from jax.experimental.pallas import tpu as pltpu

F32 = jnp.float32
BF16 = jnp.bfloat16

D_MODEL = 1024
GRID_W = 64
EPS = 1e-6
HEADS = 8
HEAD_W = 128
GDN_WIDTH = HEADS * HEAD_W
CONV_K = 5
CHUNK = 64
DIFF_DQK = 64
ROPE_THETA = 10000.0
LAMBDA_INIT = 0.8 - 0.6 * math.exp(-0.3 * 0)
IN_SIZES = (3 * GDN_WIDTH, GDN_WIDTH, 2 * HEADS, 2 * HEADS, 1024, 1024, 1024, 1024, 2 * D_MODEL)

ROW_TILE = 256
HALO = 16
LANES = 128
VMEM_LIMIT = 56 * 1024 * 1024


def _bdot(a, b):
    return jnp.dot(a.astype(BF16), b.astype(BF16), preferred_element_type=F32)


def _bdot_nt(a, b):
    return lax.dot_general(a.astype(BF16), b.astype(BF16), (((1,), (1,)), ((), ())),
                           preferred_element_type=F32)


def _bdot_tn(a, b):
    return lax.dot_general(a.astype(BF16), b.astype(BF16), (((0,), (0,)), ((), ())),
                           preferred_element_type=F32)


def _silu(x):
    return x * (1.0 / (1.0 + jnp.exp(-x)))


def _sigmoid(x):
    return 1.0 / (1.0 + jnp.exp(-x))


def _split3(x):
    hi = x.astype(BF16)
    r1 = x - hi.astype(F32)
    mid = r1.astype(BF16)
    lo = (r1 - mid.astype(F32)).astype(BF16)
    return hi, mid, lo


def _ada_kernel(c_ref, w_ref, b_ref, o_ref):
    a = _split3(_silu(c_ref[...]))
    w = _split3(w_ref[...])
    acc = b_ref[...]
    for i, j in ((2, 0), (1, 1), (0, 2), (1, 0), (0, 1), (0, 0)):
        acc = acc + jnp.dot(a[i], w[j], preferred_element_type=F32)
    o_ref[...] = acc


def _ada_modulation(cc, w_ada, b_ada):
    n_col = w_ada.shape[1] // D_MODEL
    return pl.pallas_call(
        _ada_kernel,
        grid=(n_col,),
        in_specs=[pl.BlockSpec((8, D_MODEL), lambda j: (0, 0)),
                  pl.BlockSpec((D_MODEL, D_MODEL), lambda j: (0, j)),
                  pl.BlockSpec((1, D_MODEL), lambda j: (0, j))],
        out_specs=pl.BlockSpec((8, D_MODEL), lambda j: (0, j)),
        out_shape=jax.ShapeDtypeStruct((8, w_ada.shape[1]), F32),
        name="ada_modulation",
    )(cc, w_ada, b_ada)


LOG2E = 1.4426950408889634


def _qk_norm_rope(x, gain, cos, sin):
    lane = lax.broadcasted_iota(jnp.int32, x.shape, 1)
    first_comp = lane < DIFF_DQK
    first_half = (lane % (DIFF_DQK // 2)) < (DIFF_DQK // 4)
    xx = x * x
    s1 = jnp.sum(jnp.where(first_comp, xx, 0.0), axis=-1, keepdims=True)
    s2 = jnp.sum(jnp.where(first_comp, 0.0, xx), axis=-1, keepdims=True)
    ms = jnp.where(first_comp, s1, s2) * (1.0 / DIFF_DQK)
    t = x * lax.rsqrt(ms + EPS) * gain
    rot = jnp.where(first_half, -pltpu.roll(t, LANES - DIFF_DQK // 4, 1), pltpu.roll(t, DIFF_DQK // 4, 1))
    return t * cos + rot * sin


def _inproj_kernel(n_lat_tiles, n_batch, x_ref, ctx_ref, mod_ref, wl_ref, wqvt_ref, wk_ref, wzg_ref,
                   cos_ref, sin_ref, cost_ref, sint_ref, gain_ref, qgain_ref, convw_ref, gpar_ref,
                   qa_ref, ka_ref, va_ref, gb_ref, za_ref, qt_ref, kn_ref, vt_ref, zb_ref, gates_ref, xp_ref):
    b = pl.program_id(0)
    i = pl.program_id(1)

    @pl.when(i == 0)
    def _():
        xp_ref[...] = jnp.zeros_like(xp_ref)

    is_ctx = i >= n_lat_tiles
    xt = jnp.where(is_ctx, ctx_ref[0], x_ref[0])
    mod = mod_ref[pl.ds(jnp.where(is_ctx, n_batch, b), 1), :]
    shift = mod[:, 0:D_MODEL]
    scale = mod[:, D_MODEL:2 * D_MODEL]
    ms = jnp.mean(xt * xt, axis=-1, keepdims=True)
    h = (xt * lax.rsqrt(ms + EPS) * (1.0 + scale) + shift).astype(BF16)
    proj = lambda w_ref, c0, cw: jnp.dot(h, w_ref[:, c0:c0 + cw], preferred_element_type=F32)
    head_cols = lambda hd: slice(hd * HEAD_W, (hd + 1) * HEAD_W)

    cur_ends_seq = (i == n_lat_tiles) | (i == n_lat_tiles + 1)
    next_starts_seq = i == n_lat_tiles
    pad = CONV_K // 2
    for which, o_ref in enumerate((qa_ref, ka_ref, va_ref)):
        res = proj(wl_ref, which * GDN_WIDTH, GDN_WIDTH)
        for hd in range(HEADS):
            cols = slice(which * GDN_WIDTH + hd * HEAD_W, which * GDN_WIDTH + (hd + 1) * HEAD_W)
            xp_ref[HALO + ROW_TILE:, cols] = jnp.where(cur_ends_seq, 0.0, res[0:HALO, head_cols(hd)])
            slab = xp_ref[:, cols].reshape(-1, 8, HEAD_W)
            sub = lax.broadcasted_iota(jnp.int32, slab.shape, 1)
            y = None
            for j in range(CONV_K):
                s = pad - j
                if s == 0:
                    shifted = slab
                else:
                    rot = pltpu.roll(slab, s % 8, 1)
                    if s > 0:
                        shifted = jnp.where(sub >= s, rot, jnp.concatenate([rot[-1:], rot[:-1]], axis=0))
                    else:
                        shifted = jnp.where(sub < 8 + s, rot, jnp.concatenate([rot[1:], rot[:1]], axis=0))
                body = shifted[HALO // 8:(HALO + ROW_TILE) // 8].reshape(ROW_TILE, HEAD_W)
                term = body * convw_ref[j:j + 1, cols]
                y = term if y is None else y + term
            y = _silu(y)
            if which < 2:
                y = y * lax.rsqrt(jnp.sum(y * y, axis=-1, keepdims=True) + EPS)
            if which == 0:
                y = y * (HEAD_W ** -0.5)
            o_ref[0, :, head_cols(hd)] = y.astype(o_ref.dtype)
            xp_ref[0:HALO, cols] = jnp.where(next_starts_seq, 0.0, xp_ref[ROW_TILE:ROW_TILE + HALO, cols])
            xp_ref[HALO:HALO + ROW_TILE, cols] = res[:, head_cols(hd)]
    za_ref[0] = proj(wl_ref, 3 * GDN_WIDTH, GDN_WIDTH).astype(za_ref.dtype)

    ab = proj(wl_ref, 4 * GDN_WIDTH, LANES)
    lane = lax.broadcasted_iota(jnp.int32, ab.shape, 1)
    xs = ab + gpar_ref[1:2, :]
    softplus = jnp.maximum(xs, 0.0) + jnp.log1p(jnp.exp(-jnp.abs(xs)))
    g_all = -jnp.exp(gpar_ref[0:1, :]) * softplus
    gb_ref[0] = jnp.where(lane < 2 * HEADS, g_all, jnp.where(lane < 4 * HEADS, _sigmoid(ab), 0.0))

    nt = (((1,), (1,)), ((), ()))
    q_t = lax.dot_general(wqvt_ref[0:D_MODEL, :], h, nt, preferred_element_type=F32)
    cos_t = cost_ref[...]
    sin_t = sint_ref[...]
    half_rows = DIFF_DQK // 4
    for hd in range(HEADS):
        t = q_t[head_cols(hd), :]
        tt = t * t
        r = jnp.concatenate(
            [jnp.broadcast_to(lax.rsqrt(jnp.mean(tt[c * DIFF_DQK:(c + 1) * DIFF_DQK], axis=0, keepdims=True) + EPS),
                              (DIFF_DQK, ROW_TILE)) for c in range(2)], axis=0)
        tn = t * r * qgain_ref[...]
        blocks = [tn[b * half_rows:(b + 1) * half_rows] for b in range(HEAD_W // half_rows)]
        partner = jnp.concatenate([blocks[b ^ 1] for b in range(len(blocks))], axis=0)
        qt_ref[0, head_cols(hd), :] = (tn * cos_t + partner * sin_t).astype(qt_ref.dtype)
    vt_ref[0] = lax.dot_general(wqvt_ref[D_MODEL:2 * D_MODEL, :], h, nt,
                                preferred_element_type=F32).astype(vt_ref.dtype)
    cos = cos_ref[...]
    sin = sin_ref[...]
    res = proj(wk_ref, 0, D_MODEL)
    for hd in range(HEADS):
        kn_ref[0, :, head_cols(hd)] = _qk_norm_rope(
            res[:, head_cols(hd)], gain_ref[1:2, :], cos, sin).astype(kn_ref.dtype)
    zb_ref[0] = proj(wzg_ref, 0, D_MODEL).astype(zb_ref.dtype)
    for c0 in range(0, 2 * D_MODEL, D_MODEL):
        gates_ref[0, :, c0:c0 + D_MODEL] = proj(wzg_ref, D_MODEL + c0, D_MODEL).astype(gates_ref.dtype)


def _input_projection(x, ctx, mod, w_all, w_qv_t, w_k, w_zg, cos_tab, sin_tab, cos_t, sin_t, gains, qgain_t,
                      conv_w8, gpar):
    n_batch, n_lat, _ = x.shape
    n_lat_tiles = n_lat // ROW_TILE
    n_tiles = n_lat_tiles + 1
    n_tok = n_lat + ctx.shape[1]
    last = n_tiles - 1
    now = lambda b, i: (b, jnp.minimum(i, last), 0)
    now_t = lambda b, i: (b, 0, jnp.minimum(i, last))
    trailing = lambda b, i: (b, jnp.maximum(i - 1, 0), 0)
    tok = lambda w, dt: jax.ShapeDtypeStruct((n_batch, n_tok, w), dt)
    feat = jax.ShapeDtypeStruct((n_batch, D_MODEL, n_tok), BF16)
    out_shape = [tok(GDN_WIDTH, BF16)] * 3 + [tok(LANES, F32), tok(GDN_WIDTH, BF16), feat,
                                              tok(D_MODEL, BF16), feat, tok(D_MODEL, BF16),
                                              tok(2 * D_MODEL, BF16)]
    tok_spec = lambda w: pl.BlockSpec((1, ROW_TILE, w), now)
    feat_spec = pl.BlockSpec((1, D_MODEL, ROW_TILE), now_t)
    out_specs = ([pl.BlockSpec((1, ROW_TILE, GDN_WIDTH), trailing)] * 3
                 + [tok_spec(LANES), tok_spec(GDN_WIDTH), feat_spec, tok_spec(D_MODEL), feat_spec,
                    tok_spec(D_MODEL), tok_spec(2 * D_MODEL)])
    resident = lambda a: pl.BlockSpec(a.shape, lambda b, i: (0, 0), pipeline_mode=pl.Buffered(1))
    tab_spec = pl.BlockSpec((ROW_TILE, LANES), lambda b, i: (jnp.minimum(i, last), 0))
    tab_t_spec = pl.BlockSpec((LANES, ROW_TILE), lambda b, i: (0, jnp.minimum(i, last)))
    small = lambda a: pl.BlockSpec(a.shape, lambda b, i: (0, 0))
    return pl.pallas_call(
        functools.partial(_inproj_kernel, n_lat_tiles, n_batch),
        grid=(n_batch, n_tiles + 1),
        in_specs=[
            pl.BlockSpec((1, ROW_TILE, D_MODEL), lambda b, i: (b, jnp.minimum(i, n_lat_tiles - 1), 0)),
            pl.BlockSpec((1, ROW_TILE, D_MODEL), lambda b, i: (b, 0, 0)),
            small(mod),
            pl.BlockSpec((D_MODEL, 4 * GDN_WIDTH + LANES), lambda b, i: (0, 0), pipeline_mode=pl.Buffered(1)),
            resident(w_qv_t), resident(w_k), resident(w_zg),
            tab_spec, tab_spec, tab_t_spec, tab_t_spec, small(gains), small(qgain_t), small(conv_w8), small(gpar),
        ],
        out_specs=out_specs,
        out_shape=out_shape,
        scratch_shapes=[pltpu.VMEM((ROW_TILE + 2 * HALO, 3 * GDN_WIDTH), F32)],
        compiler_params=pltpu.CompilerParams(
            dimension_semantics=("arbitrary", "arbitrary"), vmem_limit_bytes=VMEM_LIMIT),
        name="input_projection",
    )(x, ctx, mod, w_all, w_qv_t, w_k, w_zg, cos_tab, sin_tab, cos_t, sin_t, gains, qgain_t, conv_w8, gpar)


GROUP = 1
GSIZE = GROUP * CHUNK
N_LEVELS = 5
M_EYE, M_PAIR, M_LEVEL0 = 0, 1, 2
M_INCL = (M_LEVEL0 + N_LEVELS, M_LEVEL0 + N_LEVELS + 2)
M_STRICT = (M_LEVEL0 + N_LEVELS + 1, M_LEVEL0 + N_LEVELS + 3)
SCAN_CHUNKS = 4
SCAN_ROWS = SCAN_CHUNKS * CHUNK


def _scan_masks():
    r = np.arange(GSIZE)[:, None]
    c = np.arange(GSIZE)[None, :]
    blk = (r // CHUNK) == (c // CHUNK)
    planes = [r == c, (r >> 1) == (c >> 1)]
    for shift in range(1, N_LEVELS + 1):
        planes.append(((r >> (shift + 1)) == (c >> (shift + 1))) & ((r >> shift) != (c >> shift)))
    planes += [blk & (r >= c), blk & (r > c), blk & (r <= c), blk & (r < c)]
    return np.stack(planes).astype(np.float32)


def _gdn_scan_kernel(qf_ref, kf_ref, vf_ref, gbf_ref, qb_ref, kb_ref, vb_ref, gbb_ref, mask_ref,
                     of_ref, ob_ref, s_ref):
    @pl.when(pl.program_id(1) == 0)
    def _():
        s_ref[...] = jnp.zeros_like(s_ref)

    rows_of = lambda c: slice(c * CHUNK, (c + 1) * CHUNK)
    stack_rows = lambda parts: jnp.concatenate(parts, axis=0)

    groups = []
    for d, (q_ref, k_ref, v_ref, gb_ref, o_ref) in enumerate(
            ((qf_ref, kf_ref, vf_ref, gbf_ref, of_ref), (qb_ref, kb_ref, vb_ref, gbb_ref, ob_ref))):
        last = CHUNK - 1 if d == 0 else 0
        tri = mask_ref[M_INCL[d], 0:CHUNK, 0:CHUNK].astype(BF16)
        for pos in range(SCAN_CHUNKS):
            local = pos if d == 0 else SCAN_CHUNKS - 1 - pos
            rows = slice(local * CHUNK, (local + 1) * CHUNK)
            gb = gb_ref[0, rows, :]
            gc = None
            for part in _split3(gb):
                term = jnp.dot(tri, part, preferred_element_type=F32)
                gc = term if gc is None else gc + term
            gc_t = gc.T
            for half in range(HEADS // GROUP):
                heads = [half * GROUP + c for c in range(GROUP)]
                lanes = [d * HEADS + h for h in heads]
                head_cols = [slice(h * HEAD_W, (h + 1) * HEAD_W) for h in heads]
                groups.append(dict(
                    d=d, pos=pos, rows=rows, lanes=lanes, head_cols=head_cols, o_ref=o_ref,
                    q=stack_rows([q_ref[0, rows, cs].astype(F32) for cs in head_cols]),
                    k=stack_rows([k_ref[0, rows, cs].astype(F32) for cs in head_cols]),
                    v=stack_rows([v_ref[0, rows, cs].astype(F32) for cs in head_cols]),
                    g_col=stack_rows([gc[:, l:l + 1] for l in lanes]),
                    g_row=jnp.concatenate([gc_t[l:l + 1, :] for l in lanes], axis=1),
                    g_last=stack_rows([jnp.broadcast_to(gc[last:last + 1, l:l + 1], (CHUNK, 1))
                                       for l in lanes]),
                    beta=stack_rows([gb[:, 2 * HEADS + l:2 * HEADS + l + 1] for l in lanes])))

    for g in groups:
        incl = mask_ref[M_INCL[g["d"]]]
        decay = jnp.exp((g["g_col"] - g["g_row"]) * incl) * incl
        kq = _bdot_nt(stack_rows([g["k"] * g["beta"], g["q"]]), g["k"])
        g["a"] = kq[:GSIZE] * decay * mask_ref[M_STRICT[g["d"]]]
        g["qk"] = kq[GSIZE:] * decay
        g["t"] = mask_ref[M_EYE] - g["a"] * mask_ref[M_PAIR]
    for level in range(N_LEVELS):
        for g in groups:
            g["x"] = _bdot(g["a"] * mask_ref[M_LEVEL0 + level], g["t"])
        for g in groups:
            g["t"] = g["t"] - _bdot(g["t"], g["x"])
    for g in groups:
        e_g = jnp.exp(g["g_col"])
        uw = _bdot(g["t"], jnp.concatenate([g["v"] * g["beta"], g["k"] * (g["beta"] * e_g)], axis=1))
        g["u"] = uw[:, :HEAD_W]
        g["w"] = uw[:, HEAD_W:]
        g["qd"] = g["q"] * e_g
        g["k_dec"] = g["k"] * jnp.exp(g["g_last"] - g["g_col"])

    state = [s_ref[lane] for lane in range(2 * HEADS)]
    for pos in range(SCAN_CHUNKS):
        now = [g for g in groups if g["pos"] == pos]
        for g in now:
            g["ws_qs"] = [_bdot(stack_rows([g["w"][rows_of(c)], g["qd"][rows_of(c)]]), state[lane])
                          for c, lane in enumerate(g["lanes"])]
        for g in now:
            g["v_new"] = [g["u"][rows_of(c)] - g["ws_qs"][c][:CHUNK] for c in range(GROUP)]
        for g in now:
            for c, lane in enumerate(g["lanes"]):
                state[lane] = (state[lane] * jnp.exp(g["g_last"][c * CHUNK:c * CHUNK + 1, :])
                               + _bdot_tn(g["k_dec"][rows_of(c)], g["v_new"][c]))
        for g in now:
            o = (stack_rows([g["ws_qs"][c][CHUNK:] for c in range(GROUP)])
                 + _bdot(g["qk"], stack_rows(g["v_new"])))
            for c, cs in enumerate(g["head_cols"]):
                g["o_ref"][0, g["rows"], cs] = o[rows_of(c)].astype(g["o_ref"].dtype)
    for lane in range(2 * HEADS):
        s_ref[lane] = state[lane]


def _gdn_scan(q, k, v, gb, n_lat):
    n_batch, n_tok, _ = q.shape
    assert n_lat % SCAN_ROWS == 0 and (n_tok - n_lat) % SCAN_ROWS == 0
    n_lat_blocks = n_lat // SCAN_ROWS
    n_ctx_blocks = (n_tok - n_lat) // SCAN_ROWS
    n_steps = n_tok // SCAN_ROWS

    def fwd_block(s):
        return jnp.where(s < n_ctx_blocks, n_lat_blocks + s, s - n_ctx_blocks)

    def bwd_block(s):
        return jnp.where(s < n_ctx_blocks, n_lat_blocks + n_ctx_blocks - 1 - s,
                         n_lat_blocks - 1 - (s - n_ctx_blocks))

    f_spec = lambda w: pl.BlockSpec((1, SCAN_ROWS, w), lambda b, s: (b, fwd_block(s), 0))
    b_spec = lambda w: pl.BlockSpec((1, SCAN_ROWS, w), lambda b, s: (b, bwd_block(s), 0))
    masks = jnp.asarray(_scan_masks())
    return pl.pallas_call(
        _gdn_scan_kernel,
        grid=(n_batch, n_steps),
        in_specs=[f_spec(GDN_WIDTH)] * 3 + [f_spec(LANES)] + [b_spec(GDN_WIDTH)] * 3 + [b_spec(LANES)]
        + [pl.BlockSpec(masks.shape, lambda b, s: (0, 0, 0))],
        out_specs=[f_spec(GDN_WIDTH), b_spec(GDN_WIDTH)],
        out_shape=[jax.ShapeDtypeStruct((n_batch, n_tok, GDN_WIDTH), BF16)] * 2,
        scratch_shapes=[pltpu.VMEM((2 * HEADS, HEAD_W, HEAD_W), F32)],
        compiler_params=pltpu.CompilerParams(
            dimension_semantics=("arbitrary", "arbitrary"), vmem_limit_bytes=VMEM_LIMIT),
        name="gdn_scan",
    )(q, k, v, gb, q, k, v, gb, masks)


ATT_TQ = 1024
ATT_TK = 768
ATT_VMEM_LIMIT = 62 * 1024 * 1024


BOUND_LIMIT = 50.0


def _diff_attn_kernel(n_kv, q_ref, k_ref, vt_ref, z_ref, lam_ref, normw_ref, o_ref, kx_ref, kmax_ref):
    tq = q_ref.shape[2]

    @pl.when(pl.program_id(2) == 0)
    def _():
        lane_k = lax.broadcasted_iota(jnp.int32, (ATT_TK, HEAD_W), 1)
        unit = jnp.where(lane_k == 0, 1.0, 0.0).astype(BF16)

        def widen(j, carry):
            n1, n2 = carry
            rows = pl.ds(pl.multiple_of(j * ATT_TK, ATT_TK), ATT_TK)
            kb = k_ref[0, rows, :]
            kx_ref[rows, 0:HEAD_W] = kb
            kx_ref[rows, HEAD_W:2 * HEAD_W] = unit
            sq = kb.astype(F32) * kb.astype(F32)
            a1 = jnp.sum(jnp.where(lane_k < DIFF_DQK, sq, 0.0), axis=1, keepdims=True)
            a2 = jnp.sum(jnp.where(lane_k < DIFF_DQK, 0.0, sq), axis=1, keepdims=True)
            return (jnp.maximum(n1, jnp.max(a1, axis=0, keepdims=True)),
                    jnp.maximum(n2, jnp.max(a2, axis=0, keepdims=True)))

        zero11 = jnp.zeros((1, 1), F32)
        n1, n2 = lax.fori_loop(0, n_kv, widen, (zero11, zero11))
        kmax_ref[0:1, :] = jnp.broadcast_to(jnp.sqrt(n1), (1, LANES))
        kmax_ref[1:2, :] = jnp.broadcast_to(jnp.sqrt(n2), (1, LANES))

    qt = q_ref[0]
    feat = lax.broadcasted_iota(jnp.int32, (HEAD_W, tq), 0)
    first = feat < DIFF_DQK
    zero = jnp.zeros_like(qt)
    qs_t = jnp.concatenate([jnp.where(first, qt, zero), jnp.where(first, zero, qt)], axis=1)
    sq = qt.astype(F32) * qt.astype(F32)
    c1 = jnp.sqrt(jnp.sum(jnp.where(first, sq, 0.0), axis=0, keepdims=True)) * kmax_ref[0:1, 0:1]
    c2 = jnp.sqrt(jnp.sum(jnp.where(first, 0.0, sq), axis=0, keepdims=True)) * kmax_ref[1:2, 0:1]
    c = jnp.concatenate([c1, c2], axis=1)
    bound = jnp.max(c)

    def shifted_by_bound():
        feat2 = lax.broadcasted_iota(jnp.int32, (HEAD_W, 2 * tq), 0)
        aug = jnp.where(feat2 == 0, -c, 0.0).astype(BF16)
        qx_t = jnp.concatenate([qs_t, aug], axis=0)
        l = jnp.zeros((1, 2 * tq), F32)
        acc = jnp.zeros((HEAD_W, 2 * tq), F32)
        for j in range(n_kv):
            rows = slice(j * ATT_TK, (j + 1) * ATT_TK)
            p = jnp.exp2(jnp.dot(kx_ref[rows, :], qx_t, preferred_element_type=F32))
            l = l + jnp.sum(p, axis=0, keepdims=True)
            acc = acc + jnp.dot(vt_ref[0, :, rows], p.astype(BF16), preferred_element_type=F32)
        return acc / l

    def running_max():
        m = jnp.full((1, 2 * tq), -jnp.inf, F32)
        l = jnp.zeros((1, 2 * tq), F32)
        acc = jnp.zeros((HEAD_W, 2 * tq), F32)
        for j in range(n_kv):
            rows = slice(j * ATT_TK, (j + 1) * ATT_TK)
            s = jnp.dot(k_ref[0, rows, :], qs_t, preferred_element_type=F32)
            m_new = jnp.maximum(m, jnp.max(s, axis=0, keepdims=True))
            alpha = jnp.exp2(m - m_new)
            p = jnp.exp2(s - m_new)
            l = alpha * l + jnp.sum(p, axis=0, keepdims=True)
            acc = alpha * acc + jnp.dot(vt_ref[0, :, rows], p.astype(BF16), preferred_element_type=F32)
            m = m_new
        return acc / l

    o_t = lax.cond(bound < BOUND_LIMIT, shifted_by_bound, running_max)
    lam_p = lam_ref[...]
    lam = (jnp.exp(jnp.sum(lam_p[0:1, :] * lam_p[1:2, :], axis=-1, keepdims=True))
           - jnp.exp(jnp.sum(lam_p[2:3, :] * lam_p[3:4, :], axis=-1, keepdims=True)) + LAMBDA_INIT)
    o_t = o_t[:, :tq] - lam * o_t[:, tq:]
    y_t = o_t * lax.rsqrt(jnp.mean(o_t * o_t, axis=0, keepdims=True) + EPS)
    y = y_t.T * normw_ref[0:1, :]
    o_ref[0] = (y * (1.0 - LAMBDA_INIT) * _silu(z_ref[0].astype(F32))).astype(o_ref.dtype)


def _diff_attention(q_t, kn, v_t, z_b, lam_par, norm_w, n_lat):
    n_batch, n_tok, _ = kn.shape
    assert n_lat % ATT_TQ == 0 and n_tok % ATT_TK == 0
    qt_spec = pl.BlockSpec((1, HEAD_W, ATT_TQ), lambda b, h, i: (b, h, i))
    tok_spec = pl.BlockSpec((1, ATT_TQ, HEAD_W), lambda b, h, i: (b, i, h))
    kv_spec = pl.BlockSpec((1, n_tok, HEAD_W), lambda b, h, i: (b, 0, h))
    par_spec = pl.BlockSpec((8, LANES), lambda b, h, i: (0, 0))
    return pl.pallas_call(
        functools.partial(_diff_attn_kernel, n_tok // ATT_TK),
        grid=(n_batch, HEADS, n_lat // ATT_TQ),
        in_specs=[qt_spec, kv_spec, pl.BlockSpec((1, HEAD_W, n_tok), lambda b, h, i: (b, h, 0)), tok_spec,
                  par_spec, par_spec],
        out_specs=tok_spec,
        out_shape=jax.ShapeDtypeStruct((n_batch, n_lat, HEADS * HEAD_W), BF16),
        scratch_shapes=[pltpu.VMEM((n_tok, 2 * HEAD_W), BF16), pltpu.VMEM((8, LANES), F32)],
        compiler_params=pltpu.CompilerParams(
            dimension_semantics=("arbitrary", "arbitrary", "arbitrary"), vmem_limit_bytes=ATT_VMEM_LIMIT),
        name="diff_attention",
    )(q_t, kn, v_t, z_b, lam_par, norm_w)


def _mixer_out_kernel(x_ref, of_ref, ob_ref, za_ref, yb_ref, gates_ref, mod_ref, normw_ref,
                      woa_ref, wob_ref, wout_ref, o_ref, ya_ref):
    b = pl.program_id(0)
    for h in range(HEADS):
        cols = slice(h * HEAD_W, (h + 1) * HEAD_W)
        o_a = of_ref[0, :, cols].astype(F32) + ob_ref[0, :, cols].astype(F32)
        y = o_a * lax.rsqrt(jnp.mean(o_a * o_a, axis=-1, keepdims=True) + EPS) * normw_ref[0:1, :]
        ya_ref[:, cols] = (y * _silu(za_ref[0, :, cols].astype(F32))).astype(BF16)
    gates = _sigmoid(gates_ref[0].astype(F32))
    merged = (gates[:, :D_MODEL] * jnp.dot(ya_ref[...], woa_ref[...], preferred_element_type=F32)
              + gates[:, D_MODEL:] * jnp.dot(yb_ref[0], wob_ref[...], preferred_element_type=F32))
    out = jnp.dot(merged.astype(BF16), wout_ref[...], preferred_element_type=F32)
    gate = mod_ref[pl.ds(b, 1), 2 * D_MODEL:3 * D_MODEL]
    o_ref[0] = x_ref[0] + gate * out


def _mixer_output(x, o_f, o_b, z_a, y_b, gates, mod, gdn_norm_w, w_oa, w_ob, w_out):
    n_batch, n_lat, _ = x.shape
    tok_spec = lambda w: pl.BlockSpec((1, ROW_TILE, w), lambda b, i: (b, i, 0))
    w_spec = pl.BlockSpec((D_MODEL, D_MODEL), lambda b, i: (0, 0))
    return pl.pallas_call(
        _mixer_out_kernel,
        grid=(n_batch, n_lat // ROW_TILE),
        in_specs=[tok_spec(D_MODEL), tok_spec(GDN_WIDTH), tok_spec(GDN_WIDTH), tok_spec(GDN_WIDTH),
                  tok_spec(D_MODEL), tok_spec(2 * D_MODEL),
                  pl.BlockSpec((8, 3 * D_MODEL), lambda b, i: (0, 0)),
                  pl.BlockSpec((8, LANES), lambda b, i: (0, 0)),
                  w_spec, w_spec, w_spec],
        out_specs=tok_spec(D_MODEL),
        out_shape=jax.ShapeDtypeStruct(x.shape, F32),
        scratch_shapes=[pltpu.VMEM((ROW_TILE, GDN_WIDTH), BF16)],
        compiler_params=pltpu.CompilerParams(
            dimension_semantics=("arbitrary", "arbitrary"), vmem_limit_bytes=VMEM_LIMIT),
        name="mixer_output",
    )(x, o_f, o_b, z_a, y_b, gates, mod, gdn_norm_w, w_oa, w_ob, w_out)


def _pad_rows(a, rows=8):
    return jnp.pad(a, ((0, rows - a.shape[0]), (0, 0)))


def _pad_lanes(a, lanes=LANES):
    return jnp.pad(a, ((0, 0), (0, lanes - a.shape[1])))


def _rope_tables(n_lat, n_ctx):
    rows = n_lat // GRID_W
    row = np.broadcast_to(np.arange(rows)[:, None], (rows, GRID_W)).reshape(-1).astype(np.float32)
    col = np.broadcast_to(np.arange(GRID_W)[None, :], (rows, GRID_W)).reshape(-1).astype(np.float32)
    n_freq = DIFF_DQK // 4
    inv_freq = (np.float32(ROPE_THETA) ** (-np.arange(n_freq, dtype=np.float32) / np.float32(n_freq))).astype(np.float32)
    ang_r = row[:, None] * inv_freq
    ang_c = col[:, None] * inv_freq
    ang = np.concatenate([ang_r, ang_r, ang_c, ang_c] * 2, axis=-1).astype(np.float32)
    cos = np.concatenate([np.cos(ang), np.ones((n_ctx, LANES), np.float32)], axis=0)
    sin = np.concatenate([np.sin(ang), np.zeros((n_ctx, LANES), np.float32)], axis=0)
    sign = np.where(np.arange(LANES) % (DIFF_DQK // 2) < DIFF_DQK // 4, -1.0, 1.0).astype(np.float32)
    return (jnp.asarray(cos, F32), jnp.asarray(sin, F32),
            jnp.asarray(np.ascontiguousarray(cos.T), F32), jnp.asarray(np.ascontiguousarray((sin * sign).T), F32))


def kernel(x, c, ctx, c_ctx, w_ada, b_ada, w_in, conv_w, a_log, dt_bias, gdn_norm_w, q_norm_w, k_norm_w,
           lambda_q1, lambda_k1, lambda_q2, lambda_k2, diff_norm_w, w_oa, w_ob, w_out):
    assert w_ada.shape[0] == 1, "single-layer block"
    n_batch, n_lat, _ = x.shape
    n_ctx = ctx.shape[1]
    assert n_ctx == ROW_TILE and n_lat % ROW_TILE == 0 and n_batch < 8

    bounds = np.cumsum((0,) + IN_SIZES)
    assert bounds[2] == 4 * GDN_WIDTH and bounds[4] - bounds[2] <= LANES
    w_all = w_in[0].astype(BF16)
    piece = lambda j0, j1: w_all[:, bounds[j0]:bounds[j1]]
    w_qv_t = jnp.concatenate([piece(4, 5).T, piece(6, 7).T], axis=0)
    w_k = piece(5, 6)
    w_zg = piece(7, 9)
    cc = _pad_rows(jnp.concatenate([c, c_ctx[None, :]], axis=0))
    conv_w8 = _pad_rows(conv_w[0])
    gpar = _pad_rows(_pad_lanes(jnp.stack([a_log[0].reshape(-1), dt_bias[0].reshape(-1)])))
    gains = _pad_rows(jnp.stack([jnp.tile(q_norm_w[0], 2), jnp.tile(k_norm_w[0], 2)]))
    qgain_t = jnp.broadcast_to((jnp.tile(q_norm_w[0], 2) * (DIFF_DQK ** -0.5 * LOG2E))[:, None],
                               (HEAD_W, ROW_TILE))
    lam_par = _pad_rows(_pad_lanes(jnp.stack([lambda_q1[0], lambda_k1[0], lambda_q2[0], lambda_k2[0]])))
    cos_tab, sin_tab, cos_t, sin_t = _rope_tables(n_lat, n_ctx)

    mod = _ada_modulation(cc, w_ada[0], b_ada)
    q_a, k_a, v_a, gb, z_a, q_t, kn, v_t, z_b, gates = _input_projection(
        x, ctx, mod, w_all, w_qv_t, w_k, w_zg, cos_tab, sin_tab, cos_t, sin_t, gains, qgain_t, conv_w8, gpar)
    o_f, o_b = _gdn_scan(q_a, k_a, v_a, gb, n_lat)
    y_b = _diff_attention(q_t, kn, v_t, z_b, lam_par, _pad_rows(diff_norm_w), n_lat)
    return _mixer_output(x, o_f, o_b, z_a, y_b, gates, mod, _pad_rows(gdn_norm_w),
                         w_oa[0].astype(BF16), w_ob[0].astype(BF16), w_out[0].astype(BF16))
```

```python
import functools
import math

import jax
import jax.numpy as jnp
import numpy as np
from jax import lax
from jax.experimental import pallas as pl
from jax.experimental.pallas import tpu as pltpu

F32 = jnp.float32
BF16 = jnp.bfloat16

D_MODEL = 1024
GRID_W = 64
EPS = 1e-6
HEADS = 8
HEAD_W = 128
GDN_WIDTH = HEADS * HEAD_W
CONV_K = 5
CHUNK = 64
DIFF_DQK = 64
ROPE_THETA = 10000.0
LAMBDA_INIT = 0.8 - 0.6 * math.exp(-0.3 * 0)
IN_SIZES = (3 * GDN_WIDTH, GDN_WIDTH, 2 * HEADS, 2 * HEADS, 1024, 1024, 1024, 1024, 2 * D_MODEL)

ROW_TILE = 256
HALO = 16
LANES = 128
VMEM_LIMIT = 56 * 1024 * 1024


def _bdot(a, b):
    return jnp.dot(a.astype(BF16), b.astype(BF16), preferred_element_type=F32)


def _bdot_nt(a, b):
    return lax.dot_general(a.astype(BF16), b.astype(BF16), (((1,), (1,)), ((), ())),
                           preferred_element_type=F32)


def _bdot_tn(a, b):
    return lax.dot_general(a.astype(BF16), b.astype(BF16), (((0,), (0,)), ((), ())),
                           preferred_element_type=F32)


def _silu(x):
    return x * (1.0 / (1.0 + jnp.exp(-x)))


def _sigmoid(x):
    return 1.0 / (1.0 + jnp.exp(-x))


def _split3(x):
    hi = x.astype(BF16)
    r1 = x - hi.astype(F32)
    mid = r1.astype(BF16)
    lo = (r1 - mid.astype(F32)).astype(BF16)
    return hi, mid, lo


def _ada_kernel(c_ref, w_ref, b_ref, o_ref):
    a = _split3(_silu(c_ref[...]))
    w = _split3(w_ref[...])
    acc = b_ref[...]
    for i, j in ((2, 0), (1, 1), (0, 2), (1, 0), (0, 1), (0, 0)):
        acc = acc + jnp.dot(a[i], w[j], preferred_element_type=F32)
    o_ref[...] = acc


def _ada_modulation(cc, w_ada, b_ada):
    n_col = w_ada.shape[1] // D_MODEL
    return pl.pallas_call(
        _ada_kernel,
        grid=(n_col,),
        in_specs=[pl.BlockSpec((8, D_MODEL), lambda j: (0, 0)),
                  pl.BlockSpec((D_MODEL, D_MODEL), lambda j: (0, j)),
                  pl.BlockSpec((1, D_MODEL), lambda j: (0, j))],
        out_specs=pl.BlockSpec((8, D_MODEL), lambda j: (0, j)),
        out_shape=jax.ShapeDtypeStruct((8, w_ada.shape[1]), F32),
        name="ada_modulation",
    )(cc, w_ada, b_ada)


LOG2E = 1.4426950408889634


def _qk_norm_rope(x, gain, cos, sin):
    lane = lax.broadcasted_iota(jnp.int32, x.shape, 1)
    first_comp = lane < DIFF_DQK
    first_half = (lane % (DIFF_DQK // 2)) < (DIFF_DQK // 4)
    xx = x * x
    s1 = jnp.sum(jnp.where(first_comp, xx, 0.0), axis=-1, keepdims=True)
    s2 = jnp.sum(jnp.where(first_comp, 0.0, xx), axis=-1, keepdims=True)
    ms = jnp.where(first_comp, s1, s2) * (1.0 / DIFF_DQK)
    t = x * lax.rsqrt(ms + EPS) * gain
    rot = jnp.where(first_half, -pltpu.roll(t, LANES - DIFF_DQK // 4, 1), pltpu.roll(t, DIFF_DQK // 4, 1))
    return t * cos + rot * sin


def _inproj_kernel(n_lat_tiles, n_batch, x_ref, ctx_ref, mod_ref, wl_ref, wqvt_ref, wk_ref, wzg_ref,
                   cos_ref, sin_ref, cost_ref, sint_ref, gain_ref, qgain_ref, convw_ref, gpar_ref,
                   qa_ref, ka_ref, va_ref, gb_ref, za_ref, qt_ref, kn_ref, vt_ref, zb_ref, gates_ref, xp_ref):
    b = pl.program_id(0)
    i = pl.program_id(1)

    @pl.when(i == 0)
    def _():
        xp_ref[...] = jnp.zeros_like(xp_ref)

    is_ctx = i >= n_lat_tiles
    xt = jnp.where(is_ctx, ctx_ref[0], x_ref[0])
    mod = mod_ref[pl.ds(jnp.where(is_ctx, n_batch, b), 1), :]
    shift = mod[:, 0:D_MODEL]
    scale = mod[:, D_MODEL:2 * D_MODEL]
    ms = jnp.mean(xt * xt, axis=-1, keepdims=True)
    h = (xt * lax.rsqrt(ms + EPS) * (1.0 + scale) + shift).astype(BF16)
    proj = lambda w_ref, c0, cw: jnp.dot(h, w_ref[:, c0:c0 + cw], preferred_element_type=F32)
    head_cols = lambda hd: slice(hd * HEAD_W, (hd + 1) * HEAD_W)

    cur_ends_seq = (i == n_lat_tiles) | (i == n_lat_tiles + 1)
    next_starts_seq = i == n_lat_tiles
    pad = CONV_K // 2
    for which, o_ref in enumerate((qa_ref, ka_ref, va_ref)):
        res = proj(wl_ref, which * GDN_WIDTH, GDN_WIDTH)
        for hd in range(HEADS):
            cols = slice(which * GDN_WIDTH + hd * HEAD_W, which * GDN_WIDTH + (hd + 1) * HEAD_W)
            xp_ref[HALO + ROW_TILE:, cols] = jnp.where(cur_ends_seq, 0.0, res[0:HALO, head_cols(hd)])
            slab = xp_ref[:, cols].reshape(-1, 8, HEAD_W)
            sub = lax.broadcasted_iota(jnp.int32, slab.shape, 1)
            y = None
            for j in range(CONV_K):
                s = pad - j
                if s == 0:
                    shifted = slab
                else:
                    rot = pltpu.roll(slab, s % 8, 1)
                    if s > 0:
                        shifted = jnp.where(sub >= s, rot, jnp.concatenate([rot[-1:], rot[:-1]], axis=0))
                    else:
                        shifted = jnp.where(sub < 8 + s, rot, jnp.concatenate([rot[1:], rot[:1]], axis=0))
                body = shifted[HALO // 8:(HALO + ROW_TILE) // 8].reshape(ROW_TILE, HEAD_W)
                term = body * convw_ref[j:j + 1, cols]
                y = term if y is None else y + term
            y = _silu(y)
            if which < 2:
                y = y * lax.rsqrt(jnp.sum(y * y, axis=-1, keepdims=True) + EPS)
            if which == 0:
                y = y * (HEAD_W ** -0.5)
            o_ref[0, :, head_cols(hd)] = y.astype(o_ref.dtype)
            xp_ref[0:HALO, cols] = jnp.where(next_starts_seq, 0.0, xp_ref[ROW_TILE:ROW_TILE + HALO, cols])
            xp_ref[HALO:HALO + ROW_TILE, cols] = res[:, head_cols(hd)]
    za_ref[0] = proj(wl_ref, 3 * GDN_WIDTH, GDN_WIDTH).astype(za_ref.dtype)

    ab = proj(wl_ref, 4 * GDN_WIDTH, LANES)
    lane = lax.broadcasted_iota(jnp.int32, ab.shape, 1)
    xs = ab + gpar_ref[1:2, :]
    softplus = jnp.maximum(xs, 0.0) + jnp.log1p(jnp.exp(-jnp.abs(xs)))
    g_all = -jnp.exp(gpar_ref[0:1, :]) * softplus
    gb_ref[0] = jnp.where(lane < 2 * HEADS, g_all, jnp.where(lane < 4 * HEADS, _sigmoid(ab), 0.0))

    nt = (((1,), (1,)), ((), ()))
    q_t = lax.dot_general(wqvt_ref[0:D_MODEL, :], h, nt, preferred_element_type=F32)
    cos_t = cost_ref[...]
    sin_t = sint_ref[...]
    half_rows = DIFF_DQK // 4
    for hd in range(HEADS):
        t = q_t[head_cols(hd), :]
        tt = t * t
        r = jnp.concatenate(
            [jnp.broadcast_to(lax.rsqrt(jnp.mean(tt[c * DIFF_DQK:(c + 1) * DIFF_DQK], axis=0, keepdims=True) + EPS),
                              (DIFF_DQK, ROW_TILE)) for c in range(2)], axis=0)
        tn = t * r * qgain_ref[...]
        blocks = [tn[b * half_rows:(b + 1) * half_rows] for b in range(HEAD_W // half_rows)]
        partner = jnp.concatenate([blocks[b ^ 1] for b in range(len(blocks))], axis=0)
        qt_ref[0, head_cols(hd), :] = (tn * cos_t + partner * sin_t).astype(qt_ref.dtype)
    vt_ref[0] = lax.dot_general(wqvt_ref[D_MODEL:2 * D_MODEL, :], h, nt,
                                preferred_element_type=F32).astype(vt_ref.dtype)
    cos = cos_ref[...]
    sin = sin_ref[...]
    res = proj(wk_ref, 0, D_MODEL)
    for hd in range(HEADS):
        kn_ref[0, :, head_cols(hd)] = _qk_norm_rope(
            res[:, head_cols(hd)], gain_ref[1:2, :], cos, sin).astype(kn_ref.dtype)
    zb_ref[0] = proj(wzg_ref, 0, D_MODEL).astype(zb_ref.dtype)
    for c0 in range(0, 2 * D_MODEL, D_MODEL):
        gates_ref[0, :, c0:c0 + D_MODEL] = proj(wzg_ref, D_MODEL + c0, D_MODEL).astype(gates_ref.dtype)


def _input_projection(x, ctx, mod, w_all, w_qv_t, w_k, w_zg, cos_tab, sin_tab, cos_t, sin_t, gains, qgain_t,
                      conv_w8, gpar):
    n_batch, n_lat, _ = x.shape
    n_lat_tiles = n_lat // ROW_TILE
    n_tiles = n_lat_tiles + 1
    n_tok = n_lat + ctx.shape[1]
    last = n_tiles - 1
    now = lambda b, i: (b, jnp.minimum(i, last), 0)
    now_t = lambda b, i: (b, 0, jnp.minimum(i, last))
    trailing = lambda b, i: (b, jnp.maximum(i - 1, 0), 0)
    tok = lambda w, dt: jax.ShapeDtypeStruct((n_batch, n_tok, w), dt)
    feat = jax.ShapeDtypeStruct((n_batch, D_MODEL, n_tok), BF16)
    out_shape = [tok(GDN_WIDTH, BF16)] * 3 + [tok(LANES, F32), tok(GDN_WIDTH, BF16), feat,
                                              tok(D_MODEL, BF16), feat, tok(D_MODEL, BF16),
                                              tok(2 * D_MODEL, BF16)]
    tok_spec = lambda w: pl.BlockSpec((1, ROW_TILE, w), now)
    feat_spec = pl.BlockSpec((1, D_MODEL, ROW_TILE), now_t)
    out_specs = ([pl.BlockSpec((1, ROW_TILE, GDN_WIDTH), trailing)] * 3
                 + [tok_spec(LANES), tok_spec(GDN_WIDTH), feat_spec, tok_spec(D_MODEL), feat_spec,
                    tok_spec(D_MODEL), tok_spec(2 * D_MODEL)])
    resident = lambda a: pl.BlockSpec(a.shape, lambda b, i: (0, 0), pipeline_mode=pl.Buffered(1))
    tab_spec = pl.BlockSpec((ROW_TILE, LANES), lambda b, i: (jnp.minimum(i, last), 0))
    tab_t_spec = pl.BlockSpec((LANES, ROW_TILE), lambda b, i: (0, jnp.minimum(i, last)))
    small = lambda a: pl.BlockSpec(a.shape, lambda b, i: (0, 0))
    return pl.pallas_call(
        functools.partial(_inproj_kernel, n_lat_tiles, n_batch),
        grid=(n_batch, n_tiles + 1),
        in_specs=[
            pl.BlockSpec((1, ROW_TILE, D_MODEL), lambda b, i: (b, jnp.minimum(i, n_lat_tiles - 1), 0)),
            pl.BlockSpec((1, ROW_TILE, D_MODEL), lambda b, i: (b, 0, 0)),
            small(mod),
            pl.BlockSpec((D_MODEL, 4 * GDN_WIDTH + LANES), lambda b, i: (0, 0), pipeline_mode=pl.Buffered(1)),
            resident(w_qv_t), resident(w_k), resident(w_zg),
            tab_spec, tab_spec, tab_t_spec, tab_t_spec, small(gains), small(qgain_t), small(conv_w8), small(gpar),
        ],
        out_specs=out_specs,
        out_shape=out_shape,
        scratch_shapes=[pltpu.VMEM((ROW_TILE + 2 * HALO, 3 * GDN_WIDTH), F32)],
        compiler_params=pltpu.CompilerParams(
            dimension_semantics=("arbitrary", "arbitrary"), vmem_limit_bytes=VMEM_LIMIT),
        name="input_projection",
    )(x, ctx, mod, w_all, w_qv_t, w_k, w_zg, cos_tab, sin_tab, cos_t, sin_t, gains, qgain_t, conv_w8, gpar)


GROUP = 1
GSIZE = GROUP * CHUNK
N_LEVELS = 5
M_EYE, M_PAIR, M_LEVEL0 = 0, 1, 2
M_INCL = (M_LEVEL0 + N_LEVELS, M_LEVEL0 + N_LEVELS + 2)
M_STRICT = (M_LEVEL0 + N_LEVELS + 1, M_LEVEL0 + N_LEVELS + 3)
SCAN_CHUNKS = 4
SCAN_ROWS = SCAN_CHUNKS * CHUNK


def _scan_masks():
    r = np.arange(GSIZE)[:, None]
    c = np.arange(GSIZE)[None, :]
    blk = (r // CHUNK) == (c // CHUNK)
    planes = [r == c, (r >> 1) == (c >> 1)]
    for shift in range(1, N_LEVELS + 1):
        planes.append(((r >> (shift + 1)) == (c >> (shift + 1))) & ((r >> shift) != (c >> shift)))
    planes += [blk & (r >= c), blk & (r > c), blk & (r <= c), blk & (r < c)]
    return np.stack(planes).astype(np.float32)


def _gdn_scan_kernel(qf_ref, kf_ref, vf_ref, gbf_ref, qb_ref, kb_ref, vb_ref, gbb_ref, mask_ref,
                     of_ref, ob_ref, s_ref):
    @pl.when(pl.program_id(1) == 0)
    def _():
        s_ref[...] = jnp.zeros_like(s_ref)

    rows_of = lambda c: slice(c * CHUNK, (c + 1) * CHUNK)
    stack_rows = lambda parts: jnp.concatenate(parts, axis=0)

    groups = []
    for d, (q_ref, k_ref, v_ref, gb_ref, o_ref) in enumerate(
            ((qf_ref, kf_ref, vf_ref, gbf_ref, of_ref), (qb_ref, kb_ref, vb_ref, gbb_ref, ob_ref))):
        last = CHUNK - 1 if d == 0 else 0
        tri = mask_ref[M_INCL[d], 0:CHUNK, 0:CHUNK].astype(BF16)
        for pos in range(SCAN_CHUNKS):
            local = pos if d == 0 else SCAN_CHUNKS - 1 - pos
            rows = slice(local * CHUNK, (local + 1) * CHUNK)
            gb = gb_ref[0, rows, :]
            gc = None
            for part in _split3(gb):
                term = jnp.dot(tri, part, preferred_element_type=F32)
                gc = term if gc is None else gc + term
            gc_t = gc.T
            for half in range(HEADS // GROUP):
                heads = [half * GROUP + c for c in range(GROUP)]
                lanes = [d * HEADS + h for h in heads]
                head_cols = [slice(h * HEAD_W, (h + 1) * HEAD_W) for h in heads]
                groups.append(dict(
                    d=d, pos=pos, rows=rows, lanes=lanes, head_cols=head_cols, o_ref=o_ref,
                    q=stack_rows([q_ref[0, rows, cs].astype(F32) for cs in head_cols]),
                    k=stack_rows([k_ref[0, rows, cs].astype(F32) for cs in head_cols]),
                    v=stack_rows([v_ref[0, rows, cs].astype(F32) for cs in head_cols]),
                    g_col=stack_rows([gc[:, l:l + 1] for l in lanes]),
                    g_row=jnp.concatenate([gc_t[l:l + 1, :] for l in lanes], axis=1),
                    g_last=stack_rows([jnp.broadcast_to(gc[last:last + 1, l:l + 1], (CHUNK, 1))
                                       for l in lanes]),
                    beta=stack_rows([gb[:, 2 * HEADS + l:2 * HEADS + l + 1] for l in lanes])))

    for g in groups:
        incl = mask_ref[M_INCL[g["d"]]]
        decay = jnp.exp((g["g_col"] - g["g_row"]) * incl) * incl
        kq = _bdot_nt(stack_rows([g["k"] * g["beta"], g["q"]]), g["k"])
        g["a"] = kq[:GSIZE] * decay * mask_ref[M_STRICT[g["d"]]]
        g["qk"] = kq[GSIZE:] * decay
        g["t"] = mask_ref[M_EYE] - g["a"] * mask_ref[M_PAIR]
    for level in range(N_LEVELS):
        for g in groups:
            g["x"] = _bdot(g["a"] * mask_ref[M_LEVEL0 + level], g["t"])
        for g in groups:
            g["t"] = g["t"] - _bdot(g["t"], g["x"])
    for g in groups:
        e_g = jnp.exp(g["g_col"])
        uw = _bdot(g["t"], jnp.concatenate([g["v"] * g["beta"], g["k"] * (g["beta"] * e_g)], axis=1))
        g["u"] = uw[:, :HEAD_W]
        g["w"] = uw[:, HEAD_W:]
        g["qd"] = g["q"] * e_g
        g["k_dec"] = g["k"] * jnp.exp(g["g_last"] - g["g_col"])

    state = [s_ref[lane] for lane in range(2 * HEADS)]
    for pos in range(SCAN_CHUNKS):
        now = [g for g in groups if g["pos"] == pos]
        for g in now:
            g["ws_qs"] = [_bdot(stack_rows([g["w"][rows_of(c)], g["qd"][rows_of(c)]]), state[lane])
                          for c, lane in enumerate(g["lanes"])]
        for g in now:
            g["v_new"] = [g["u"][rows_of(c)] - g["ws_qs"][c][:CHUNK] for c in range(GROUP)]
        for g in now:
            for c, lane in enumerate(g["lanes"]):
                state[lane] = (state[lane] * jnp.exp(g["g_last"][c * CHUNK:c * CHUNK + 1, :])
                               + _bdot_tn(g["k_dec"][rows_of(c)], g["v_new"][c]))
        for g in now:
            o = (stack_rows([g["ws_qs"][c][CHUNK:] for c in range(GROUP)])
                 + _bdot(g["qk"], stack_rows(g["v_new"])))
            for c, cs in enumerate(g["head_cols"]):
                g["o_ref"][0, g["rows"], cs] = o[rows_of(c)].astype(g["o_ref"].dtype)
    for lane in range(2 * HEADS):
        s_ref[lane] = state[lane]


def _gdn_scan(q, k, v, gb, n_lat):
    n_batch, n_tok, _ = q.shape
    assert n_lat % SCAN_ROWS == 0 and (n_tok - n_lat) % SCAN_ROWS == 0
    n_lat_blocks = n_lat // SCAN_ROWS
    n_ctx_blocks = (n_tok - n_lat) // SCAN_ROWS
    n_steps = n_tok // SCAN_ROWS

    def fwd_block(s):
        return jnp.where(s < n_ctx_blocks, n_lat_blocks + s, s - n_ctx_blocks)

    def bwd_block(s):
        return jnp.where(s < n_ctx_blocks, n_lat_blocks + n_ctx_blocks - 1 - s,
                         n_lat_blocks - 1 - (s - n_ctx_blocks))

    f_spec = lambda w: pl.BlockSpec((1, SCAN_ROWS, w), lambda b, s: (b, fwd_block(s), 0))
    b_spec = lambda w: pl.BlockSpec((1, SCAN_ROWS, w), lambda b, s: (b, bwd_block(s), 0))
    masks = jnp.asarray(_scan_masks())
    return pl.pallas_call(
        _gdn_scan_kernel,
        grid=(n_batch, n_steps),
        in_specs=[f_spec(GDN_WIDTH)] * 3 + [f_spec(LANES)] + [b_spec(GDN_WIDTH)] * 3 + [b_spec(LANES)]
        + [pl.BlockSpec(masks.shape, lambda b, s: (0, 0, 0))],
        out_specs=[f_spec(GDN_WIDTH), b_spec(GDN_WIDTH)],
        out_shape=[jax.ShapeDtypeStruct((n_batch, n_tok, GDN_WIDTH), BF16)] * 2,
        scratch_shapes=[pltpu.VMEM((2 * HEADS, HEAD_W, HEAD_W), F32)],
        compiler_params=pltpu.CompilerParams(
            dimension_semantics=("arbitrary", "arbitrary"), vmem_limit_bytes=VMEM_LIMIT),
        name="gdn_scan",
    )(q, k, v, gb, q, k, v, gb, masks)


ATT_TQ = 1024
ATT_TK = 768
ATT_VMEM_LIMIT = 62 * 1024 * 1024


BOUND_LIMIT = 50.0
NORM_MARGIN = 1.0 + 2.0 ** -7


def _diff_attn_kernel(n_kv, q_ref, k_ref, vt_ref, z_ref, lam_ref, normw_ref, o_ref, kx_ref, kmax_ref):
    tq = q_ref.shape[2]

    @pl.when(pl.program_id(2) == 0)
    def _():
        lane_k = lax.broadcasted_iota(jnp.int32, (ATT_TK, HEAD_W), 1)
        unit = jnp.where(lane_k == 0, 1.0, 0.0).astype(BF16)

        r_i = lax.broadcasted_iota(jnp.int32, (HEAD_W, HEAD_W), 0)
        c_i = lax.broadcasted_iota(jnp.int32, (HEAD_W, HEAD_W), 1)
        comp_ones = jnp.where((r_i < DIFF_DQK) == (c_i < DIFF_DQK), 1.0, 0.0).astype(BF16)

        def widen(j, n):
            rows = pl.ds(pl.multiple_of(j * ATT_TK, ATT_TK), ATT_TK)
            kb = k_ref[0, rows, :]
            kx_ref[rows, 0:HEAD_W] = kb
            kx_ref[rows, HEAD_W:2 * HEAD_W] = unit
            sq = (kb.astype(F32) * kb.astype(F32)).astype(BF16)
            sums = jnp.dot(sq, comp_ones, preferred_element_type=F32)
            return jnp.maximum(n, jnp.max(sums, axis=0, keepdims=True))

        n = lax.fori_loop(0, n_kv, widen, jnp.zeros((1, LANES), F32))
        kmax = jnp.sqrt(n * NORM_MARGIN)
        kmax_ref[0:1, :] = jnp.broadcast_to(kmax[:, 0:1], (1, LANES))
        kmax_ref[1:2, :] = jnp.broadcast_to(kmax[:, DIFF_DQK:DIFF_DQK + 1], (1, LANES))

    qt = q_ref[0]
    feat = lax.broadcasted_iota(jnp.int32, (HEAD_W, tq), 0)
    first = feat < DIFF_DQK
    zero = jnp.zeros_like(qt)
    qs_t = jnp.concatenate([jnp.where(first, qt, zero), jnp.where(first, zero, qt)], axis=1)
    sq = qt.astype(F32) * qt.astype(F32)
    c1 = jnp.sqrt(jnp.sum(jnp.where(first, sq, 0.0), axis=0, keepdims=True)) * kmax_ref[0:1, 0:1]
    c2 = jnp.sqrt(jnp.sum(jnp.where(first, 0.0, sq), axis=0, keepdims=True)) * kmax_ref[1:2, 0:1]
    c = jnp.concatenate([c1, c2], axis=1)
    bound = jnp.max(c)

    def shifted_by_bound():
        feat2 = lax.broadcasted_iota(jnp.int32, (HEAD_W, 2 * tq), 0)
        aug = jnp.where(feat2 == 0, -c, 0.0).astype(BF16)
        qx_t = jnp.concatenate([qs_t, aug], axis=0)
        l = jnp.zeros((1, 2 * tq), F32)
        acc = jnp.zeros((HEAD_W, 2 * tq), F32)
        for j in range(n_kv):
            rows = slice(j * ATT_TK, (j + 1) * ATT_TK)
            p = jnp.exp2(jnp.dot(kx_ref[rows, :], qx_t, preferred_element_type=F32))
            l = l + jnp.sum(p, axis=0, keepdims=True)
            acc = acc + jnp.dot(vt_ref[0, :, rows], p.astype(BF16), preferred_element_type=F32)
        return acc / l

    def running_max():
        m = jnp.full((1, 2 * tq), -jnp.inf, F32)
        l = jnp.zeros((1, 2 * tq), F32)
        acc = jnp.zeros((HEAD_W, 2 * tq), F32)
        for j in range(n_kv):
            rows = slice(j * ATT_TK, (j + 1) * ATT_TK)
            s = jnp.dot(k_ref[0, rows, :], qs_t, preferred_element_type=F32)
            m_new = jnp.maximum(m, jnp.max(s, axis=0, keepdims=True))
            alpha = jnp.exp2(m - m_new)
            p = jnp.exp2(s - m_new)
            l = alpha * l + jnp.sum(p, axis=0, keepdims=True)
            acc = alpha * acc + jnp.dot(vt_ref[0, :, rows], p.astype(BF16), preferred_element_type=F32)
            m = m_new
        return acc / l

    o_t = lax.cond(bound < BOUND_LIMIT, shifted_by_bound, running_max)
    lam_p = lam_ref[...]
    lam = (jnp.exp(jnp.sum(lam_p[0:1, :] * lam_p[1:2, :], axis=-1, keepdims=True))
           - jnp.exp(jnp.sum(lam_p[2:3, :] * lam_p[3:4, :], axis=-1, keepdims=True)) + LAMBDA_INIT)
    o_t = o_t[:, :tq] - lam * o_t[:, tq:]
    y_t = o_t * lax.rsqrt(jnp.mean(o_t * o_t, axis=0, keepdims=True) + EPS)
    y = y_t.T * normw_ref[0:1, :]
    o_ref[0] = (y * (1.0 - LAMBDA_INIT) * _silu(z_ref[0].astype(F32))).astype(o_ref.dtype)


def _diff_attention(q_t, kn, v_t, z_b, lam_par, norm_w, n_lat):
    n_batch, n_tok, _ = kn.shape
    assert n_lat % ATT_TQ == 0 and n_tok % ATT_TK == 0
    qt_spec = pl.BlockSpec((1, HEAD_W, ATT_TQ), lambda b, h, i: (b, h, i))
    tok_spec = pl.BlockSpec((1, ATT_TQ, HEAD_W), lambda b, h, i: (b, i, h))
    kv_spec = pl.BlockSpec((1, n_tok, HEAD_W), lambda b, h, i: (b, 0, h))
    par_spec = pl.BlockSpec((8, LANES), lambda b, h, i: (0, 0))
    return pl.pallas_call(
        functools.partial(_diff_attn_kernel, n_tok // ATT_TK),
        grid=(n_batch, HEADS, n_lat // ATT_TQ),
        in_specs=[qt_spec, kv_spec, pl.BlockSpec((1, HEAD_W, n_tok), lambda b, h, i: (b, h, 0)), tok_spec,
                  par_spec, par_spec],
        out_specs=tok_spec,
        out_shape=jax.ShapeDtypeStruct((n_batch, n_lat, HEADS * HEAD_W), BF16),
        scratch_shapes=[pltpu.VMEM((n_tok, 2 * HEAD_W), BF16), pltpu.VMEM((8, LANES), F32)],
        compiler_params=pltpu.CompilerParams(
            dimension_semantics=("arbitrary", "arbitrary", "arbitrary"), vmem_limit_bytes=ATT_VMEM_LIMIT),
        name="diff_attention",
    )(q_t, kn, v_t, z_b, lam_par, norm_w)


def _mixer_out_kernel(x_ref, of_ref, ob_ref, za_ref, yb_ref, gates_ref, mod_ref, normw_ref,
                      woa_ref, wob_ref, wout_ref, o_ref, ya_ref):
    b = pl.program_id(0)
    for h in range(HEADS):
        cols = slice(h * HEAD_W, (h + 1) * HEAD_W)
        o_a = of_ref[0, :, cols].astype(F32) + ob_ref[0, :, cols].astype(F32)
        y = o_a * lax.rsqrt(jnp.mean(o_a * o_a, axis=-1, keepdims=True) + EPS) * normw_ref[0:1, :]
        ya_ref[:, cols] = (y * _silu(za_ref[0, :, cols].astype(F32))).astype(BF16)
    gates = _sigmoid(gates_ref[0].astype(F32))
    merged = (gates[:, :D_MODEL] * jnp.dot(ya_ref[...], woa_ref[...], preferred_element_type=F32)
              + gates[:, D_MODEL:] * jnp.dot(yb_ref[0], wob_ref[...], preferred_element_type=F32))
    out = jnp.dot(merged.astype(BF16), wout_ref[...], preferred_element_type=F32)
    gate = mod_ref[pl.ds(b, 1), 2 * D_MODEL:3 * D_MODEL]
    o_ref[0] = x_ref[0] + gate * out


def _mixer_output(x, o_f, o_b, z_a, y_b, gates, mod, gdn_norm_w, w_oa, w_ob, w_out):
    n_batch, n_lat, _ = x.shape
    tok_spec = lambda w: pl.BlockSpec((1, ROW_TILE, w), lambda b, i: (b, i, 0))
    w_spec = pl.BlockSpec((D_MODEL, D_MODEL), lambda b, i: (0, 0))
    return pl.pallas_call(
        _mixer_out_kernel,
        grid=(n_batch, n_lat // ROW_TILE),
        in_specs=[tok_spec(D_MODEL), tok_spec(GDN_WIDTH), tok_spec(GDN_WIDTH), tok_spec(GDN_WIDTH),
                  tok_spec(D_MODEL), tok_spec(2 * D_MODEL),
                  pl.BlockSpec((8, 3 * D_MODEL), lambda b, i: (0, 0)),
                  pl.BlockSpec((8, LANES), lambda b, i: (0, 0)),
                  w_spec, w_spec, w_spec],
        out_specs=tok_spec(D_MODEL),
        out_shape=jax.ShapeDtypeStruct(x.shape, F32),
        scratch_shapes=[pltpu.VMEM((ROW_TILE, GDN_WIDTH), BF16)],
        compiler_params=pltpu.CompilerParams(
            dimension_semantics=("arbitrary", "arbitrary"), vmem_limit_bytes=VMEM_LIMIT),
        name="mixer_output",
    )(x, o_f, o_b, z_a, y_b, gates, mod, gdn_norm_w, w_oa, w_ob, w_out)


def _pad_rows(a, rows=8):
    return jnp.pad(a, ((0, rows - a.shape[0]), (0, 0)))


def _pad_lanes(a, lanes=LANES):
    return jnp.pad(a, ((0, 0), (0, lanes - a.shape[1])))


def _rope_tables(n_lat, n_ctx):
    rows = n_lat // GRID_W
    row = np.broadcast_to(np.arange(rows)[:, None], (rows, GRID_W)).reshape(-1).astype(np.float32)
    col = np.broadcast_to(np.arange(GRID_W)[None, :], (rows, GRID_W)).reshape(-1).astype(np.float32)
    n_freq = DIFF_DQK // 4
    inv_freq = (np.float32(ROPE_THETA) ** (-np.arange(n_freq, dtype=np.float32) / np.float32(n_freq))).astype(np.float32)
    ang_r = row[:, None] * inv_freq
    ang_c = col[:, None] * inv_freq
    ang = np.concatenate([ang_r, ang_r, ang_c, ang_c] * 2, axis=-1).astype(np.float32)
    cos = np.concatenate([np.cos(ang), np.ones((n_ctx, LANES), np.float32)], axis=0)
    sin = np.concatenate([np.sin(ang), np.zeros((n_ctx, LANES), np.float32)], axis=0)
    sign = np.where(np.arange(LANES) % (DIFF_DQK // 2) < DIFF_DQK // 4, -1.0, 1.0).astype(np.float32)
    return (jnp.asarray(cos, F32), jnp.asarray(sin, F32),
            jnp.asarray(np.ascontiguousarray(cos.T), F32), jnp.asarray(np.ascontiguousarray((sin * sign).T), F32))


def kernel(x, c, ctx, c_ctx, w_ada, b_ada, w_in, conv_w, a_log, dt_bias, gdn_norm_w, q_norm_w, k_norm_w,
           lambda_q1, lambda_k1, lambda_q2, lambda_k2, diff_norm_w, w_oa, w_ob, w_out):
    assert w_ada.shape[0] == 1, "single-layer block"
    n_batch, n_lat, _ = x.shape
    n_ctx = ctx.shape[1]
    assert n_ctx == ROW_TILE and n_lat % ROW_TILE == 0 and n_batch < 8

    bounds = np.cumsum((0,) + IN_SIZES)
    assert bounds[2] == 4 * GDN_WIDTH and bounds[4] - bounds[2] <= LANES
    w_all = w_in[0].astype(BF16)
    piece = lambda j0, j1: w_all[:, bounds[j0]:bounds[j1]]
    w_qv_t = jnp.concatenate([piece(4, 5).T, piece(6, 7).T], axis=0)
    w_k = piece(5, 6)
    w_zg = piece(7, 9)
    cc = _pad_rows(jnp.concatenate([c, c_ctx[None, :]], axis=0))
    conv_w8 = _pad_rows(conv_w[0])
    gpar = _pad_rows(_pad_lanes(jnp.stack([a_log[0].reshape(-1), dt_bias[0].reshape(-1)])))
    gains = _pad_rows(jnp.stack([jnp.tile(q_norm_w[0], 2), jnp.tile(k_norm_w[0], 2)]))
    qgain_t = jnp.broadcast_to((jnp.tile(q_norm_w[0], 2) * (DIFF_DQK ** -0.5 * LOG2E))[:, None],
                               (HEAD_W, ROW_TILE))
    lam_par = _pad_rows(_pad_lanes(jnp.stack([lambda_q1[0], lambda_k1[0], lambda_q2[0], lambda_k2[0]])))
    cos_tab, sin_tab, cos_t, sin_t = _rope_tables(n_lat, n_ctx)

    mod = _ada_modulation(cc, w_ada[0], b_ada)
    q_a, k_a, v_a, gb, z_a, q_t, kn, v_t, z_b, gates = _input_projection(
        x, ctx, mod, w_all, w_qv_t, w_k, w_zg, cos_tab, sin_tab, cos_t, sin_t, gains, qgain_t, conv_w8, gpar)
    o_f, o_b = _gdn_scan(q_a, k_a, v_a, gb, n_lat)
    y_b = _diff_attention(q_t, kn, v_t, z_b, lam_par, _pad_rows(diff_norm_w), n_lat)
    return _mixer_output(x, o_f, o_b, z_a, y_b, gates, mod, _pad_rows(gdn_norm_w),
                         w_oa[0].astype(BF16), w_ob[0].astype(BF16), w_out[0].astype(BF16))
```
